```python
import jax
import jax.numpy as jnp
from jax import lax
import numpy as np

D_MODEL = 1024
BATCH = 16
SEQ = 4096
DEPTH = 2

CTX_LEN = 256
GRID_W = 64
HEAD_DIM = 64
A_HEADS = 8
A_KV_HEADS = 2
A_GROUP = A_HEADS // A_KV_HEADS
B_HEADS = 4
B_Q_RANK = 192
B_KV_RANK = 128
B_NOPE = 64
B_ROPE = 32
B_V = 64
C_HEADS = 4
C_DK = 64
C_DV = 64
D_FF = 4 * D_MODEL
A_OUT = A_HEADS * HEAD_DIM
B_OUT = B_HEADS * B_V
C_OUT = C_HEADS * C_DV
D_MIX = A_OUT + B_OUT + C_OUT
IN_SIZES = (A_HEADS * HEAD_DIM, A_KV_HEADS * HEAD_DIM, A_KV_HEADS * HEAD_DIM,
            B_Q_RANK, B_KV_RANK, B_ROPE,
            C_HEADS * C_DK, C_HEADS * C_DK, C_HEADS * C_DK, C_HEADS * C_DV, C_HEADS * C_DV)
D_IN = sum(IN_SIZES)
SPLIT_AT = tuple(int(v) for v in np.cumsum(IN_SIZES)[:-1])
Q_BLOCK = 128
SCAN_CHUNK = 64
ROPE_THETA = 10000.0
EPS = 1e-6
F_TINY = 1e-30
F32 = jnp.float32

kernel_name = 'hybrid_parallel_heads_dit_block'


def rms_norm(x, gain):
    xf = x.astype(F32)
    y = xf * lax.rsqrt(jnp.mean(xf * xf, axis=-1, keepdims=True) + EPS)
    return (y * gain.astype(F32)).astype(x.dtype)


def modulate(h, shift, scale):
    return h * (1 + scale) + shift


def axial_rope(row_ids, col_ids, rot_dim):
    n_freq = rot_dim // 4
    inv = ROPE_THETA ** (-jnp.arange(n_freq, dtype=F32) / n_freq)
    ang = jnp.concatenate([row_ids.astype(F32)[:, None] * inv, col_ids.astype(F32)[:, None] * inv], axis=-1)
    return jnp.cos(ang)[:, None, :], jnp.sin(ang)[:, None, :]


def apply_rope(x, rope):
    cos, sin = rope
    xf = x.astype(F32)
    x1, x2 = jnp.split(xf, 2, axis=-1)
    return jnp.concatenate([x1 * cos - x2 * sin, x2 * cos + x1 * sin], axis=-1).astype(x.dtype)


def forget_gate(z, lb):
    zf = z.astype(F32)
    f = lb + (1.0 - lb) * jax.nn.sigmoid(zf)
    log_f = jnp.log(jnp.maximum(f, F_TINY))
    k = (1.0 - lb) * jax.nn.sigmoid(-zf)
    return log_f, k.astype(z.dtype)


def block_attention(q, k, v, scale):
    bsz, t = q.shape[:2]
    nb = t // Q_BLOCK
    qb = jnp.moveaxis(q.reshape(bsz, nb, Q_BLOCK, *q.shape[2:]), 1, 0)

    def one_block(qi):
        s = jnp.einsum('bqhgd,bkhd->bhgqk', qi, k, preferred_element_type=F32) * scale
        p = jax.nn.softmax(s, axis=-1).astype(v.dtype)
        return jnp.einsum('bhgqk,bkhv->bqhgv', p, v)

    out = lax.map(one_block, qb)
    return jnp.moveaxis(out, 0, 1).reshape(bsz, t, *out.shape[3:])


def gla_chunk_scan(q, k, v, g, s0):
    bsz, n_tok, n_h, dk = q.shape
    dv = v.shape[-1]
    n_chunk = n_tok // SCAN_CHUNK

    def to_chunks(a):
        return a.astype(F32).reshape(bsz, n_chunk, SCAN_CHUNK, n_h, a.shape[-1]).transpose(1, 0, 3, 2, 4)

    lower_tri = jnp.tril(jnp.ones((SCAN_CHUNK, SCAN_CHUNK), dtype=bool))[:, :, None]

    def step(state, inp):
        q_, k_, v_, g_ = inp
        b = jnp.cumsum(g_, axis=-2)
        diff = b[..., :, None, :] - b[..., None, :, :]
        decay = jnp.where(lower_tri, jnp.exp(jnp.where(lower_tri, diff, 0.0)), 0.0)
        scores = jnp.einsum('bhtd,bhsd,bhtsd->bhts', q_, k_, decay)
        o = jnp.einsum('bhts,bhsv->bhtv', scores, v_) + jnp.einsum('bhtd,bhdv->bhtv', q_ * jnp.exp(b), state)
        b_last = b[..., -1:, :]
        new_state = state * jnp.exp(b_last[..., 0, :])[..., None] + jnp.einsum('bhsd,bhsv->bhdv', k_ * jnp.exp(b_last - b), v_)
        return new_state, o

    s_fin, o = lax.scan(step, s0, (to_chunks(q), to_chunks(k), to_chunks(v), to_chunks(g)))
    o = o.transpose(1, 0, 3, 2, 4).reshape(bsz, n_tok, n_h, dv)
    return o.astype(v.dtype), s_fin


def mixer_features(h, w_in, a_q_norm, a_k_norm, b_q_norm, w_q_up, b_kv_norm, w_kv_up, lb, rope_a, rope_b):
    bsz, t, _ = h.shape
    aq, ak, av, bqd, bkvd, bkr, cq, cff, cfb, ci, cg = jnp.split(h @ w_in, SPLIT_AT, axis=-1)
    aq = rms_norm(aq.reshape(bsz, t, A_HEADS, HEAD_DIM), a_q_norm)
    ak = rms_norm(ak.reshape(bsz, t, A_KV_HEADS, HEAD_DIM), a_k_norm)
    av = av.reshape(bsz, t, A_KV_HEADS, HEAD_DIM)
    bq = (rms_norm(bqd, b_q_norm) @ w_q_up).reshape(bsz, t, B_HEADS, B_NOPE + B_ROPE)
    bkv = (rms_norm(bkvd, b_kv_norm) @ w_kv_up).reshape(bsz, t, B_HEADS, B_NOPE + B_V)
    bq_nope, bq_pe = jnp.split(bq, [B_NOPE], axis=-1)
    bk_nope, bv = jnp.split(bkv, [B_NOPE], axis=-1)
    bk_pe = bkr[:, :, None, :]
    if rope_a is not None:
        aq = apply_rope(aq, rope_a)
        ak = apply_rope(ak, rope_a)
        bq_pe = apply_rope(bq_pe, rope_b)
        bk_pe = apply_rope(bk_pe, rope_b)
    bq = jnp.concatenate([bq_nope, bq_pe], axis=-1)
    bk = jnp.concatenate([bk_nope, jnp.broadcast_to(bk_pe, (bsz, t, B_HEADS, B_ROPE))], axis=-1)
    heads = lambda a: a.reshape(bsz, t, C_HEADS, -1)
    g_f, k_f = forget_gate(cff, lb[0])
    g_b, k_b = forget_gate(cfb, lb[1])
    return {'aq': aq, 'ak': ak, 'av': av, 'bq': bq, 'bk': bk, 'bv': bv,
            'cq': heads(jax.nn.silu(cq)), 'cv': heads(ci), 'cgate': heads(cg),
            'ck_f': heads(k_f), 'cg_f': heads(g_f), 'ck_b': heads(k_b), 'cg_b': heads(g_b)}


def attention_groups(fq, fkv_list):
    bsz, t = fq['aq'].shape[:2]
    cat = lambda name: jnp.concatenate([f[name] for f in fkv_list], axis=1)
    ya = block_attention(fq['aq'].reshape(bsz, t, A_KV_HEADS, A_GROUP, HEAD_DIM), cat('ak'), cat('av'), HEAD_DIM ** -0.5)
    yb = block_attention(fq['bq'][:, :, :, None, :], cat('bk'), cat('bv'), (B_NOPE + B_ROPE) ** -0.5)
    return ya.reshape(bsz, t, A_OUT), yb.reshape(bsz, t, B_OUT)


def hgrn2_bidirectional(fl, fc, out_norm, need_ctx_out):
    bsz = fl['cq'].shape[0]
    zero = jnp.zeros((bsz, C_HEADS, C_DK, C_DV), F32)
    rev = lambda a: jnp.flip(a, axis=1)
    oc_f, s_f = gla_chunk_scan(fc['cq'], fc['ck_f'], fc['cv'], fc['cg_f'], zero)
    oc_b, s_b = gla_chunk_scan(rev(fc['cq']), rev(fc['ck_b']), rev(fc['cv']), rev(fc['cg_b']), zero)
    ol_f, _ = gla_chunk_scan(fl['cq'], fl['ck_f'], fl['cv'], fl['cg_f'], s_f)
    ol_b, _ = gla_chunk_scan(rev(fl['cq']), rev(fl['ck_b']), rev(fl['cv']), rev(fl['cg_b']), s_b)

    def readout(o, gate):
        y = rms_norm(o, out_norm) * jax.nn.silu(gate)
        return y.reshape(*y.shape[:2], C_OUT)

    y_lat = readout(ol_f + rev(ol_b), fl['cgate'])
    y_ctx = readout(oc_f + rev(oc_b), fc['cgate']) if need_ctx_out else None
    return y_lat, y_ctx


def sq_relu_mlp(h, w1, w2):
    return jnp.square(jax.nn.relu(h @ w1)) @ w2


def setup_inputs(seed: int = 0) -> dict:
    key = jax.random.key(seed)
    ks = jax.random.split(key, 24)
    nrm = lambda k, shape, scale: jax.random.normal(k, shape, F32) * scale
    gain = lambda k, shape: 1.0 + 0.05 * jax.random.normal(k, shape, F32)
    return {
        'x': nrm(ks[0], (BATCH, SEQ, D_MODEL), 1.0),
        'c': nrm(ks[1], (BATCH, D_MODEL), 1.0),
        'ctx': nrm(ks[2], (BATCH, CTX_LEN, D_MODEL), 1.0),
        'c_ctx': nrm(ks[3], (D_MODEL,), 1.0),
        'w_ada': nrm(ks[4], (DEPTH, D_MODEL, 6 * D_MODEL), 0.5 * D_MODEL ** -0.5),
        'b_ada': nrm(ks[5], (DEPTH, 6 * D_MODEL), 0.02),
        'g_pre_mix': gain(ks[6], (DEPTH, D_MODEL)),
        'g_post_mix': gain(ks[7], (DEPTH, D_MODEL)),
        'g_pre_ffn': gain(ks[8], (DEPTH, D_MODEL)),
        'g_post_ffn': gain(ks[9], (DEPTH, D_MODEL)),
        'w_in': nrm(ks[10], (DEPTH, D_MODEL, D_IN), D_MODEL ** -0.5),
        'a_q_norm': gain(ks[11], (DEPTH, HEAD_DIM)),
        'a_k_norm': gain(ks[12], (DEPTH, HEAD_DIM)),
        'b_q_norm': gain(ks[13], (DEPTH, B_Q_RANK)),
        'w_q_up': nrm(ks[14], (DEPTH, B_Q_RANK, B_HEADS * (B_NOPE + B_ROPE)), B_Q_RANK ** -0.5),
        'b_kv_norm': gain(ks[15], (DEPTH, B_KV_RANK)),
        'w_kv_up': nrm(ks[16], (DEPTH, B_KV_RANK, B_HEADS * (B_NOPE + B_V)), B_KV_RANK ** -0.5),
        'c_lower_bounds': nrm(ks[17], (DEPTH, 2, C_HEADS * C_DK), 0.5),
        'c_out_norm': gain(ks[18], (DEPTH, C_DV)),
        'w_out': nrm(ks[19], (DEPTH, D_MIX, D_MODEL), D_MIX ** -0.5),
        'w_ff1': nrm(ks[20], (DEPTH, D_MODEL, D_FF), D_MODEL ** -0.5),
        'w_ff2': nrm(ks[21], (DEPTH, D_FF, D_MODEL), D_FF ** -0.5),
    }


def reference(x, c, ctx, c_ctx, w_ada, b_ada, g_pre_mix, g_post_mix, g_pre_ffn, g_post_ffn,
              w_in, a_q_norm, a_k_norm, b_q_norm, w_q_up, b_kv_norm, w_kv_up,
              c_lower_bounds, c_out_norm, w_out, w_ff1, w_ff2):
    n_lat = x.shape[1]
    rows = n_lat // GRID_W
    row_ids = jnp.repeat(jnp.arange(rows, dtype=jnp.int32), GRID_W)
    col_ids = jnp.tile(jnp.arange(GRID_W, dtype=jnp.int32), rows)
    rope_a = axial_rope(row_ids, col_ids, HEAD_DIM)
    rope_b = axial_rope(row_ids, col_ids, B_ROPE)
    p_lb = jax.nn.softmax(c_lower_bounds.astype(F32), axis=0)
    lower = jnp.cumsum(p_lb, axis=0) - p_lb[:1]

    xc = ctx
    for l in range(DEPTH):
        need_ctx = l < DEPTH - 1
        sh_m, sc_m, gt_m, sh_f, sc_f, gt_f = jnp.split((jax.nn.silu(c) @ w_ada[l] + b_ada[l])[:, None, :], 6, axis=-1)
        csh_m, csc_m, cgt_m, csh_f, csc_f, cgt_f = jnp.split((jax.nn.silu(c_ctx) @ w_ada[l] + b_ada[l])[None, None, :], 6, axis=-1)
        feat_args = (w_in[l], a_q_norm[l], a_k_norm[l], b_q_norm[l], w_q_up[l], b_kv_norm[l], w_kv_up[l], lower[l])

        f_lat = mixer_features(modulate(rms_norm(x, g_pre_mix[l]), sh_m, sc_m), *feat_args, rope_a, rope_b)
        f_ctx = mixer_features(modulate(rms_norm(xc, g_pre_mix[l]), csh_m, csc_m), *feat_args, None, None)
        ya, yb = attention_groups(f_lat, [f_lat, f_ctx])
        yc, yc_ctx = hgrn2_bidirectional(f_lat, f_ctx, c_out_norm[l], need_ctx)
        x = x + gt_m * rms_norm(jnp.concatenate([ya, yb, yc], axis=-1) @ w_out[l], g_post_mix[l])

        x = x + gt_f * rms_norm(sq_relu_mlp(modulate(rms_norm(x, g_pre_ffn[l]), sh_f, sc_f), w_ff1[l], w_ff2[l]), g_post_ffn[l])

        if need_ctx:
            ya_c, yb_c = attention_groups(f_ctx, [f_ctx])
            xc = xc + cgt_m * rms_norm(jnp.concatenate([ya_c, yb_c, yc_ctx], axis=-1) @ w_out[l], g_post_mix[l])
            xc = xc + cgt_f * rms_norm(sq_relu_mlp(modulate(rms_norm(xc, g_pre_ffn[l]), csh_f, csc_f), w_ff1[l], w_ff2[l]), g_post_ffn[l])
    return x
```

```python
import functools

import numpy as np
import jax
import jax.numpy as jnp
from jax import lax
from jax.experimental import pallas as pl
from jax.experimental.pallas import tpu as pltpu

F32 = jnp.float32
BF16 = jnp.bfloat16

D_MODEL = 1024
GRID_W = 64
HEAD_DIM = 64
A_HEADS = 8
A_KV_HEADS = 2
B_HEADS = 4
B_Q_RANK = 192
B_KV_RANK = 128
B_NOPE = 64
B_ROPE = 32
B_V = 64
C_HEADS = 4
C_DK = 64
C_DV = 64
D_FF = 4 * D_MODEL
A_OUT = A_HEADS * HEAD_DIM
B_OUT = B_HEADS * B_V
C_OUT = C_HEADS * C_DV
C_W = C_HEADS * C_DK
ROPE_THETA = 10000.0
EPS = 1e-6
F_TINY = 1e-30

LANE = 128
VMEM_LIMIT = 56 * 1024 * 1024

OFF_AQ = 0
OFF_AK = 512
OFF_AV = 768
OFF_BQD = 896
OFF_BKVD = 1152
OFF_KPE = 1280
OFF_CQ = 1408
OFF_CFF = 1664
OFF_CFB = 1920
OFF_CI = 2176
OFF_CG = 2432
N_COL = 2688

C_Q, C_KF, C_GF, C_KB, C_GB, C_V, C_GATE = range(7)

SCAN_C = 64
N_LEVELS = 6


def _dot(a, b):
    return jnp.dot(a, b, preferred_element_type=F32)


def _dot_nt(a, b):
    return lax.dot_general(a, b, (((1,), (1,)), ((), ())), preferred_element_type=F32)


def _dot_tn(a, b):
    return lax.dot_general(a, b, (((0,), (0,)), ((), ())), preferred_element_type=F32)


def _split3(x):
    hi = x.astype(BF16)
    r = x - hi.astype(F32)
    mid = r.astype(BF16)
    lo = (r - mid.astype(F32)).astype(BF16)
    return hi, mid, lo


def _dot_exact_lhs(a, x):
    hi, mid, lo = _split3(x)
    return _dot(a, hi) + _dot(a, mid) + _dot(a, lo)


def _sigmoid_pair(z):
    e = jnp.exp(-jnp.abs(z))
    inv = 1.0 / (1.0 + e)
    small = e * inv
    pos = z >= 0
    return jnp.where(pos, inv, small), jnp.where(pos, small, inv)


def _silu(z):
    s, _ = _sigmoid_pair(z)
    return z * s


def _const_spec(shape, index):
    return pl.BlockSpec(shape, lambda *_: index, pipeline_mode=pl.Buffered(1))


def _params(sem):
    return pltpu.CompilerParams(dimension_semantics=sem, vmem_limit_bytes=VMEM_LIMIT)


def _ada_kernel(c_ref, w_ref, b_ref, o_ref):
    a = _silu(c_ref[...])
    w = w_ref[...]
    a_hi = a.astype(BF16)
    a_lo = (a - a_hi.astype(F32)).astype(BF16)
    w_hi = w.astype(BF16)
    w_lo = (w - w_hi.astype(F32)).astype(BF16)
    acc = _dot(a_hi, w_hi) + (_dot(a_hi, w_lo) + _dot(a_lo, w_hi))
    o_ref[...] = acc + b_ref[...]


def _ada(cvec, w_ada, b_ada):
    depth = w_ada.shape[0]
    rows = cvec.shape[0]
    n_blk = w_ada.shape[2] // D_MODEL
    return pl.pallas_call(
        _ada_kernel,
        out_shape=jax.ShapeDtypeStruct((depth, rows, n_blk * D_MODEL), F32),
        grid=(depth, n_blk),
        in_specs=[
            pl.BlockSpec((rows, D_MODEL), lambda l, j: (0, 0)),
            pl.BlockSpec((None, D_MODEL, D_MODEL), lambda l, j: (l, 0, j)),
            pl.BlockSpec((None, None, 1, D_MODEL), lambda l, j: (l, j, 0, 0)),
        ],
        out_specs=pl.BlockSpec((None, rows, D_MODEL), lambda l, j: (l, 0, j)),
        compiler_params=_params(("arbitrary", "arbitrary")),
        name="ada",
    )(cvec, w_ada, b_ada.reshape(depth, n_blk, 1, D_MODEL))


def _rope128(v, cos, s1, s2, half):
    up = pltpu.roll(v, LANE - half, 1)
    dn = pltpu.roll(v, half, 1)
    return v * cos + up * s1 + dn * s2


def _head_rms(v, bo, gain):
    sq = v * v
    hi = sq.astype(BF16)
    lo = (sq - hi.astype(F32)).astype(BF16)
    ms = _dot(hi, bo) + _dot(lo, bo)
    return v * lax.rsqrt(ms + EPS) * gain


def _feat_kernel(*refs, use_rope):
    (x_ref, sh_ref, sc_ref, gpre_ref, win_ref, wq_ref, wkk_ref, wkv_ref,
     aqn_ref, akn_ref, bqn_ref, bkvn_ref, bo_ref, lb_ref) = refs[:14]
    rest = refs[14:]
    if use_rope:
        ca_ref, s1a_ref, s2a_ref, cb_ref, s1b_ref, s2b_ref = rest[:6]
        rest = rest[6:]
    aq_o, ak_o, avt_o, bq_o, bk_o, bvt_o, c_o = rest

    x = x_ref[...]
    ms = jnp.mean(x * x, axis=-1, keepdims=True)
    h = x * lax.rsqrt(ms + EPS) * gpre_ref[...]
    h = h * (1.0 + sc_ref[...]) + sh_ref[...]
    p = _dot(h.astype(BF16), win_ref[...])
    bo = bo_ref[...]

    def rope_a(v):
        if not use_rope:
            return v
        return _rope128(v, ca_ref[...], s1a_ref[...], s2a_ref[...], HEAD_DIM // 2)

    def rope_b(v):
        if not use_rope:
            return v
        return _rope128(v, cb_ref[...], s1b_ref[...], s2b_ref[...], B_ROPE // 2)

    aqn = aqn_ref[...]
    for half in range(2):
        v = _head_rms(p[:, OFF_AQ + 256 * half:OFF_AQ + 256 * (half + 1)], bo, aqn)
        for s in range(2):
            blk = rope_a(v[:, LANE * s:LANE * (s + 1)]) * (HEAD_DIM ** -0.5)
            lo = 256 * half + LANE * s
            aq_o[:, lo:lo + LANE] = blk.astype(BF16)
    v = _head_rms(p[:, OFF_AK:OFF_AK + 256], bo, akn_ref[...])
    for s in range(2):
        ak_o[:, LANE * s:LANE * (s + 1)] = rope_a(v[:, LANE * s:LANE * (s + 1)]).astype(BF16)
    avt_o[...] = p[:, OFF_AV:OFF_AV + LANE].T.astype(BF16)

    bqd = p[:, OFF_BQD:OFF_BQD + 256]
    ms = jnp.sum(bqd * bqd, axis=-1, keepdims=True) * (1.0 / B_Q_RANK)
    qn = (bqd * lax.rsqrt(ms + EPS) * bqn_ref[...]).astype(BF16)
    bq = _dot(qn, wq_ref[...])
    bkvd = p[:, OFF_BKVD:OFF_BKVD + LANE]
    ms = jnp.mean(bkvd * bkvd, axis=-1, keepdims=True)
    kvn = (bkvd * lax.rsqrt(ms + EPS) * bkvn_ref[...]).astype(BF16)
    bkn = _dot(kvn, wkk_ref[...])
    bv = _dot(kvn, wkv_ref[...])
    kpe = rope_b(p[:, OFF_KPE:OFF_KPE + LANE])
    b_scale = (B_NOPE + B_ROPE) ** -0.5
    for hh in range(B_HEADS):
        sl = slice(LANE * hh, LANE * (hh + 1))
        bq_o[:, sl] = (rope_b(bq[:, sl]) * b_scale).astype(BF16)
        bk_o[:, sl] = (bkn[:, sl] + kpe).astype(BF16)
    bvt_o[...] = bv.T.astype(BF16)

    c_o[:, C_W * C_Q:C_W * (C_Q + 1)] = _silu(p[:, OFF_CQ:OFF_CQ + C_W])
    for d, (off, ck, cg) in enumerate(((OFF_CFF, C_KF, C_GF), (OFF_CFB, C_KB, C_GB))):
        lb = lb_ref[d:d + 1, :]
        sp, sn = _sigmoid_pair(p[:, off:off + C_W])
        f = lb + (1.0 - lb) * sp
        c_o[:, C_W * cg:C_W * (cg + 1)] = jnp.log(jnp.maximum(f, F_TINY))
        c_o[:, C_W * ck:C_W * (ck + 1)] = (1.0 - lb) * sn
    c_o[:, C_W * C_V:C_W * (C_V + 1)] = p[:, OFF_CI:OFF_CI + C_W]
    c_o[:, C_W * C_GATE:C_W * (C_GATE + 1)] = p[:, OFF_CG:OFF_CG + C_W]


def _features(xs, mod, w, layer, mod_row, rope):
    bsz, t, _ = xs.shape
    tm = min(512, t)
    nt = t // tm
    nb_rows = w["n_mod_rows"]

    def mod_spec(j):
        return pl.BlockSpec((None, 1, D_MODEL),
                            lambda b, i: ((layer * nb_rows + mod_row(b)) * 6 + j, 0, 0))

    in_specs = [
        pl.BlockSpec((None, tm, D_MODEL), lambda b, i: (b, i, 0)),
        mod_spec(0), mod_spec(1),
        _const_spec((None, 1, D_MODEL), (layer, 0, 0)),
        _const_spec((None, D_MODEL, N_COL), (layer, 0, 0)),
        _const_spec((None, 256, 512), (layer, 0, 0)),
        _const_spec((None, LANE, 512), (layer, 0, 0)),
        _const_spec((None, LANE, 256), (layer, 0, 0)),
        _const_spec((None, 1, 256), (layer, 0, 0)),
        _const_spec((None, 1, 256), (layer, 0, 0)),
        _const_spec((None, 1, 256), (layer, 0, 0)),
        _const_spec((None, 1, LANE), (layer, 0, 0)),
        _const_spec((256, 256), (0, 0)),
        _const_spec((None, 2, C_W), (layer, 0, 0)),
    ]
    args = [xs, mod, mod, w["g_pre_mix"], w["w_in"], w["w_q_up"], w["w_kv_k"], w["w_kv_v"],
            w["a_q_norm"], w["a_k_norm"], w["b_q_norm"], w["b_kv_norm"], w["block_ones"], w["lower"]]
    if rope is not None:
        in_specs += [pl.BlockSpec((tm, LANE), lambda b, i: (i, 0))] * 6
        args += list(rope)
    out_shape = [
        jax.ShapeDtypeStruct((bsz, t, 512), BF16),
        jax.ShapeDtypeStruct((bsz, t, 256), BF16),
        jax.ShapeDtypeStruct((bsz, nt, LANE, tm), BF16),
        jax.ShapeDtypeStruct((bsz, t, 512), BF16),
        jax.ShapeDtypeStruct((bsz, t, 512), BF16),
        jax.ShapeDtypeStruct((bsz, nt, 256, tm), BF16),
        jax.ShapeDtypeStruct((bsz, t, 7 * C_W), F32),
    ]
    out_specs = [
        pl.BlockSpec((None, tm, 512), lambda b, i: (b, i, 0)),
        pl.BlockSpec((None, tm, 256), lambda b, i: (b, i, 0)),
        pl.BlockSpec((None, None, LANE, tm), lambda b, i: (b, i, 0, 0)),
        pl.BlockSpec((None, tm, 512), lambda b, i: (b, i, 0)),
        pl.BlockSpec((None, tm, 512), lambda b, i: (b, i, 0)),
        pl.BlockSpec((None, None, 256, tm), lambda b, i: (b, i, 0, 0)),
        pl.BlockSpec((None, tm, 7 * C_W), lambda b, i: (b, i, 0)),
    ]
    outs = pl.pallas_call(
        functools.partial(_feat_kernel, use_rope=rope is not None),
        out_shape=out_shape, grid=(bsz, nt), in_specs=in_specs, out_specs=out_specs,
        compiler_params=_params(("parallel", "parallel")),
        name="feat_rope" if rope is not None else "feat_ctx",
    )(*args)
    return dict(zip(("aq", "ak", "avt", "bq", "bk", "bvt", "c"), outs))


ACC_ROWS = B_V + 16
NEG_BIG = -1e30


def _attn_kernel(*refs, n_src, n_qblk, heads_per_blk):
    q_ref = refs[0]
    src_refs = refs[1:1 + 2 * n_src]
    o_ref = refs[1 + 2 * n_src]
    tq = q_ref.shape[0]

    head = 0
    for blk in range(n_qblk):
        qb = q_ref[:, LANE * blk:LANE * (blk + 1)]
        for hh in range(heads_per_blk):
            if heads_per_blk == 2:
                lane = lax.broadcasted_iota(jnp.int32, qb.shape, 1)
                keep = (lane < HEAD_DIM) if hh == 0 else (lane >= HEAD_DIM)
                qm = jnp.where(keep, qb, jnp.zeros_like(qb))
            else:
                qm = qb
            m = jnp.full((1, tq), NEG_BIG, F32)
            acc = jnp.zeros((ACC_ROWS, tq), F32)
            for s in range(n_src):
                k_ref, vt_ref = src_refs[2 * s], src_refs[2 * s + 1]
                n_chunk, _, tk = vt_ref.shape

                def body(c, carry, k_ref=k_ref, vt_ref=vt_ref, tk=tk, qm=qm):
                    m, acc = carry
                    kc = k_ref[pl.ds(pl.multiple_of(c * tk, tk), tk), :]
                    st = _dot_nt(kc, qm)
                    m_new = jnp.maximum(m, jnp.max(st, axis=0, keepdims=True))
                    pt = jnp.exp(st - m_new).astype(BF16)
                    alpha = jnp.exp(m - m_new)
                    vt = jnp.concatenate([vt_ref[c], jnp.ones((16, tk), BF16)], axis=0)
                    return m_new, alpha * acc + _dot(vt, pt)

                if n_chunk == 1:
                    m, acc = body(0, (m, acc))
                else:
                    m, acc = lax.fori_loop(0, n_chunk, body, (m, acc))
            out = acc[:B_V] * (1.0 / acc[B_V:B_V + 1])
            o_ref[B_V * head:B_V * (head + 1), :] = out.astype(BF16)
            head += 1


def _attention(q, srcs, *, n_kv, n_qblk, heads_per_blk, name):
    bsz, t, _ = q.shape
    tq = min(256, t)
    n_heads_step = n_qblk * heads_per_blk
    in_specs = [pl.BlockSpec((None, tq, LANE * n_qblk), lambda b, g, i: (b, i, g))]
    args = [q]
    for k, vt in srcs:
        length = k.shape[1]
        n_chunk, tk = vt.shape[1], vt.shape[3]
        in_specs.append(pl.BlockSpec((None, length, LANE), lambda b, g, i: (b, 0, g)))
        in_specs.append(pl.BlockSpec((None, n_chunk, B_V, tk), lambda b, g, i: (b, 0, g, 0)))
        args += [k, vt]
    rows = B_V * n_heads_step
    return pl.pallas_call(
        functools.partial(_attn_kernel, n_src=len(srcs), n_qblk=n_qblk, heads_per_blk=heads_per_blk),
        out_shape=jax.ShapeDtypeStruct((bsz, rows * n_kv, t), BF16),
        grid=(bsz, n_kv, t // tq),
        in_specs=in_specs,
        out_specs=pl.BlockSpec((None, rows, tq), lambda b, g, i: (b, g, i)),
        compiler_params=_params(("parallel", "parallel", "parallel")),
        name=name,
    )(*args)


def _scan_constants(reverse):
    c = SCAN_C
    t = np.arange(c)[:, None]
    s = np.arange(c)[None, :]
    cum = (s >= t) if reverse else (s <= t)
    gat = np.zeros((N_LEVELS, c, c), np.float32)
    msk = np.zeros((N_LEVELS + 1, c, c), np.float32)
    for lvl in range(N_LEVELS):
        w = (c // 2) >> lvl
        same = (t // (2 * w)) == (s // (2 * w))
        t_hi = (t % (2 * w)) >= w
        s_hi = (s % (2 * w)) >= w
        if reverse:
            bnd = (np.arange(c) // (2 * w)) * 2 * w + w
            msk[lvl] = same & ~t_hi & s_hi
        else:
            bnd = (np.arange(c) // (2 * w)) * 2 * w + w - 1
            msk[lvl] = same & t_hi & ~s_hi
        gat[lvl, np.arange(c), bnd] = 1.0
    msk[N_LEVELS] = (t == s)
    return (jnp.asarray(cum, BF16), jnp.asarray(gat.reshape(N_LEVELS * c, c), BF16),
            jnp.asarray(np.tile(msk, (1, 1, C_HEADS)), F32))


def _hgrn_kernel(q_ref, k_ref, g_ref, v_ref, s0_ref, cum_ref, gat_ref, msk_ref, bm_ref,
                 o_ref, sfin_ref, st_ref, *, reverse, n_chunk):
    i = pl.program_id(1)

    @pl.when(i == 0)
    def _():
        st_ref[...] = s0_ref[...]

    bm = bm_ref[...]
    bm16 = bm.astype(BF16)
    cum = cum_ref[...]
    gat = gat_ref[...]

    def stack_heads(a):
        a16 = a.astype(BF16)
        return jnp.concatenate([a16] * C_HEADS, axis=0) * bm16

    def chunk(j, carry):
        jj = (n_chunk - 1 - j) if reverse else j
        rows = pl.ds(pl.multiple_of(jj * SCAN_C, SCAN_C), SCAN_C)
        q = q_ref[rows, :]
        k = k_ref[rows, :]
        g = g_ref[rows, :]
        v = v_ref[rows, :]
        b = _dot_exact_lhs(cum, g)
        bnd = _dot_exact_lhs(gat, b)
        b_all = b[0:1, :] if reverse else b[SCAN_C - 1:SCAN_C, :]

        sc = msk_ref[N_LEVELS] * _dot_nt(q.astype(BF16), stack_heads(k))
        for lvl in range(N_LEVELS):
            d = b - bnd[SCAN_C * lvl:SCAN_C * (lvl + 1), :]
            ql = q * jnp.exp(jnp.minimum(d, 0.0))
            kl = k * jnp.exp(jnp.minimum(-d, 0.0))
            sc = sc + msk_ref[lvl] * _dot_nt(ql.astype(BF16), stack_heads(kl))

        st = st_ref[...]
        qdec = q * jnp.exp(b)
        o = _dot(sc.astype(BF16), stack_heads(v)) + _dot_nt(qdec.astype(BF16), st.astype(BF16))
        o_ref[rows, :] = o
        kdec = k * jnp.exp(b_all - b)
        st_ref[...] = st * jnp.exp(b_all) + _dot_tn(v.astype(BF16), kdec.astype(BF16)) * bm
        return carry

    lax.fori_loop(0, n_chunk, chunk, 0)

    @pl.when(i == pl.num_programs(1) - 1)
    def _():
        sfin_ref[...] = st_ref[...]


def _hgrn(c_slab, s0, consts, block_mask, reverse):
    bsz, t, _ = c_slab.shape
    tb = min(512, t)
    nblk = t // tb
    cum, gat, msk = consts

    def blk(i):
        return (nblk - 1 - i) if reverse else i

    def slab_spec(j):
        return pl.BlockSpec((None, tb, C_W), lambda b, i: (b, blk(i), j))

    return pl.pallas_call(
        functools.partial(_hgrn_kernel, reverse=reverse, n_chunk=tb // SCAN_C),
        out_shape=[jax.ShapeDtypeStruct((bsz, t, C_W), F32),
                   jax.ShapeDtypeStruct((bsz, C_W, C_W), F32)],
        grid=(bsz, nblk),
        in_specs=[
            slab_spec(C_Q), slab_spec(C_KB if reverse else C_KF), slab_spec(C_GB if reverse else C_GF),
            slab_spec(C_V),
            pl.BlockSpec((None, C_W, C_W), lambda b, i: (b, 0, 0)),
            _const_spec(cum.shape, (0, 0)),
            _const_spec(gat.shape, (0, 0)),
            _const_spec(msk.shape, (0, 0, 0)),
            _const_spec(block_mask.shape, (0, 0)),
        ],
        out_specs=[pl.BlockSpec((None, tb, C_W), lambda b, i: (b, blk(i), 0)),
                   pl.BlockSpec((None, C_W, C_W), lambda b, i: (b, 0, 0))],
        scratch_shapes=[pltpu.VMEM((C_W, C_W), F32)],
        compiler_params=_params(("parallel", "arbitrary")),
        name="hgrn_bwd" if reverse else "hgrn_fwd",
    )(c_slab, c_slab, c_slab, c_slab, s0, cum, gat, msk, block_mask)


def _rms(v, gain):
    ms = jnp.mean(v * v, axis=-1, keepdims=True)
    return v * lax.rsqrt(ms + EPS) * gain


def _post_kernel(x_ref, ya_ref, yb_ref, of_ref, ob_ref, gate_ref,
                 gtm_ref, shf_ref, scf_ref, gtf_ref,
                 gpm_ref, gpf_ref, gqf_ref, con_ref, bo_ref,
                 woa_ref, wob_ref, woc_ref, w1_ref, w2_ref, o_ref):
    o = of_ref[...] + ob_ref[...]
    yc = (_head_rms(o, bo_ref[...], con_ref[...]) * _silu(gate_ref[...])).astype(BF16)
    mix = (_dot_tn(ya_ref[...], woa_ref[...]) + _dot_tn(yb_ref[...], wob_ref[...])
           + _dot(yc, woc_ref[...]))
    x1 = x_ref[...] + gtm_ref[...] * _rms(mix, gpm_ref[...])
    h = _rms(x1, gpf_ref[...]) * (1.0 + scf_ref[...]) + shf_ref[...]
    u = jnp.maximum(_dot(h.astype(BF16), w1_ref[...]), 0.0)
    ff = _dot((u * u).astype(BF16), w2_ref[...])
    o_ref[...] = x1 + gtf_ref[...] * _rms(ff, gqf_ref[...])


def _post(xs, yta, ytb, o_f, o_b, c_slab, mod, w, layer, mod_row):
    bsz, t, _ = xs.shape
    tm = min(512, t)
    nb_rows = w["n_mod_rows"]

    def mod_spec(j):
        return pl.BlockSpec((None, 1, D_MODEL),
                            lambda b, i: ((layer * nb_rows + mod_row(b)) * 6 + j, 0, 0))

    def vec_spec():
        return _const_spec((None, 1, D_MODEL), (layer, 0, 0))

    in_specs = [
        pl.BlockSpec((None, tm, D_MODEL), lambda b, i: (b, i, 0)),
        pl.BlockSpec((None, A_OUT, tm), lambda b, i: (b, 0, i)),
        pl.BlockSpec((None, B_OUT, tm), lambda b, i: (b, 0, i)),
        pl.BlockSpec((None, tm, C_W), lambda b, i: (b, i, 0)),
        pl.BlockSpec((None, tm, C_W), lambda b, i: (b, i, 0)),
        pl.BlockSpec((None, tm, C_W), lambda b, i: (b, i, C_GATE)),
        mod_spec(2), mod_spec(3), mod_spec(4), mod_spec(5),
        vec_spec(), vec_spec(), vec_spec(),
        _const_spec((None, 1, C_W), (layer, 0, 0)),
        _const_spec((256, 256), (0, 0)),
        _const_spec((None, A_OUT, D_MODEL), (layer, 0, 0)),
        _const_spec((None, B_OUT, D_MODEL), (layer, 0, 0)),
        _const_spec((None, C_OUT, D_MODEL), (layer, 0, 0)),
        _const_spec((None, D_MODEL, D_FF), (layer, 0, 0)),
        _const_spec((None, D_FF, D_MODEL), (layer, 0, 0)),
    ]
    return pl.pallas_call(
        _post_kernel,
        out_shape=jax.ShapeDtypeStruct((bsz, t, D_MODEL), F32),
        grid=(bsz, t // tm),
        in_specs=in_specs,
        out_specs=pl.BlockSpec((None, tm, D_MODEL), lambda b, i: (b, i, 0)),
        compiler_params=_params(("parallel", "parallel")),
        name="post",
    )(xs, yta, ytb, o_f, o_b, c_slab, mod, mod, mod, mod,
      w["g_post_mix"], w["g_pre_ffn"], w["g_post_ffn"], w["c_out_norm"], w["block_ones"],
      w["w_out_a"], w["w_out_b"], w["w_out_c"], w["w_ff1"], w["w_ff2"])


def _w_in_columns():
    src = np.full((N_COL,), -1, np.int64)
    a_k0 = A_HEADS * HEAD_DIM
    a_v0 = a_k0 + A_KV_HEADS * HEAD_DIM
    b_qd0 = a_v0 + A_KV_HEADS * HEAD_DIM
    b_kv0 = b_qd0 + B_Q_RANK
    b_kr0 = b_kv0 + B_KV_RANK
    c0 = b_kr0 + B_ROPE
    src[OFF_AQ:OFF_AQ + 512] = np.arange(512)
    for g in range(A_KV_HEADS):
        for rep in range(2):
            lo = OFF_AK + (2 * g + rep) * HEAD_DIM
            src[lo:lo + HEAD_DIM] = a_k0 + g * HEAD_DIM + np.arange(HEAD_DIM)
    src[OFF_AV:OFF_AV + 128] = a_v0 + np.arange(128)
    src[OFF_BQD:OFF_BQD + B_Q_RANK] = b_qd0 + np.arange(B_Q_RANK)
    src[OFF_BKVD:OFF_BKVD + B_KV_RANK] = b_kv0 + np.arange(B_KV_RANK)
    src[OFF_KPE + B_NOPE:OFF_KPE + B_NOPE + B_ROPE] = b_kr0 + np.arange(B_ROPE)
    src[OFF_CQ:OFF_CQ + 5 * C_W] = c0 + np.arange(5 * C_W)
    return src


def _gather_cols(w, src):
    keep = jnp.asarray(src >= 0)
    return jnp.where(keep, jnp.take(w, jnp.asarray(np.maximum(src, 0)), axis=-1), 0.0)


def _prepare_weights(p, n_mod_rows):
    depth = p["w_in"].shape[0]
    w = {"n_mod_rows": n_mod_rows}
    w["w_in"] = _gather_cols(p["w_in"], _w_in_columns()).astype(BF16)
    src = np.full((B_HEADS * LANE,), -1, np.int64)
    for hh in range(B_HEADS):
        src[hh * LANE:hh * LANE + B_NOPE + B_ROPE] = hh * (B_NOPE + B_ROPE) + np.arange(B_NOPE + B_ROPE)
    wq = _gather_cols(p["w_q_up"], src)
    w["w_q_up"] = jnp.pad(wq, ((0, 0), (0, 256 - B_Q_RANK), (0, 0))).astype(BF16)
    src = np.full((B_HEADS * LANE,), -1, np.int64)
    srcv = np.zeros((B_HEADS * B_V,), np.int64)
    for hh in range(B_HEADS):
        src[hh * LANE:hh * LANE + B_NOPE] = hh * (B_NOPE + B_V) + np.arange(B_NOPE)
        srcv[hh * B_V:(hh + 1) * B_V] = hh * (B_NOPE + B_V) + B_NOPE + np.arange(B_V)
    w["w_kv_k"] = _gather_cols(p["w_kv_up"], src).astype(BF16)
    w["w_kv_v"] = _gather_cols(p["w_kv_up"], srcv).astype(BF16)
    w["a_q_norm"] = jnp.tile(p["a_q_norm"], (1, 4))[:, None, :]
    w["a_k_norm"] = jnp.tile(p["a_k_norm"], (1, 4))[:, None, :]
    w["b_q_norm"] = jnp.pad(p["b_q_norm"], ((0, 0), (0, 256 - B_Q_RANK)))[:, None, :]
    w["b_kv_norm"] = p["b_kv_norm"][:, None, :]
    w["c_out_norm"] = jnp.tile(p["c_out_norm"], (1, C_HEADS))[:, None, :]
    for name in ("g_pre_mix", "g_post_mix", "g_pre_ffn", "g_post_ffn"):
        w[name] = p[name][:, None, :]
    head = np.arange(256) // HEAD_DIM
    same_head = head[:, None] == head[None, :]
    w["block_ones"] = jnp.asarray(same_head / float(HEAD_DIM), BF16)
    w["block_mask"] = jnp.asarray(same_head, F32)
    p_lb = jax.nn.softmax(p["c_lower_bounds"].astype(F32), axis=0)
    w["lower"] = jnp.cumsum(p_lb, axis=0) - p_lb[:1]
    w["w_out_a"] = p["w_out"][:, :A_OUT].astype(BF16)
    w["w_out_b"] = p["w_out"][:, A_OUT:A_OUT + B_OUT].astype(BF16)
    w["w_out_c"] = p["w_out"][:, A_OUT + B_OUT:].astype(BF16)
    w["w_ff1"] = p["w_ff1"].astype(BF16)
    w["w_ff2"] = p["w_ff2"].astype(BF16)
    del depth
    return w


def _rope_tables(n_tok):
    tok = np.arange(n_tok)
    row = (tok // GRID_W).astype(np.float32)[:, None]
    col = (tok % GRID_W).astype(np.float32)[:, None]

    def angles(rot_dim):
        n_freq = rot_dim // 4
        inv = jnp.asarray(ROPE_THETA, F32) ** (-jnp.arange(n_freq, dtype=F32) / n_freq)
        ang = jnp.concatenate([jnp.asarray(row) * inv, jnp.asarray(col) * inv], axis=-1)
        return jnp.cos(ang), jnp.sin(ang)

    zeros = lambda n: jnp.zeros((n_tok, n), F32)
    ones = lambda n: jnp.ones((n_tok, n), F32)
    cos, sin = angles(HEAD_DIM)
    ca = jnp.concatenate([cos, cos, cos, cos], axis=-1)
    s1a = jnp.concatenate([-sin, zeros(32), -sin, zeros(32)], axis=-1)
    s2a = jnp.concatenate([zeros(32), sin, zeros(32), sin], axis=-1)
    cos, sin = angles(B_ROPE)
    cb = jnp.concatenate([ones(64), cos, cos, ones(32)], axis=-1)
    s1b = jnp.concatenate([zeros(64), -sin, zeros(16), zeros(32)], axis=-1)
    s2b = jnp.concatenate([zeros(64), zeros(16), sin, zeros(32)], axis=-1)
    return ca, s1a, s2a, cb, s1b, s2b


def kernel(x, c, ctx, c_ctx, w_ada, b_ada, g_pre_mix, g_post_mix, g_pre_ffn, g_post_ffn, w_in, a_q_norm, a_k_norm, b_q_norm, w_q_up, b_kv_norm, w_kv_up, c_lower_bounds, c_out_norm, w_out, w_ff1, w_ff2):
    bsz, n_lat, _ = x.shape
    depth = w_in.shape[0]
    n_mod_rows = -(-(bsz + 1) // 8) * 8
    params = dict(w_in=w_in, a_q_norm=a_q_norm, a_k_norm=a_k_norm, b_q_norm=b_q_norm, w_q_up=w_q_up,
                  b_kv_norm=b_kv_norm, w_kv_up=w_kv_up, c_lower_bounds=c_lower_bounds,
                  c_out_norm=c_out_norm, w_out=w_out, w_ff1=w_ff1, w_ff2=w_ff2,
                  g_pre_mix=g_pre_mix, g_post_mix=g_post_mix, g_pre_ffn=g_pre_ffn, g_post_ffn=g_post_ffn)
    w = _prepare_weights(params, n_mod_rows)
    rope = _rope_tables(n_lat)
    scan_f = _scan_constants(False)
    scan_b = _scan_constants(True)

    cvec = jnp.concatenate([c, c_ctx[None, :], jnp.zeros((n_mod_rows - bsz - 1, D_MODEL), F32)], axis=0)
    mod = _ada(cvec, w_ada, b_ada).reshape(depth * n_mod_rows * 6, 1, D_MODEL)

    lat_row = lambda b: b
    ctx_row = lambda b: bsz
    zero_state = jnp.zeros((bsz, C_W, C_W), F32)
    attn_a = functools.partial(_attention, n_kv=A_KV_HEADS, n_qblk=2, heads_per_blk=2)
    attn_b = functools.partial(_attention, n_kv=B_HEADS, n_qblk=1, heads_per_blk=1)

    xc = ctx
    for layer in range(depth):
        need_ctx = layer < depth - 1
        fl = _features(x, mod, w, layer, lat_row, rope)
        fc = _features(xc, mod, w, layer, ctx_row, None)
        yta = attn_a(fl["aq"], [(fl["ak"], fl["avt"]), (fc["ak"], fc["avt"])], name="attn_a")
        ytb = attn_b(fl["bq"], [(fl["bk"], fl["bvt"]), (fc["bk"], fc["bvt"])], name="attn_b")
        ocf, s_f = _hgrn(fc["c"], zero_state, scan_f, w["block_mask"], False)
        ocb, s_b = _hgrn(fc["c"], zero_state, scan_b, w["block_mask"], True)
        olf, _ = _hgrn(fl["c"], s_f, scan_f, w["block_mask"], False)
        olb, _ = _hgrn(fl["c"], s_b, scan_b, w["block_mask"], True)
        x_new = _post(x, yta, ytb, olf, olb, fl["c"], mod, w, layer, lat_row)
        if need_ctx:
            yta_c = attn_a(fc["aq"], [(fc["ak"], fc["avt"])], name="attn_a_ctx")
            ytb_c = attn_b(fc["bq"], [(fc["bk"], fc["bvt"])], name="attn_b_ctx")
            xc = _post(xc, yta_c, ytb_c, ocf, ocb, fc["c"], mod, w, layer, ctx_row)
        x = x_new
    return x
```

```python
import functools

import numpy as np
import jax
import jax.numpy as jnp
from jax import lax
from jax.experimental import pallas as pl
from jax.experimental.pallas import tpu as pltpu

F32 = jnp.float32
BF16 = jnp.bfloat16

D_MODEL = 1024
GRID_W = 64
HEAD_DIM = 64
A_HEADS = 8
A_KV_HEADS = 2
B_HEADS = 4
B_Q_RANK = 192
B_KV_RANK = 128
B_NOPE = 64
B_ROPE = 32
B_V = 64
C_HEADS = 4
C_DK = 64
C_DV = 64
D_FF = 4 * D_MODEL
A_OUT = A_HEADS * HEAD_DIM
B_OUT = B_HEADS * B_V
C_OUT = C_HEADS * C_DV
C_W = C_HEADS * C_DK
ROPE_THETA = 10000.0
EPS = 1e-6
F_TINY = 1e-30

LANE = 128
VMEM_LIMIT = 56 * 1024 * 1024

OFF_AQ = 0
OFF_AK = 512
OFF_AV = 768
OFF_BQD = 896
OFF_BKVD = 1152
OFF_KPE = 1280
OFF_CQ = 1408
OFF_CFF = 1664
OFF_CFB = 1920
OFF_CI = 2176
OFF_CG = 2432
N_COL = 2688

C_Q, C_KF, C_GF, C_KB, C_GB, C_V, C_GATE = range(7)

SCAN_C = 64
N_LEVELS = 6


def _dot(a, b):
    return jnp.dot(a, b, preferred_element_type=F32)


def _dot_nt(a, b):
    return lax.dot_general(a, b, (((1,), (1,)), ((), ())), preferred_element_type=F32)


def _dot_tn(a, b):
    return lax.dot_general(a, b, (((0,), (0,)), ((), ())), preferred_element_type=F32)


def _split3(x):
    hi = x.astype(BF16)
    r = x - hi.astype(F32)
    mid = r.astype(BF16)
    lo = (r - mid.astype(F32)).astype(BF16)
    return hi, mid, lo


def _dot_exact_lhs(a, x):
    hi, mid, lo = _split3(x)
    return _dot(a, hi) + _dot(a, mid) + _dot(a, lo)


def _sigmoid_pair(z):
    e = jnp.exp(-jnp.abs(z))
    inv = 1.0 / (1.0 + e)
    small = e * inv
    pos = z >= 0
    return jnp.where(pos, inv, small), jnp.where(pos, small, inv)


def _silu(z):
    s, _ = _sigmoid_pair(z)
    return z * s


def _const_spec(shape, index):
    return pl.BlockSpec(shape, lambda *_: index, pipeline_mode=pl.Buffered(1))


def _params(sem):
    return pltpu.CompilerParams(dimension_semantics=sem, vmem_limit_bytes=VMEM_LIMIT)


def _ada_kernel(c_ref, w_ref, b_ref, o_ref):
    a = _silu(c_ref[...])
    w = w_ref[...]
    a_hi = a.astype(BF16)
    a_lo = (a - a_hi.astype(F32)).astype(BF16)
    w_hi = w.astype(BF16)
    w_lo = (w - w_hi.astype(F32)).astype(BF16)
    acc = _dot(a_hi, w_hi) + (_dot(a_hi, w_lo) + _dot(a_lo, w_hi))
    o_ref[...] = acc + b_ref[...]


def _ada(cvec, w_ada, b_ada):
    depth = w_ada.shape[0]
    rows = cvec.shape[0]
    n_blk = w_ada.shape[2] // D_MODEL
    return pl.pallas_call(
        _ada_kernel,
        out_shape=jax.ShapeDtypeStruct((depth, rows, n_blk * D_MODEL), F32),
        grid=(depth, n_blk),
        in_specs=[
            pl.BlockSpec((rows, D_MODEL), lambda l, j: (0, 0)),
            pl.BlockSpec((None, D_MODEL, D_MODEL), lambda l, j: (l, 0, j)),
            pl.BlockSpec((None, None, 1, D_MODEL), lambda l, j: (l, j, 0, 0)),
        ],
        out_specs=pl.BlockSpec((None, rows, D_MODEL), lambda l, j: (l, 0, j)),
        compiler_params=_params(("arbitrary", "arbitrary")),
        name="ada",
    )(cvec, w_ada, b_ada.reshape(depth, n_blk, 1, D_MODEL))


def _rope128(v, cos, s1, s2, half):
    up = pltpu.roll(v, LANE - half, 1)
    dn = pltpu.roll(v, half, 1)
    return v * cos + up * s1 + dn * s2


def _head_rms(v, bo, gain):
    sq = v * v
    hi = sq.astype(BF16)
    lo = (sq - hi.astype(F32)).astype(BF16)
    ms = _dot(hi, bo) + _dot(lo, bo)
    return v * lax.rsqrt(ms + EPS) * gain


def _feat_kernel(*refs, use_rope):
    (x_ref, sh_ref, sc_ref, gpre_ref, win_ref, wq_ref, wkk_ref, wkv_ref,
     aqn_ref, akn_ref, bqn_ref, bkvn_ref, bo_ref, lb_ref) = refs[:14]
    rest = refs[14:]
    if use_rope:
        ca_ref, s1a_ref, s2a_ref, cb_ref, s1b_ref, s2b_ref = rest[:6]
        rest = rest[6:]
    aq_o, ak_o, avt_o, bq_o, bk_o, bvt_o, c_o = rest

    x = x_ref[...]
    ms = jnp.mean(x * x, axis=-1, keepdims=True)
    h = x * lax.rsqrt(ms + EPS) * gpre_ref[...]
    h = h * (1.0 + sc_ref[...]) + sh_ref[...]
    p = _dot(h.astype(BF16), win_ref[...])
    bo = bo_ref[...]

    def rope_a(v):
        if not use_rope:
            return v
        return _rope128(v, ca_ref[...], s1a_ref[...], s2a_ref[...], HEAD_DIM // 2)

    def rope_b(v):
        if not use_rope:
            return v
        return _rope128(v, cb_ref[...], s1b_ref[...], s2b_ref[...], B_ROPE // 2)

    aqn = aqn_ref[...]
    for half in range(2):
        v = _head_rms(p[:, OFF_AQ + 256 * half:OFF_AQ + 256 * (half + 1)], bo, aqn)
        for s in range(2):
            blk = rope_a(v[:, LANE * s:LANE * (s + 1)]) * (HEAD_DIM ** -0.5 * LOG2_E)
            lo = 256 * half + LANE * s
            aq_o[lo:lo + LANE, :] = blk.T.astype(BF16)
    v = _head_rms(p[:, OFF_AK:OFF_AK + 256], bo, akn_ref[...])
    for s in range(2):
        ak_o[:, LANE * s:LANE * (s + 1)] = rope_a(v[:, LANE * s:LANE * (s + 1)]).astype(BF16)
    avt_o[...] = p[:, OFF_AV:OFF_AV + LANE].T.astype(BF16)

    bqd = p[:, OFF_BQD:OFF_BQD + 256]
    ms = jnp.sum(bqd * bqd, axis=-1, keepdims=True) * (1.0 / B_Q_RANK)
    qn = (bqd * lax.rsqrt(ms + EPS) * bqn_ref[...]).astype(BF16)
    bq = _dot(qn, wq_ref[...])
    bkvd = p[:, OFF_BKVD:OFF_BKVD + LANE]
    ms = jnp.mean(bkvd * bkvd, axis=-1, keepdims=True)
    kvn = (bkvd * lax.rsqrt(ms + EPS) * bkvn_ref[...]).astype(BF16)
    bkn = _dot(kvn, wkk_ref[...])
    bv = _dot(kvn, wkv_ref[...])
    kpe = rope_b(p[:, OFF_KPE:OFF_KPE + LANE])
    b_scale = (B_NOPE + B_ROPE) ** -0.5 * LOG2_E
    for hh in range(B_HEADS):
        sl = slice(LANE * hh, LANE * (hh + 1))
        bq_o[sl, :] = (rope_b(bq[:, sl]) * b_scale).T.astype(BF16)
        bk_o[:, sl] = (bkn[:, sl] + kpe).astype(BF16)
    bvt_o[...] = bv.T.astype(BF16)

    c_o[:, C_W * C_Q:C_W * (C_Q + 1)] = _silu(p[:, OFF_CQ:OFF_CQ + C_W])
    for d, (off, ck, cg) in enumerate(((OFF_CFF, C_KF, C_GF), (OFF_CFB, C_KB, C_GB))):
        lb = lb_ref[d:d + 1, :]
        sp, sn = _sigmoid_pair(p[:, off:off + C_W])
        f = lb + (1.0 - lb) * sp
        c_o[:, C_W * cg:C_W * (cg + 1)] = jnp.log(jnp.maximum(f, F_TINY))
        c_o[:, C_W * ck:C_W * (ck + 1)] = (1.0 - lb) * sn
    c_o[:, C_W * C_V:C_W * (C_V + 1)] = p[:, OFF_CI:OFF_CI + C_W]
    c_o[:, C_W * C_GATE:C_W * (C_GATE + 1)] = p[:, OFF_CG:OFF_CG + C_W]


def _features(xs, mod, w, layer, mod_row, rope):
    bsz, t, _ = xs.shape
    tm = min(512, t)
    nt = t // tm
    nb_rows = w["n_mod_rows"]

    def mod_spec(j):
        return pl.BlockSpec((None, 1, D_MODEL),
                            lambda b, i: ((layer * nb_rows + mod_row(b)) * 6 + j, 0, 0))

    in_specs = [
        pl.BlockSpec((None, tm, D_MODEL), lambda b, i: (b, i, 0)),
        mod_spec(0), mod_spec(1),
        _const_spec((None, 1, D_MODEL), (layer, 0, 0)),
        _const_spec((None, D_MODEL, N_COL), (layer, 0, 0)),
        _const_spec((None, 256, 512), (layer, 0, 0)),
        _const_spec((None, LANE, 512), (layer, 0, 0)),
        _const_spec((None, LANE, 256), (layer, 0, 0)),
        _const_spec((None, 1, 256), (layer, 0, 0)),
        _const_spec((None, 1, 256), (layer, 0, 0)),
        _const_spec((None, 1, 256), (layer, 0, 0)),
        _const_spec((None, 1, LANE), (layer, 0, 0)),
        _const_spec((256, 256), (0, 0)),
        _const_spec((None, 2, C_W), (layer, 0, 0)),
    ]
    args = [xs, mod, mod, w["g_pre_mix"], w["w_in"], w["w_q_up"], w["w_kv_k"], w["w_kv_v"],
            w["a_q_norm"], w["a_k_norm"], w["b_q_norm"], w["b_kv_norm"], w["block_ones"], w["lower"]]
    if rope is not None:
        in_specs += [pl.BlockSpec((tm, LANE), lambda b, i: (i, 0))] * 6
        args += list(rope)
    out_shape = [
        jax.ShapeDtypeStruct((bsz, 512, t), BF16),
        jax.ShapeDtypeStruct((bsz, t, 256), BF16),
        jax.ShapeDtypeStruct((bsz, nt, LANE, tm), BF16),
        jax.ShapeDtypeStruct((bsz, 512, t), BF16),
        jax.ShapeDtypeStruct((bsz, t, 512), BF16),
        jax.ShapeDtypeStruct((bsz, nt, 256, tm), BF16),
        jax.ShapeDtypeStruct((bsz, t, 7 * C_W), F32),
    ]
    out_specs = [
        pl.BlockSpec((None, 512, tm), lambda b, i: (b, 0, i)),
        pl.BlockSpec((None, tm, 256), lambda b, i: (b, i, 0)),
        pl.BlockSpec((None, None, LANE, tm), lambda b, i: (b, i, 0, 0)),
        pl.BlockSpec((None, 512, tm), lambda b, i: (b, 0, i)),
        pl.BlockSpec((None, tm, 512), lambda b, i: (b, i, 0)),
        pl.BlockSpec((None, None, 256, tm), lambda b, i: (b, i, 0, 0)),
        pl.BlockSpec((None, tm, 7 * C_W), lambda b, i: (b, i, 0)),
    ]
    outs = pl.pallas_call(
        functools.partial(_feat_kernel, use_rope=rope is not None),
        out_shape=out_shape, grid=(bsz, nt), in_specs=in_specs, out_specs=out_specs,
        compiler_params=_params(("parallel", "parallel")),
        name="feat_rope" if rope is not None else "feat_ctx",
    )(*args)
    return dict(zip(("aq", "ak", "avt", "bq", "bk", "bvt", "c"), outs))


ACC_ROWS = B_V + 16
NEG_BIG = -1e30
LOG2_E = 1.4426950408889634


def _attn_kernel(*refs, n_src, n_heads, shared_kv):
    q_ref = refs[0]
    src_refs = refs[1:1 + 2 * n_src]
    o_ref = refs[1 + 2 * n_src]
    rhs_scr, m_scr, acc_scr, st_scr, mx_scr = refs[2 + 2 * n_src:]
    tq = q_ref.shape[1]
    n_grp, k_dim, grp_w = rhs_scr.shape
    heads_per_grp = n_heads // n_grp

    if shared_kv:
        for h in range(n_heads):
            rhs_scr[0, :, tq * h:tq * (h + 1)] = q_ref[HEAD_DIM * h:HEAD_DIM * (h + 1), :]
    else:
        rhs_scr[...] = jnp.zeros(rhs_scr.shape, BF16)
        for h in range(n_heads):
            g, j = divmod(h, heads_per_grp)
            rhs_scr[g, LANE * j:LANE * (j + 1), tq * j:tq * (j + 1)] = q_ref[LANE * h:LANE * (h + 1), :]
    m_scr[...] = jnp.full(m_scr.shape, NEG_BIG, F32)
    acc_scr[...] = jnp.zeros(acc_scr.shape, F32)

    def stage(s, c, slot):
        k_ref, vt_ref = src_refs[2 * s], src_refs[2 * s + 1]
        tk = vt_ref.shape[2]
        rows = pl.ds(pl.multiple_of(c * tk, tk), tk)
        for g in range(n_grp):
            cols = slice(grp_w * g, grp_w * (g + 1))
            kc = k_ref[rows, 0:k_dim] if shared_kv else k_ref[rows, k_dim * g:k_dim * (g + 1)]
            st = _dot(kc, rhs_scr[g])
            st_scr[slot, 0:tk, cols] = st
            mx_scr[slot, :, cols] = jnp.max(st, axis=0, keepdims=True)

    def consume(s, c, slot):
        vt_ref = src_refs[2 * s + 1]
        tk = vt_ref.shape[2]
        ones = jnp.ones((ACC_ROWS - B_V, tk), BF16)
        for g in range(n_grp):
            cols = slice(grp_w * g, grp_w * (g + 1))
            m_old = m_scr[:, cols]
            m_new = jnp.maximum(m_old, mx_scr[slot, :, cols])
            pt = jnp.exp2(st_scr[slot, 0:tk, cols] - m_new).astype(BF16)
            alpha = jnp.exp2(m_old - m_new)
            m_scr[:, cols] = m_new
            if shared_kv:
                vt = jnp.concatenate([vt_ref[c], ones], axis=0)
                acc_scr[:, cols] = alpha * acc_scr[:, cols] + _dot(vt, pt)
            else:
                for j in range(heads_per_grp):
                    h = g * heads_per_grp + j
                    hc = slice(tq * h, tq * (h + 1))
                    lc = slice(tq * j, tq * (j + 1))
                    vt = jnp.concatenate([vt_ref[c, B_V * h:B_V * (h + 1), :], ones], axis=0)
                    acc_scr[:, hc] = alpha[:, lc] * acc_scr[:, hc] + _dot(vt, pt[:, lc])

    n0 = src_refs[1].shape[0]
    n_loop = (n0 - 2) // 2 if n0 >= 4 else 0
    stage(0, 0, 0)
    if n_loop:
        def pair(i, carry):
            stage(0, 2 * i + 1, 1)
            consume(0, 2 * i, 0)
            stage(0, 2 * i + 2, 0)
            consume(0, 2 * i + 1, 1)
            return carry
        lax.fori_loop(0, n_loop, pair, 0)
    tail = [(0, c) for c in range(2 * n_loop, n0)]
    tail += [(s, c) for s in range(1, n_src) for c in range(src_refs[2 * s + 1].shape[0])]
    for i, (s, c) in enumerate(tail):
        if i + 1 < len(tail):
            stage(*tail[i + 1], (i + 1) % 2)
        consume(s, c, i % 2)

    for h in range(n_heads):
        acc = acc_scr[:, tq * h:tq * (h + 1)]
        o_ref[B_V * h:B_V * (h + 1), :] = (acc[:B_V] * (1.0 / acc[B_V:B_V + 1])).astype(BF16)


def _attention(q, srcs, *, n_kv, n_heads, shared_kv, name):
    bsz, _, t = q.shape
    tq = min(256, t)
    q_r = (HEAD_DIM if shared_kv else LANE) * n_heads
    k_w = LANE * (1 if shared_kv else n_heads)
    v_r = B_V * (1 if shared_kv else n_heads)
    rhs_shape = (1, HEAD_DIM, n_heads * tq) if shared_kv else (n_heads // 2, 2 * LANE, 2 * tq)
    in_specs = [pl.BlockSpec((None, q_r, tq), lambda b, g, i: (b, g, i))]
    args = [q]
    for k, vt in srcs:
        length = k.shape[1]
        n_chunk, tk = vt.shape[1], vt.shape[3]
        in_specs.append(pl.BlockSpec((None, length, k_w), lambda b, g, i: (b, 0, g)))
        in_specs.append(pl.BlockSpec((None, n_chunk, v_r, tk), lambda b, g, i: (b, 0, g, 0)))
        args += [k, vt]
    rows = B_V * n_heads
    return pl.pallas_call(
        functools.partial(_attn_kernel, n_src=len(srcs), n_heads=n_heads, shared_kv=shared_kv),
        out_shape=jax.ShapeDtypeStruct((bsz, rows * n_kv, t), BF16),
        grid=(bsz, n_kv, t // tq),
        in_specs=in_specs,
        out_specs=pl.BlockSpec((None, rows, tq), lambda b, g, i: (b, g, i)),
        scratch_shapes=[pltpu.VMEM(rhs_shape, BF16),
                        pltpu.VMEM((1, n_heads * tq), F32),
                        pltpu.VMEM((ACC_ROWS, n_heads * tq), F32),
                        pltpu.VMEM((2, max(vt.shape[3] for _, vt in srcs), n_heads * tq), F32),
                        pltpu.VMEM((2, 1, n_heads * tq), F32)],
        compiler_params=_params(("parallel", "parallel", "parallel")),
        name=name,
    )(*args)


def _scan_constants(reverse):
    c = SCAN_C
    t = np.arange(c)[:, None]
    s = np.arange(c)[None, :]
    cum = (s >= t) if reverse else (s <= t)
    gat = np.zeros((N_LEVELS, c, c), np.float32)
    msk = np.zeros((N_LEVELS + 1, c, c), np.float32)
    for lvl in range(N_LEVELS):
        w = (c // 2) >> lvl
        same = (t // (2 * w)) == (s // (2 * w))
        t_hi = (t % (2 * w)) >= w
        s_hi = (s % (2 * w)) >= w
        if reverse:
            bnd = (np.arange(c) // (2 * w)) * 2 * w + w
            msk[lvl] = same & ~t_hi & s_hi
        else:
            bnd = (np.arange(c) // (2 * w)) * 2 * w + w - 1
            msk[lvl] = same & t_hi & ~s_hi
        gat[lvl, np.arange(c), bnd] = 1.0
    msk[N_LEVELS] = (t == s)
    return (jnp.asarray(cum, BF16), jnp.asarray(gat.reshape(N_LEVELS * c, c), BF16),
            jnp.asarray(np.tile(msk, (1, 1, C_HEADS)), F32))


def _hgrn_kernel(q_ref, k_ref, g_ref, v_ref, s0_ref, cum_ref, gat_ref, msk_ref, bm_ref,
                 o_ref, sfin_ref, st_ref, *, reverse, n_chunk):
    i = pl.program_id(1)

    @pl.when(i == 0)
    def _():
        st_ref[...] = s0_ref[...]

    bm = bm_ref[...]
    bm16 = bm.astype(BF16)
    cum = cum_ref[...]
    gat = gat_ref[...]

    def stack_heads(a):
        a16 = a.astype(BF16)
        return jnp.concatenate([a16] * C_HEADS, axis=0) * bm16

    def chunk(j, carry):
        jj = (n_chunk - 1 - j) if reverse else j
        rows = pl.ds(pl.multiple_of(jj * SCAN_C, SCAN_C), SCAN_C)
        q = q_ref[rows, :]
        k = k_ref[rows, :]
        g = g_ref[rows, :]
        v = v_ref[rows, :]
        b = _dot_exact_lhs(cum, g)
        bnd = _dot_exact_lhs(gat, b)
        b_all = b[0:1, :] if reverse else b[SCAN_C - 1:SCAN_C, :]

        sc = msk_ref[N_LEVELS] * _dot_nt(q.astype(BF16), stack_heads(k))
        for lvl in range(N_LEVELS):
            d = b - bnd[SCAN_C * lvl:SCAN_C * (lvl + 1), :]
            ql = q * jnp.exp(jnp.minimum(d, 0.0))
            kl = k * jnp.exp(jnp.minimum(-d, 0.0))
            sc = sc + msk_ref[lvl] * _dot_nt(ql.astype(BF16), stack_heads(kl))

        st = st_ref[...]
        qdec = q * jnp.exp(b)
        o = _dot(sc.astype(BF16), stack_heads(v)) + _dot_nt(qdec.astype(BF16), st.astype(BF16))
        o_ref[rows, :] = o
        kdec = k * jnp.exp(b_all - b)
        st_ref[...] = st * jnp.exp(b_all) + _dot_tn(v.astype(BF16), kdec.astype(BF16)) * bm
        return carry

    lax.fori_loop(0, n_chunk, chunk, 0)

    @pl.when(i == pl.num_programs(1) - 1)
    def _():
        sfin_ref[...] = st_ref[...]


def _hgrn(c_slab, s0, consts, block_mask, reverse):
    bsz, t, _ = c_slab.shape
    tb = min(512, t)
    nblk = t // tb
    cum, gat, msk = consts

    def blk(i):
        return (nblk - 1 - i) if reverse else i

    def slab_spec(j):
        return pl.BlockSpec((None, tb, C_W), lambda b, i: (b, blk(i), j))

    return pl.pallas_call(
        functools.partial(_hgrn_kernel, reverse=reverse, n_chunk=tb // SCAN_C),
        out_shape=[jax.ShapeDtypeStruct((bsz, t, C_W), F32),
                   jax.ShapeDtypeStruct((bsz, C_W, C_W), F32)],
        grid=(bsz, nblk),
        in_specs=[
            slab_spec(C_Q), slab_spec(C_KB if reverse else C_KF), slab_spec(C_GB if reverse else C_GF),
            slab_spec(C_V),
            pl.BlockSpec((None, C_W, C_W), lambda b, i: (b, 0, 0)),
            _const_spec(cum.shape, (0, 0)),
            _const_spec(gat.shape, (0, 0)),
            _const_spec(msk.shape, (0, 0, 0)),
            _const_spec(block_mask.shape, (0, 0)),
        ],
        out_specs=[pl.BlockSpec((None, tb, C_W), lambda b, i: (b, blk(i), 0)),
                   pl.BlockSpec((None, C_W, C_W), lambda b, i: (b, 0, 0))],
        scratch_shapes=[pltpu.VMEM((C_W, C_W), F32)],
        compiler_params=_params(("parallel", "arbitrary")),
        name="hgrn_bwd" if reverse else "hgrn_fwd",
    )(c_slab, c_slab, c_slab, c_slab, s0, cum, gat, msk, block_mask)


def _rms(v, gain):
    ms = jnp.mean(v * v, axis=-1, keepdims=True)
    return v * lax.rsqrt(ms + EPS) * gain


def _post_kernel(x_ref, ya_ref, yb_ref, of_ref, ob_ref, gate_ref,
                 gtm_ref, shf_ref, scf_ref, gtf_ref,
                 gpm_ref, gpf_ref, gqf_ref, con_ref, bo_ref,
                 woa_ref, wob_ref, woc_ref, w1_ref, w2_ref, o_ref):
    o = of_ref[...] + ob_ref[...]
    yc = (_head_rms(o, bo_ref[...], con_ref[...]) * _silu(gate_ref[...])).astype(BF16)
    mix = (_dot_tn(ya_ref[...], woa_ref[...]) + _dot_tn(yb_ref[...], wob_ref[...])
           + _dot(yc, woc_ref[...]))
    x1 = x_ref[...] + gtm_ref[...] * _rms(mix, gpm_ref[...])
    h = _rms(x1, gpf_ref[...]) * (1.0 + scf_ref[...]) + shf_ref[...]
    u = jnp.maximum(_dot(h.astype(BF16), w1_ref[...]), 0.0)
    ff = _dot((u * u).astype(BF16), w2_ref[...])
    o_ref[...] = x1 + gtf_ref[...] * _rms(ff, gqf_ref[...])


def _post(xs, yta, ytb, o_f, o_b, c_slab, mod, w, layer, mod_row):
    bsz, t, _ = xs.shape
    tm = min(512, t)
    nb_rows = w["n_mod_rows"]

    def mod_spec(j):
        return pl.BlockSpec((None, 1, D_MODEL),
                            lambda b, i: ((layer * nb_rows + mod_row(b)) * 6 + j, 0, 0))

    def vec_spec():
        return _const_spec((None, 1, D_MODEL), (layer, 0, 0))

    in_specs = [
        pl.BlockSpec((None, tm, D_MODEL), lambda b, i: (b, i, 0)),
        pl.BlockSpec((None, A_OUT, tm), lambda b, i: (b, 0, i)),
        pl.BlockSpec((None, B_OUT, tm), lambda b, i: (b, 0, i)),
        pl.BlockSpec((None, tm, C_W), lambda b, i: (b, i, 0)),
        pl.BlockSpec((None, tm, C_W), lambda b, i: (b, i, 0)),
        pl.BlockSpec((None, tm, C_W), lambda b, i: (b, i, C_GATE)),
        mod_spec(2), mod_spec(3), mod_spec(4), mod_spec(5),
        vec_spec(), vec_spec(), vec_spec(),
        _const_spec((None, 1, C_W), (layer, 0, 0)),
        _const_spec((256, 256), (0, 0)),
        _const_spec((None, A_OUT, D_MODEL), (layer, 0, 0)),
        _const_spec((None, B_OUT, D_MODEL), (layer, 0, 0)),
        _const_spec((None, C_OUT, D_MODEL), (layer, 0, 0)),
        _const_spec((None, D_MODEL, D_FF), (layer, 0, 0)),
        _const_spec((None, D_FF, D_MODEL), (layer, 0, 0)),
    ]
    return pl.pallas_call(
        _post_kernel,
        out_shape=jax.ShapeDtypeStruct((bsz, t, D_MODEL), F32),
        grid=(bsz, t // tm),
        in_specs=in_specs,
        out_specs=pl.BlockSpec((None, tm, D_MODEL), lambda b, i: (b, i, 0)),
        compiler_params=_params(("parallel", "parallel")),
        name="post",
    )(xs, yta, ytb, o_f, o_b, c_slab, mod, mod, mod, mod,
      w["g_post_mix"], w["g_pre_ffn"], w["g_post_ffn"], w["c_out_norm"], w["block_ones"],
      w["w_out_a"], w["w_out_b"], w["w_out_c"], w["w_ff1"], w["w_ff2"])


def _w_in_columns():
    src = np.full((N_COL,), -1, np.int64)
    a_k0 = A_HEADS * HEAD_DIM
    a_v0 = a_k0 + A_KV_HEADS * HEAD_DIM
    b_qd0 = a_v0 + A_KV_HEADS * HEAD_DIM
    b_kv0 = b_qd0 + B_Q_RANK
    b_kr0 = b_kv0 + B_KV_RANK
    c0 = b_kr0 + B_ROPE
    src[OFF_AQ:OFF_AQ + 512] = np.arange(512)
    for g in range(A_KV_HEADS):
        for rep in range(2):
            lo = OFF_AK + (2 * g + rep) * HEAD_DIM
            src[lo:lo + HEAD_DIM] = a_k0 + g * HEAD_DIM + np.arange(HEAD_DIM)
    src[OFF_AV:OFF_AV + 128] = a_v0 + np.arange(128)
    src[OFF_BQD:OFF_BQD + B_Q_RANK] = b_qd0 + np.arange(B_Q_RANK)
    src[OFF_BKVD:OFF_BKVD + B_KV_RANK] = b_kv0 + np.arange(B_KV_RANK)
    src[OFF_KPE + B_NOPE:OFF_KPE + B_NOPE + B_ROPE] = b_kr0 + np.arange(B_ROPE)
    src[OFF_CQ:OFF_CQ + 5 * C_W] = c0 + np.arange(5 * C_W)
    return src


def _gather_cols(w, src):
    keep = jnp.asarray(src >= 0)
    return jnp.where(keep, jnp.take(w, jnp.asarray(np.maximum(src, 0)), axis=-1), 0.0)


def _prepare_weights(p, n_mod_rows):
    depth = p["w_in"].shape[0]
    w = {"n_mod_rows": n_mod_rows}
    w["w_in"] = _gather_cols(p["w_in"], _w_in_columns()).astype(BF16)
    src = np.full((B_HEADS * LANE,), -1, np.int64)
    for hh in range(B_HEADS):
        src[hh * LANE:hh * LANE + B_NOPE + B_ROPE] = hh * (B_NOPE + B_ROPE) + np.arange(B_NOPE + B_ROPE)
    wq = _gather_cols(p["w_q_up"], src)
    w["w_q_up"] = jnp.pad(wq, ((0, 0), (0, 256 - B_Q_RANK), (0, 0))).astype(BF16)
    src = np.full((B_HEADS * LANE,), -1, np.int64)
    srcv = np.zeros((B_HEADS * B_V,), np.int64)
    for hh in range(B_HEADS):
        src[hh * LANE:hh * LANE + B_NOPE] = hh * (B_NOPE + B_V) + np.arange(B_NOPE)
        srcv[hh * B_V:(hh + 1) * B_V] = hh * (B_NOPE + B_V) + B_NOPE + np.arange(B_V)
    w["w_kv_k"] = _gather_cols(p["w_kv_up"], src).astype(BF16)
    w["w_kv_v"] = _gather_cols(p["w_kv_up"], srcv).astype(BF16)
    w["a_q_norm"] = jnp.tile(p["a_q_norm"], (1, 4))[:, None, :]
    w["a_k_norm"] = jnp.tile(p["a_k_norm"], (1, 4))[:, None, :]
    w["b_q_norm"] = jnp.pad(p["b_q_norm"], ((0, 0), (0, 256 - B_Q_RANK)))[:, None, :]
    w["b_kv_norm"] = p["b_kv_norm"][:, None, :]
    w["c_out_norm"] = jnp.tile(p["c_out_norm"], (1, C_HEADS))[:, None, :]
    for name in ("g_pre_mix", "g_post_mix", "g_pre_ffn", "g_post_ffn"):
        w[name] = p[name][:, None, :]
    head = np.arange(256) // HEAD_DIM
    same_head = head[:, None] == head[None, :]
    w["block_ones"] = jnp.asarray(same_head / float(HEAD_DIM), BF16)
    w["block_mask"] = jnp.asarray(same_head, F32)
    p_lb = jax.nn.softmax(p["c_lower_bounds"].astype(F32), axis=0)
    w["lower"] = jnp.cumsum(p_lb, axis=0) - p_lb[:1]
    w["w_out_a"] = p["w_out"][:, :A_OUT].astype(BF16)
    w["w_out_b"] = p["w_out"][:, A_OUT:A_OUT + B_OUT].astype(BF16)
    w["w_out_c"] = p["w_out"][:, A_OUT + B_OUT:].astype(BF16)
    w["w_ff1"] = p["w_ff1"].astype(BF16)
    w["w_ff2"] = p["w_ff2"].astype(BF16)
    del depth
    return w


def _rope_tables(n_tok):
    tok = np.arange(n_tok)
    row = (tok // GRID_W).astype(np.float32)[:, None]
    col = (tok % GRID_W).astype(np.float32)[:, None]

    def angles(rot_dim):
        n_freq = rot_dim // 4
        inv = jnp.asarray(ROPE_THETA, F32) ** (-jnp.arange(n_freq, dtype=F32) / n_freq)
        ang = jnp.concatenate([jnp.asarray(row) * inv, jnp.asarray(col) * inv], axis=-1)
        return jnp.cos(ang), jnp.sin(ang)

    zeros = lambda n: jnp.zeros((n_tok, n), F32)
    ones = lambda n: jnp.ones((n_tok, n), F32)
    cos, sin = angles(HEAD_DIM)
    ca = jnp.concatenate([cos, cos, cos, cos], axis=-1)
    s1a = jnp.concatenate([-sin, zeros(32), -sin, zeros(32)], axis=-1)
    s2a = jnp.concatenate([zeros(32), sin, zeros(32), sin], axis=-1)
    cos, sin = angles(B_ROPE)
    cb = jnp.concatenate([ones(64), cos, cos, ones(32)], axis=-1)
    s1b = jnp.concatenate([zeros(64), -sin, zeros(16), zeros(32)], axis=-1)
    s2b = jnp.concatenate([zeros(64), zeros(16), sin, zeros(32)], axis=-1)
    return ca, s1a, s2a, cb, s1b, s2b


def kernel(x, c, ctx, c_ctx, w_ada, b_ada, g_pre_mix, g_post_mix, g_pre_ffn, g_post_ffn, w_in, a_q_norm, a_k_norm, b_q_norm, w_q_up, b_kv_norm, w_kv_up, c_lower_bounds, c_out_norm, w_out, w_ff1, w_ff2):
    bsz, n_lat, _ = x.shape
    depth = w_in.shape[0]
    n_mod_rows = -(-(bsz + 1) // 8) * 8
    params = dict(w_in=w_in, a_q_norm=a_q_norm, a_k_norm=a_k_norm, b_q_norm=b_q_norm, w_q_up=w_q_up,
                  b_kv_norm=b_kv_norm, w_kv_up=w_kv_up, c_lower_bounds=c_lower_bounds,
                  c_out_norm=c_out_norm, w_out=w_out, w_ff1=w_ff1, w_ff2=w_ff2,
                  g_pre_mix=g_pre_mix, g_post_mix=g_post_mix, g_pre_ffn=g_pre_ffn, g_post_ffn=g_post_ffn)
    w = _prepare_weights(params, n_mod_rows)
    rope = _rope_tables(n_lat)
    scan_f = _scan_constants(False)
    scan_b = _scan_constants(True)

    cvec = jnp.concatenate([c, c_ctx[None, :], jnp.zeros((n_mod_rows - bsz - 1, D_MODEL), F32)], axis=0)
    mod = _ada(cvec, w_ada, b_ada).reshape(depth * n_mod_rows * 6, 1, D_MODEL)

    lat_row = lambda b: b
    ctx_row = lambda b: bsz
    zero_state = jnp.zeros((bsz, C_W, C_W), F32)
    attn_a = functools.partial(_attention, n_kv=A_KV_HEADS, n_heads=A_HEADS // A_KV_HEADS, shared_kv=True)
    attn_b = functools.partial(_attention, n_kv=1, n_heads=B_HEADS, shared_kv=False)

    xc = ctx
    for layer in range(depth):
        need_ctx = layer < depth - 1
        fl = _features(x, mod, w, layer, lat_row, rope)
        fc = _features(xc, mod, w, layer, ctx_row, None)
        yta = attn_a(fl["aq"], [(fl["ak"], fl["avt"]), (fc["ak"], fc["avt"])], name="attn_a")
        ytb = attn_b(fl["bq"], [(fl["bk"], fl["bvt"]), (fc["bk"], fc["bvt"])], name="attn_b")
        ocf, s_f = _hgrn(fc["c"], zero_state, scan_f, w["block_mask"], False)
        ocb, s_b = _hgrn(fc["c"], zero_state, scan_b, w["block_mask"], True)
        olf, _ = _hgrn(fl["c"], s_f, scan_f, w["block_mask"], False)
        olb, _ = _hgrn(fl["c"], s_b, scan_b, w["block_mask"], True)
        x_new = _post(x, yta, ytb, olf, olb, fl["c"], mod, w, layer, lat_row)
        if need_ctx:
            yta_c = attn_a(fc["aq"], [(fc["ak"], fc["avt"])], name="attn_a_ctx")
            ytb_c = attn_b(fc["bq"], [(fc["bk"], fc["bvt"])], name="attn_b_ctx")
            xc = _post(xc, yta_c, ytb_c, ocf, ocb, fc["c"], mod, w, layer, ctx_row)
        x = x_new
    return x
```

```python
import functools

import numpy as np
import jax
import jax.numpy as jnp
from jax import lax
from jax.experimental import pallas as pl
from jax.experimental.pallas import tpu as pltpu

F32 = jnp.float32
BF16 = jnp.bfloat16

D_MODEL = 1024
GRID_W = 64
HEAD_DIM = 64
A_HEADS = 8
A_KV_HEADS = 2
B_HEADS = 4
B_Q_RANK = 192
B_KV_RANK = 128
B_NOPE = 64
B_ROPE = 32
B_V = 64
C_HEADS = 4
C_DK = 64
C_DV = 64
D_FF = 4 * D_MODEL
A_OUT = A_HEADS * HEAD_DIM
B_OUT = B_HEADS * B_V
C_OUT = C_HEADS * C_DV
C_W = C_HEADS * C_DK
ROPE_THETA = 10000.0
EPS = 1e-6
F_TINY = 1e-30

LANE = 128
VMEM_LIMIT = 56 * 1024 * 1024

OFF_AQ = 0
OFF_AK = 512
OFF_AV = 768
OFF_BQD = 896
OFF_BKVD = 1152
OFF_KPE = 1280
OFF_CQ = 1408
OFF_CFF = 1664
OFF_CFB = 1920
OFF_CI = 2176
OFF_CG = 2432
N_COL = 2688

C_Q, C_KF, C_KB, C_V, C_GATE = range(5)
N_SLAB = 5

SCAN_C = 64
N_LEVELS = 6
SCAN_GROUP = 8


def _dot(a, b):
    return jnp.dot(a, b, preferred_element_type=F32)


def _dot_nt(a, b):
    return lax.dot_general(a, b, (((1,), (1,)), ((), ())), preferred_element_type=F32)


def _dot_tn(a, b):
    return lax.dot_general(a, b, (((0,), (0,)), ((), ())), preferred_element_type=F32)


def _sigmoid_pair(z):
    e = jnp.exp(-jnp.abs(z))
    inv = 1.0 / (1.0 + e)
    small = e * inv
    pos = z >= 0
    return jnp.where(pos, inv, small), jnp.where(pos, small, inv)


def _silu(z):
    s, _ = _sigmoid_pair(z)
    return z * s


def _const_spec(shape, index):
    return pl.BlockSpec(shape, lambda *_: index, pipeline_mode=pl.Buffered(1))


def _params(sem):
    return pltpu.CompilerParams(dimension_semantics=sem, vmem_limit_bytes=VMEM_LIMIT)


def _ada_kernel(c_ref, w_ref, b_ref, o_ref):
    a = _silu(c_ref[...])
    w = w_ref[...]
    a_hi = a.astype(BF16)
    a_lo = (a - a_hi.astype(F32)).astype(BF16)
    w_hi = w.astype(BF16)
    w_lo = (w - w_hi.astype(F32)).astype(BF16)
    acc = _dot(a_hi, w_hi) + (_dot(a_hi, w_lo) + _dot(a_lo, w_hi))
    o_ref[...] = acc + b_ref[...]


def _ada(cvec, w_ada, b_ada):
    depth = w_ada.shape[0]
    rows = cvec.shape[0]
    n_blk = w_ada.shape[2] // D_MODEL
    return pl.pallas_call(
        _ada_kernel,
        out_shape=jax.ShapeDtypeStruct((depth, rows, n_blk * D_MODEL), F32),
        grid=(depth, n_blk),
        in_specs=[
            pl.BlockSpec((rows, D_MODEL), lambda l, j: (0, 0)),
            pl.BlockSpec((None, D_MODEL, D_MODEL), lambda l, j: (l, 0, j)),
            pl.BlockSpec((None, None, 1, D_MODEL), lambda l, j: (l, j, 0, 0)),
        ],
        out_specs=pl.BlockSpec((None, rows, D_MODEL), lambda l, j: (l, 0, j)),
        compiler_params=_params(("arbitrary", "arbitrary")),
        name="ada",
    )(cvec, w_ada, b_ada.reshape(depth, n_blk, 1, D_MODEL))


def _rope128(v, cos, s1, s2, half):
    up = pltpu.roll(v, LANE - half, 1)
    dn = pltpu.roll(v, half, 1)
    return v * cos + up * s1 + dn * s2


def _head_rms(v, bo, gain):
    sq = v * v
    hi = sq.astype(BF16)
    lo = (sq - hi.astype(F32)).astype(BF16)
    ms = _dot(hi, bo) + _dot(lo, bo)
    return v * lax.rsqrt(ms + EPS) * gain


def _feat_kernel(*refs, use_rope):
    (x_ref, sh_ref, sc_ref, gpre_ref, win_ref, wq_ref, wkk_ref, wkv_ref,
     aqn_ref, akn_ref, bqn_ref, bkvn_ref, bo_ref, lb_ref) = refs[:14]
    rest = refs[14:]
    if use_rope:
        ca_ref, s1a_ref, s2a_ref, cb_ref, s1b_ref, s2b_ref = rest[:6]
        rest = rest[6:]
    aq_o, ak_o, avt_o, bq_o, bk_o, bvt_o, c_o, g_o = rest

    x = x_ref[...]
    ms = jnp.mean(x * x, axis=-1, keepdims=True)
    h = x * lax.rsqrt(ms + EPS) * gpre_ref[...]
    h = h * (1.0 + sc_ref[...]) + sh_ref[...]
    p = _dot(h.astype(BF16), win_ref[...])
    bo = bo_ref[...]

    def rope_a(v):
        if not use_rope:
            return v
        return _rope128(v, ca_ref[...], s1a_ref[...], s2a_ref[...], HEAD_DIM // 2)

    def rope_b(v):
        if not use_rope:
            return v
        return _rope128(v, cb_ref[...], s1b_ref[...], s2b_ref[...], B_ROPE // 2)

    aqn = aqn_ref[...]
    for half in range(2):
        v = _head_rms(p[:, OFF_AQ + 256 * half:OFF_AQ + 256 * (half + 1)], bo, aqn)
        for s in range(2):
            blk = rope_a(v[:, LANE * s:LANE * (s + 1)]) * (HEAD_DIM ** -0.5 * LOG2_E)
            lo = 256 * half + LANE * s
            aq_o[lo:lo + LANE, :] = blk.T.astype(BF16)
    v = _head_rms(p[:, OFF_AK:OFF_AK + 256], bo, akn_ref[...])
    for s in range(2):
        ak_o[:, LANE * s:LANE * (s + 1)] = rope_a(v[:, LANE * s:LANE * (s + 1)]).astype(BF16)
    avt_o[...] = p[:, OFF_AV:OFF_AV + LANE].T.astype(BF16)

    bqd = p[:, OFF_BQD:OFF_BQD + 256]
    ms = jnp.sum(bqd * bqd, axis=-1, keepdims=True) * (1.0 / B_Q_RANK)
    qn = (bqd * lax.rsqrt(ms + EPS) * bqn_ref[...]).astype(BF16)
    bq = _dot(qn, wq_ref[...])
    bkvd = p[:, OFF_BKVD:OFF_BKVD + LANE]
    ms = jnp.mean(bkvd * bkvd, axis=-1, keepdims=True)
    kvn = (bkvd * lax.rsqrt(ms + EPS) * bkvn_ref[...]).astype(BF16)
    bkn = _dot(kvn, wkk_ref[...])
    bv = _dot(kvn, wkv_ref[...])
    kpe = rope_b(p[:, OFF_KPE:OFF_KPE + LANE])
    b_scale = (B_NOPE + B_ROPE) ** -0.5 * LOG2_E
    for hh in range(B_HEADS):
        sl = slice(LANE * hh, LANE * (hh + 1))
        bq_o[sl, :] = (rope_b(bq[:, sl]) * b_scale).T.astype(BF16)
        bk_o[:, sl] = (bkn[:, sl] + kpe).astype(BF16)
    bvt_o[...] = bv.T.astype(BF16)

    c_o[:, C_W * C_Q:C_W * (C_Q + 1)] = _silu(p[:, OFF_CQ:OFF_CQ + C_W])
    for d, (off, ck) in enumerate(((OFF_CFF, C_KF), (OFF_CFB, C_KB))):
        lb = lb_ref[d:d + 1, :]
        sp, sn = _sigmoid_pair(p[:, off:off + C_W])
        f = lb + (1.0 - lb) * sp
        g2 = jnp.log2(jnp.maximum(f, F_TINY))
        hi = g2.astype(BF16)
        g_o[:, C_W * 2 * d:C_W * (2 * d + 1)] = hi
        g_o[:, C_W * (2 * d + 1):C_W * (2 * d + 2)] = (g2 - hi.astype(F32)).astype(BF16)
        c_o[:, C_W * ck:C_W * (ck + 1)] = (1.0 - lb) * sn
    c_o[:, C_W * C_V:C_W * (C_V + 1)] = p[:, OFF_CI:OFF_CI + C_W]
    c_o[:, C_W * C_GATE:C_W * (C_GATE + 1)] = p[:, OFF_CG:OFF_CG + C_W]


def _features(xs, mod, w, layer, mod_row, rope):
    bsz, t, _ = xs.shape
    tm = min(512, t)
    nt = t // tm
    nb_rows = w["n_mod_rows"]

    def mod_spec(j):
        return pl.BlockSpec((None, 1, D_MODEL),
                            lambda b, i: ((layer * nb_rows + mod_row(b)) * 6 + j, 0, 0))

    in_specs = [
        pl.BlockSpec((None, tm, D_MODEL), lambda b, i: (b, i, 0)),
        mod_spec(0), mod_spec(1),
        _const_spec((None, 1, D_MODEL), (layer, 0, 0)),
        _const_spec((None, D_MODEL, N_COL), (layer, 0, 0)),
        _const_spec((None, 256, 512), (layer, 0, 0)),
        _const_spec((None, LANE, 512), (layer, 0, 0)),
        _const_spec((None, LANE, 256), (layer, 0, 0)),
        _const_spec((None, 1, 256), (layer, 0, 0)),
        _const_spec((None, 1, 256), (layer, 0, 0)),
        _const_spec((None, 1, 256), (layer, 0, 0)),
        _const_spec((None, 1, LANE), (layer, 0, 0)),
        _const_spec((256, 256), (0, 0)),
        _const_spec((None, 2, C_W), (layer, 0, 0)),
    ]
    args = [xs, mod, mod, w["g_pre_mix"], w["w_in"], w["w_q_up"], w["w_kv_k"], w["w_kv_v"],
            w["a_q_norm"], w["a_k_norm"], w["b_q_norm"], w["b_kv_norm"], w["block_ones"], w["lower"]]
    if rope is not None:
        in_specs += [pl.BlockSpec((tm, LANE), lambda b, i: (i, 0))] * 6
        args += list(rope)
    out_shape = [
        jax.ShapeDtypeStruct((bsz, 512, t), BF16),
        jax.ShapeDtypeStruct((bsz, t, 256), BF16),
        jax.ShapeDtypeStruct((bsz, nt, LANE, tm), BF16),
        jax.ShapeDtypeStruct((bsz, 512, t), BF16),
        jax.ShapeDtypeStruct((bsz, t, 512), BF16),
        jax.ShapeDtypeStruct((bsz, nt, 256, tm), BF16),
        jax.ShapeDtypeStruct((bsz, t, N_SLAB * C_W), F32),
        jax.ShapeDtypeStruct((bsz, t, 4 * C_W), BF16),
    ]
    out_specs = [
        pl.BlockSpec((None, 512, tm), lambda b, i: (b, 0, i)),
        pl.BlockSpec((None, tm, 256), lambda b, i: (b, i, 0)),
        pl.BlockSpec((None, None, LANE, tm), lambda b, i: (b, i, 0, 0)),
        pl.BlockSpec((None, 512, tm), lambda b, i: (b, 0, i)),
        pl.BlockSpec((None, tm, 512), lambda b, i: (b, i, 0)),
        pl.BlockSpec((None, None, 256, tm), lambda b, i: (b, i, 0, 0)),
        pl.BlockSpec((None, tm, N_SLAB * C_W), lambda b, i: (b, i, 0)),
        pl.BlockSpec((None, tm, 4 * C_W), lambda b, i: (b, i, 0)),
    ]
    outs = pl.pallas_call(
        functools.partial(_feat_kernel, use_rope=rope is not None),
        out_shape=out_shape, grid=(bsz, nt), in_specs=in_specs, out_specs=out_specs,
        compiler_params=_params(("parallel", "parallel")),
        name="feat_rope" if rope is not None else "feat_ctx",
    )(*args)
    return dict(zip(("aq", "ak", "avt", "bq", "bk", "bvt", "c", "g"), outs))


ACC_ROWS = B_V + 16
NEG_BIG = -1e30
LOG2_E = 1.4426950408889634


def _attn_kernel(*refs, n_src, n_heads, shared_kv):
    q_ref = refs[0]
    src_refs = refs[1:1 + 2 * n_src]
    o_ref = refs[1 + 2 * n_src]
    rhs_scr, m_scr, acc_scr, st_scr, mx_scr = refs[2 + 2 * n_src:]
    tq = q_ref.shape[1]
    n_grp, k_dim, grp_w = rhs_scr.shape
    heads_per_grp = n_heads // n_grp

    if shared_kv:
        for h in range(n_heads):
            rhs_scr[0, :, tq * h:tq * (h + 1)] = q_ref[HEAD_DIM * h:HEAD_DIM * (h + 1), :]
    else:
        rhs_scr[...] = jnp.zeros(rhs_scr.shape, BF16)
        for h in range(n_heads):
            g, j = divmod(h, heads_per_grp)
            rhs_scr[g, LANE * j:LANE * (j + 1), tq * j:tq * (j + 1)] = q_ref[LANE * h:LANE * (h + 1), :]
    m_scr[...] = jnp.full(m_scr.shape, NEG_BIG, F32)
    acc_scr[...] = jnp.zeros(acc_scr.shape, F32)

    def stage(s, c, slot):
        k_ref, vt_ref = src_refs[2 * s], src_refs[2 * s + 1]
        tk = vt_ref.shape[2]
        rows = pl.ds(pl.multiple_of(c * tk, tk), tk)
        for g in range(n_grp):
            cols = slice(grp_w * g, grp_w * (g + 1))
            kc = k_ref[rows, 0:k_dim] if shared_kv else k_ref[rows, k_dim * g:k_dim * (g + 1)]
            st = _dot(kc, rhs_scr[g])
            st_scr[slot, 0:tk, cols] = st
            mx_scr[slot, :, cols] = jnp.max(st, axis=0, keepdims=True)

    def consume(s, c, slot):
        vt_ref = src_refs[2 * s + 1]
        tk = vt_ref.shape[2]
        ones = jnp.ones((ACC_ROWS - B_V, tk), BF16)
        for g in range(n_grp):
            cols = slice(grp_w * g, grp_w * (g + 1))
            m_old = m_scr[:, cols]
            m_new = jnp.maximum(m_old, mx_scr[slot, :, cols])
            pt = jnp.exp2(st_scr[slot, 0:tk, cols] - m_new).astype(BF16)
            alpha = jnp.exp2(m_old - m_new)
            m_scr[:, cols] = m_new
            if shared_kv:
                vt = jnp.concatenate([vt_ref[c], ones], axis=0)
                acc_scr[:, cols] = alpha * acc_scr[:, cols] + _dot(vt, pt)
            else:
                for j in range(heads_per_grp):
                    h = g * heads_per_grp + j
                    hc = slice(tq * h, tq * (h + 1))
                    lc = slice(tq * j, tq * (j + 1))
                    vt = jnp.concatenate([vt_ref[c, B_V * h:B_V * (h + 1), :], ones], axis=0)
                    acc_scr[:, hc] = alpha[:, lc] * acc_scr[:, hc] + _dot(vt, pt[:, lc])

    n0 = src_refs[1].shape[0]
    n_loop = (n0 - 2) // 2 if n0 >= 4 else 0
    stage(0, 0, 0)
    if n_loop:
        def pair(i, carry):
            stage(0, 2 * i + 1, 1)
            consume(0, 2 * i, 0)
            stage(0, 2 * i + 2, 0)
            consume(0, 2 * i + 1, 1)
            return carry
        lax.fori_loop(0, n_loop, pair, 0)
    tail = [(0, c) for c in range(2 * n_loop, n0)]
    tail += [(s, c) for s in range(1, n_src) for c in range(src_refs[2 * s + 1].shape[0])]
    for i, (s, c) in enumerate(tail):
        if i + 1 < len(tail):
            stage(*tail[i + 1], (i + 1) % 2)
        consume(s, c, i % 2)

    for h in range(n_heads):
        acc = acc_scr[:, tq * h:tq * (h + 1)]
        o_ref[B_V * h:B_V * (h + 1), :] = (acc[:B_V] * (1.0 / acc[B_V:B_V + 1])).astype(BF16)


def _attention(q, srcs, *, n_kv, n_heads, shared_kv, name):
    bsz, _, t = q.shape
    tq = min(256, t)
    q_r = (HEAD_DIM if shared_kv else LANE) * n_heads
    k_w = LANE * (1 if shared_kv else n_heads)
    v_r = B_V * (1 if shared_kv else n_heads)
    rhs_shape = (1, HEAD_DIM, n_heads * tq) if shared_kv else (n_heads // 2, 2 * LANE, 2 * tq)
    in_specs = [pl.BlockSpec((None, q_r, tq), lambda b, g, i: (b, g, i))]
    args = [q]
    for k, vt in srcs:
        length = k.shape[1]
        n_chunk, tk = vt.shape[1], vt.shape[3]
        in_specs.append(pl.BlockSpec((None, length, k_w), lambda b, g, i: (b, 0, g)))
        in_specs.append(pl.BlockSpec((None, n_chunk, v_r, tk), lambda b, g, i: (b, 0, g, 0)))
        args += [k, vt]
    rows = B_V * n_heads
    return pl.pallas_call(
        functools.partial(_attn_kernel, n_src=len(srcs), n_heads=n_heads, shared_kv=shared_kv),
        out_shape=jax.ShapeDtypeStruct((bsz, rows * n_kv, t), BF16),
        grid=(bsz, n_kv, t // tq),
        in_specs=in_specs,
        out_specs=pl.BlockSpec((None, rows, tq), lambda b, g, i: (b, g, i)),
        scratch_shapes=[pltpu.VMEM(rhs_shape, BF16),
                        pltpu.VMEM((1, n_heads * tq), F32),
                        pltpu.VMEM((ACC_ROWS, n_heads * tq), F32),
                        pltpu.VMEM((2, max(vt.shape[3] for _, vt in srcs), n_heads * tq), F32),
                        pltpu.VMEM((2, 1, n_heads * tq), F32)],
        compiler_params=_params(("parallel", "parallel", "parallel")),
        name=name,
    )(*args)


def _scan_constants(reverse):
    c = SCAN_C
    t = np.arange(c)[:, None]
    s = np.arange(c)[None, :]
    cum = (s >= t) if reverse else (s <= t)
    msk = np.zeros((N_LEVELS + 1, c, c), np.float32)
    for lvl in range(N_LEVELS):
        w = (c // 2) >> lvl
        same = (t // (2 * w)) == (s // (2 * w))
        t_hi = (t % (2 * w)) >= w
        s_hi = (s % (2 * w)) >= w
        msk[lvl] = (same & ~t_hi & s_hi) if reverse else (same & t_hi & ~s_hi)
    msk[N_LEVELS] = (t == s)
    return jnp.asarray(cum, BF16), jnp.asarray(np.tile(msk, (1, 1, C_HEADS)), F32)


def _boundary_rows(b, lvl, reverse):
    w = (SCAN_C // 2) >> lvl
    width = b.shape[1]
    off = w if reverse else w - 1
    if w == 1:
        odd = lax.broadcasted_iota(jnp.int32, b.shape, 0) % 2 == 1
        if reverse:
            return jnp.where(odd, b, pltpu.roll(b, SCAN_C - 1, 0))
        return jnp.where(odd, pltpu.roll(b, 1, 0), b)
    if w == 2:
        low = lax.broadcasted_iota(jnp.int32, (8, width), 0) < 4
        pieces = []
        for r0 in range(0, SCAN_C, 8):
            first = jnp.broadcast_to(b[r0 + off:r0 + off + 1, :], (8, width))
            second = jnp.broadcast_to(b[r0 + 4 + off:r0 + 5 + off, :], (8, width))
            pieces.append(jnp.where(low, first, second))
        return jnp.concatenate(pieces, axis=0)
    pieces = [jnp.broadcast_to(b[r0 + off:r0 + off + 1, :], (2 * w, width))
              for r0 in range(0, SCAN_C, 2 * w)]
    return pieces[0] if len(pieces) == 1 else jnp.concatenate(pieces, axis=0)


def _hgrn_kernel(q_ref, k_ref, ghi_ref, glo_ref, v_ref, s0_ref, cum_ref, msk_ref, bm_ref,
                 o_ref, sfin_ref, st_ref, *, reverse, n_chunk):
    i = pl.program_id(1)

    @pl.when(i == 0)
    def _():
        st_ref[...] = s0_ref[...]

    bm = bm_ref[...]
    bm16 = bm.astype(BF16)
    cum = cum_ref[...]

    def stack_heads(a):
        a16 = a.astype(BF16)
        return jnp.concatenate([a16] * C_HEADS, axis=0) * bm16

    grp = min(n_chunk, SCAN_GROUP)
    n_grp = n_chunk // grp

    def group(jg, carry):
        jgg = (n_grp - 1 - jg) if reverse else jg
        base = jgg * (grp * SCAN_C)
        order = list(range(grp - 1, -1, -1)) if reverse else list(range(grp))
        rows = [pl.ds(pl.multiple_of(base + c * SCAN_C, SCAN_C), SCAN_C) for c in range(grp)]
        q = [q_ref[r, :] for r in rows]
        k = [k_ref[r, :] for r in rows]
        v = [v_ref[r, :] for r in rows]
        b = [_dot(cum, ghi_ref[r, :]) + _dot(cum, glo_ref[r, :]) for r in rows]
        b_all = [bc[0:1, :] if reverse else bc[SCAN_C - 1:SCAN_C, :] for bc in b]

        q16 = [qc.astype(BF16) for qc in q]
        kst = [stack_heads(kc) for kc in k]
        sc = [msk_ref[N_LEVELS] * _dot_nt(q16[c], kst[c]) for c in range(grp)]
        for lvl in range(N_LEVELS):
            for c in range(grp):
                d = b[c] - _boundary_rows(b[c], lvl, reverse)
                e16 = jnp.exp2(-jnp.abs(d)).astype(BF16)
                kl = kst[c] * jnp.concatenate([e16] * C_HEADS, axis=0)
                sc[c] = sc[c] + msk_ref[lvl] * _dot_nt(q16[c] * e16, kl)

        ds = [_dot_tn(v[c].astype(BF16), (k[c] * jnp.exp2(b_all[c] - b[c])).astype(BF16)) * bm
              for c in range(grp)]
        st = st_ref[...]
        st_in = [None] * grp
        for c in order:
            st_in[c] = st
            st = st * jnp.exp2(b_all[c]) + ds[c]
        st_ref[...] = st

        for c in order:
            qdec = q[c] * jnp.exp2(b[c])
            o_ref[rows[c], :] = (_dot(sc[c].astype(BF16), stack_heads(v[c]))
                                 + _dot_nt(qdec.astype(BF16), st_in[c].astype(BF16)))
        return carry

    if n_grp == 1:
        group(0, 0)
    else:
        lax.fori_loop(0, n_grp, group, 0)

    @pl.when(i == pl.num_programs(1) - 1)
    def _():
        sfin_ref[...] = st_ref[...]


def _hgrn(c_slab, g16, s0, consts, block_mask, reverse):
    bsz, t, _ = c_slab.shape
    tb = min(512, t)
    nblk = t // tb
    cum, msk = consts

    def blk(i):
        return (nblk - 1 - i) if reverse else i

    def slab_spec(j):
        return pl.BlockSpec((None, tb, C_W), lambda b, i: (b, blk(i), j))

    g0 = 2 if reverse else 0

    return pl.pallas_call(
        functools.partial(_hgrn_kernel, reverse=reverse, n_chunk=tb // SCAN_C),
        out_shape=[jax.ShapeDtypeStruct((bsz, t, C_W), F32),
                   jax.ShapeDtypeStruct((bsz, C_W, C_W), F32)],
        grid=(bsz, nblk),
        in_specs=[
            slab_spec(C_Q), slab_spec(C_KB if reverse else C_KF), slab_spec(g0), slab_spec(g0 + 1),
            slab_spec(C_V),
            pl.BlockSpec((None, C_W, C_W), lambda b, i: (b, 0, 0)),
            _const_spec(cum.shape, (0, 0)),
            _const_spec(msk.shape, (0, 0, 0)),
            _const_spec(block_mask.shape, (0, 0)),
        ],
        out_specs=[pl.BlockSpec((None, tb, C_W), lambda b, i: (b, blk(i), 0)),
                   pl.BlockSpec((None, C_W, C_W), lambda b, i: (b, 0, 0))],
        scratch_shapes=[pltpu.VMEM((C_W, C_W), F32)],
        compiler_params=_params(("parallel", "arbitrary")),
        name="hgrn_bwd" if reverse else "hgrn_fwd",
    )(c_slab, c_slab, g16, g16, c_slab, s0, cum, msk, block_mask)


def _rms(v, gain):
    ms = jnp.mean(v * v, axis=-1, keepdims=True)
    return v * lax.rsqrt(ms + EPS) * gain


def _post_kernel(x_ref, ya_ref, yb_ref, of_ref, ob_ref, gate_ref,
                 gtm_ref, shf_ref, scf_ref, gtf_ref,
                 gpm_ref, gpf_ref, gqf_ref, con_ref, bo_ref,
                 woa_ref, wob_ref, woc_ref, w1_ref, w2_ref, o_ref):
    o = of_ref[...] + ob_ref[...]
    yc = (_head_rms(o, bo_ref[...], con_ref[...]) * _silu(gate_ref[...])).astype(BF16)
    mix = (_dot_tn(ya_ref[...], woa_ref[...]) + _dot_tn(yb_ref[...], wob_ref[...])
           + _dot(yc, woc_ref[...]))
    x1 = x_ref[...] + gtm_ref[...] * _rms(mix, gpm_ref[...])
    h = _rms(x1, gpf_ref[...]) * (1.0 + scf_ref[...]) + shf_ref[...]
    u = jnp.maximum(_dot(h.astype(BF16), w1_ref[...]), 0.0)
    ff = _dot((u * u).astype(BF16), w2_ref[...])
    o_ref[...] = x1 + gtf_ref[...] * _rms(ff, gqf_ref[...])


def _post(xs, yta, ytb, o_f, o_b, c_slab, mod, w, layer, mod_row):
    bsz, t, _ = xs.shape
    tm = min(512, t)
    nb_rows = w["n_mod_rows"]

    def mod_spec(j):
        return pl.BlockSpec((None, 1, D_MODEL),
                            lambda b, i: ((layer * nb_rows + mod_row(b)) * 6 + j, 0, 0))

    def vec_spec():
        return _const_spec((None, 1, D_MODEL), (layer, 0, 0))

    in_specs = [
        pl.BlockSpec((None, tm, D_MODEL), lambda b, i: (b, i, 0)),
        pl.BlockSpec((None, A_OUT, tm), lambda b, i: (b, 0, i)),
        pl.BlockSpec((None, B_OUT, tm), lambda b, i: (b, 0, i)),
        pl.BlockSpec((None, tm, C_W), lambda b, i: (b, i, 0)),
        pl.BlockSpec((None, tm, C_W), lambda b, i: (b, i, 0)),
        pl.BlockSpec((None, tm, C_W), lambda b, i: (b, i, C_GATE)),
        mod_spec(2), mod_spec(3), mod_spec(4), mod_spec(5),
        vec_spec(), vec_spec(), vec_spec(),
        _const_spec((None, 1, C_W), (layer, 0, 0)),
        _const_spec((256, 256), (0, 0)),
        _const_spec((None, A_OUT, D_MODEL), (layer, 0, 0)),
        _const_spec((None, B_OUT, D_MODEL), (layer, 0, 0)),
        _const_spec((None, C_OUT, D_MODEL), (layer, 0, 0)),
        _const_spec((None, D_MODEL, D_FF), (layer, 0, 0)),
        _const_spec((None, D_FF, D_MODEL), (layer, 0, 0)),
    ]
    return pl.pallas_call(
        _post_kernel,
        out_shape=jax.ShapeDtypeStruct((bsz, t, D_MODEL), F32),
        grid=(bsz, t // tm),
        in_specs=in_specs,
        out_specs=pl.BlockSpec((None, tm, D_MODEL), lambda b, i: (b, i, 0)),
        compiler_params=_params(("parallel", "parallel")),
        name="post",
    )(xs, yta, ytb, o_f, o_b, c_slab, mod, mod, mod, mod,
      w["g_post_mix"], w["g_pre_ffn"], w["g_post_ffn"], w["c_out_norm"], w["block_ones"],
      w["w_out_a"], w["w_out_b"], w["w_out_c"], w["w_ff1"], w["w_ff2"])


def _w_in_columns():
    src = np.full((N_COL,), -1, np.int64)
    a_k0 = A_HEADS * HEAD_DIM
    a_v0 = a_k0 + A_KV_HEADS * HEAD_DIM
    b_qd0 = a_v0 + A_KV_HEADS * HEAD_DIM
    b_kv0 = b_qd0 + B_Q_RANK
    b_kr0 = b_kv0 + B_KV_RANK
    c0 = b_kr0 + B_ROPE
    src[OFF_AQ:OFF_AQ + 512] = np.arange(512)
    for g in range(A_KV_HEADS):
        for rep in range(2):
            lo = OFF_AK + (2 * g + rep) * HEAD_DIM
            src[lo:lo + HEAD_DIM] = a_k0 + g * HEAD_DIM + np.arange(HEAD_DIM)
    src[OFF_AV:OFF_AV + 128] = a_v0 + np.arange(128)
    src[OFF_BQD:OFF_BQD + B_Q_RANK] = b_qd0 + np.arange(B_Q_RANK)
    src[OFF_BKVD:OFF_BKVD + B_KV_RANK] = b_kv0 + np.arange(B_KV_RANK)
    src[OFF_KPE + B_NOPE:OFF_KPE + B_NOPE + B_ROPE] = b_kr0 + np.arange(B_ROPE)
    src[OFF_CQ:OFF_CQ + 5 * C_W] = c0 + np.arange(5 * C_W)
    return src


def _gather_cols(w, src):
    keep = jnp.asarray(src >= 0)
    return jnp.where(keep, jnp.take(w, jnp.asarray(np.maximum(src, 0)), axis=-1), 0.0)


def _prepare_weights(p, n_mod_rows):
    depth = p["w_in"].shape[0]
    w = {"n_mod_rows": n_mod_rows}
    w["w_in"] = _gather_cols(p["w_in"], _w_in_columns()).astype(BF16)
    src = np.full((B_HEADS * LANE,), -1, np.int64)
    for hh in range(B_HEADS):
        src[hh * LANE:hh * LANE + B_NOPE + B_ROPE] = hh * (B_NOPE + B_ROPE) + np.arange(B_NOPE + B_ROPE)
    wq = _gather_cols(p["w_q_up"], src)
    w["w_q_up"] = jnp.pad(wq, ((0, 0), (0, 256 - B_Q_RANK), (0, 0))).astype(BF16)
    src = np.full((B_HEADS * LANE,), -1, np.int64)
    srcv = np.zeros((B_HEADS * B_V,), np.int64)
    for hh in range(B_HEADS):
        src[hh * LANE:hh * LANE + B_NOPE] = hh * (B_NOPE + B_V) + np.arange(B_NOPE)
        srcv[hh * B_V:(hh + 1) * B_V] = hh * (B_NOPE + B_V) + B_NOPE + np.arange(B_V)
    w["w_kv_k"] = _gather_cols(p["w_kv_up"], src).astype(BF16)
    w["w_kv_v"] = _gather_cols(p["w_kv_up"], srcv).astype(BF16)
    w["a_q_norm"] = jnp.tile(p["a_q_norm"], (1, 4))[:, None, :]
    w["a_k_norm"] = jnp.tile(p["a_k_norm"], (1, 4))[:, None, :]
    w["b_q_norm"] = jnp.pad(p["b_q_norm"], ((0, 0), (0, 256 - B_Q_RANK)))[:, None, :]
    w["b_kv_norm"] = p["b_kv_norm"][:, None, :]
    w["c_out_norm"] = jnp.tile(p["c_out_norm"], (1, C_HEADS))[:, None, :]
    for name in ("g_pre_mix", "g_post_mix", "g_pre_ffn", "g_post_ffn"):
        w[name] = p[name][:, None, :]
    head = np.arange(256) // HEAD_DIM
    same_head = head[:, None] == head[None, :]
    w["block_ones"] = jnp.asarray(same_head / float(HEAD_DIM), BF16)
    w["block_mask"] = jnp.asarray(same_head, F32)
    p_lb = jax.nn.softmax(p["c_lower_bounds"].astype(F32), axis=0)
    w["lower"] = jnp.cumsum(p_lb, axis=0) - p_lb[:1]
    w["w_out_a"] = p["w_out"][:, :A_OUT].astype(BF16)
    w["w_out_b"] = p["w_out"][:, A_OUT:A_OUT + B_OUT].astype(BF16)
    w["w_out_c"] = p["w_out"][:, A_OUT + B_OUT:].astype(BF16)
    w["w_ff1"] = p["w_ff1"].astype(BF16)
    w["w_ff2"] = p["w_ff2"].astype(BF16)
    del depth
    return w


def _rope_tables(n_tok):
    tok = np.arange(n_tok)
    row = (tok // GRID_W).astype(np.float32)[:, None]
    col = (tok % GRID_W).astype(np.float32)[:, None]

    def angles(rot_dim):
        n_freq = rot_dim // 4
        inv = jnp.asarray(ROPE_THETA, F32) ** (-jnp.arange(n_freq, dtype=F32) / n_freq)
        ang = jnp.concatenate([jnp.asarray(row) * inv, jnp.asarray(col) * inv], axis=-1)
        return jnp.cos(ang), jnp.sin(ang)

    zeros = lambda n: jnp.zeros((n_tok, n), F32)
    ones = lambda n: jnp.ones((n_tok, n), F32)
    cos, sin = angles(HEAD_DIM)
    ca = jnp.concatenate([cos, cos, cos, cos], axis=-1)
    s1a = jnp.concatenate([-sin, zeros(32), -sin, zeros(32)], axis=-1)
    s2a = jnp.concatenate([zeros(32), sin, zeros(32), sin], axis=-1)
    cos, sin = angles(B_ROPE)
    cb = jnp.concatenate([ones(64), cos, cos, ones(32)], axis=-1)
    s1b = jnp.concatenate([zeros(64), -sin, zeros(16), zeros(32)], axis=-1)
    s2b = jnp.concatenate([zeros(64), zeros(16), sin, zeros(32)], axis=-1)
    return ca, s1a, s2a, cb, s1b, s2b


def kernel(x, c, ctx, c_ctx, w_ada, b_ada, g_pre_mix, g_post_mix, g_pre_ffn, g_post_ffn, w_in, a_q_norm, a_k_norm, b_q_norm, w_q_up, b_kv_norm, w_kv_up, c_lower_bounds, c_out_norm, w_out, w_ff1, w_ff2):
    bsz, n_lat, _ = x.shape
    depth = w_in.shape[0]
    n_mod_rows = -(-(bsz + 1) // 8) * 8
    params = dict(w_in=w_in, a_q_norm=a_q_norm, a_k_norm=a_k_norm, b_q_norm=b_q_norm, w_q_up=w_q_up,
                  b_kv_norm=b_kv_norm, w_kv_up=w_kv_up, c_lower_bounds=c_lower_bounds,
                  c_out_norm=c_out_norm, w_out=w_out, w_ff1=w_ff1, w_ff2=w_ff2,
                  g_pre_mix=g_pre_mix, g_post_mix=g_post_mix, g_pre_ffn=g_pre_ffn, g_post_ffn=g_post_ffn)
    w = _prepare_weights(params, n_mod_rows)
    rope = _rope_tables(n_lat)
    scan_f = _scan_constants(False)
    scan_b = _scan_constants(True)

    cvec = jnp.concatenate([c, c_ctx[None, :], jnp.zeros((n_mod_rows - bsz - 1, D_MODEL), F32)], axis=0)
    mod = _ada(cvec, w_ada, b_ada).reshape(depth * n_mod_rows * 6, 1, D_MODEL)

    lat_row = lambda b: b
    ctx_row = lambda b: bsz
    zero_state = jnp.zeros((bsz, C_W, C_W), F32)
    attn_a = functools.partial(_attention, n_kv=A_KV_HEADS, n_heads=A_HEADS // A_KV_HEADS, shared_kv=True)
    attn_b = functools.partial(_attention, n_kv=1, n_heads=B_HEADS, shared_kv=False)

    xc = ctx
    for layer in range(depth):
        need_ctx = layer < depth - 1
        fl = _features(x, mod, w, layer, lat_row, rope)
        fc = _features(xc, mod, w, layer, ctx_row, None)
        yta = attn_a(fl["aq"], [(fl["ak"], fl["avt"]), (fc["ak"], fc["avt"])], name="attn_a")
        ytb = attn_b(fl["bq"], [(fl["bk"], fl["bvt"]), (fc["bk"], fc["bvt"])], name="attn_b")
        ocf, s_f = _hgrn(fc["c"], fc["g"], zero_state, scan_f, w["block_mask"], False)
        ocb, s_b = _hgrn(fc["c"], fc["g"], zero_state, scan_b, w["block_mask"], True)
        olf, _ = _hgrn(fl["c"], fl["g"], s_f, scan_f, w["block_mask"], False)
        olb, _ = _hgrn(fl["c"], fl["g"], s_b, scan_b, w["block_mask"], True)
        x_new = _post(x, yta, ytb, olf, olb, fl["c"], mod, w, layer, lat_row)
        if need_ctx:
            yta_c = attn_a(fc["aq"], [(fc["ak"], fc["avt"])], name="attn_a_ctx")
            ytb_c = attn_b(fc["bq"], [(fc["bk"], fc["bvt"])], name="attn_b_ctx")
            xc = _post(xc, yta_c, ytb_c, ocf, ocb, fc["c"], mod, w, layer, ctx_row)
        x = x_new
    return x
```

```python
import functools

import numpy as np
import jax
import jax.numpy as jnp
from jax import lax
from jax.experimental import pallas as pl
from jax.experimental.pallas import tpu as pltpu

F32 = jnp.float32
BF16 = jnp.bfloat16

D_MODEL = 1024
GRID_W = 64
HEAD_DIM = 64
A_HEADS = 8
A_KV_HEADS = 2
B_HEADS = 4
B_Q_RANK = 192
B_KV_RANK = 128
B_NOPE = 64
B_ROPE = 32
B_V = 64
C_HEADS = 4
C_DK = 64
C_DV = 64
D_FF = 4 * D_MODEL
A_OUT = A_HEADS * HEAD_DIM
B_OUT = B_HEADS * B_V
C_OUT = C_HEADS * C_DV
C_W = C_HEADS * C_DK
ROPE_THETA = 10000.0
EPS = 1e-6
F_TINY = 1e-30

LANE = 128
VMEM_LIMIT = 56 * 1024 * 1024

OFF_AQ = 0
OFF_AK = 512
OFF_AV = 768
OFF_BQD = 896
OFF_BKVD = 1152
OFF_KPE = 1280
OFF_CQ = 1408
OFF_CFF = 1664
OFF_CFB = 1920
OFF_CI = 2176
OFF_CG = 2432
N_COL = 2688

C_Q, C_KF, C_KB, C_V, C_GATE = range(5)
N_SLAB = 5

SCAN_C = 64
N_LEVELS = 6
SCAN_GROUP = 8


def _dot(a, b):
    return jnp.dot(a, b, preferred_element_type=F32)


def _dot_nt(a, b):
    return lax.dot_general(a, b, (((1,), (1,)), ((), ())), preferred_element_type=F32)


def _dot_tn(a, b):
    return lax.dot_general(a, b, (((0,), (0,)), ((), ())), preferred_element_type=F32)


def _sigmoid_pair(z):
    e = jnp.exp(-jnp.abs(z))
    inv = 1.0 / (1.0 + e)
    small = e * inv
    pos = z >= 0
    return jnp.where(pos, inv, small), jnp.where(pos, small, inv)


def _silu(z):
    s, _ = _sigmoid_pair(z)
    return z * s


def _const_spec(shape, index):
    return pl.BlockSpec(shape, lambda *_: index, pipeline_mode=pl.Buffered(1))


def _params(sem):
    return pltpu.CompilerParams(dimension_semantics=sem, vmem_limit_bytes=VMEM_LIMIT)


def _ada_kernel(c_ref, w_ref, b_ref, o_ref):
    a = _silu(c_ref[...])
    w = w_ref[...]
    a_hi = a.astype(BF16)
    a_lo = (a - a_hi.astype(F32)).astype(BF16)
    w_hi = w.astype(BF16)
    w_lo = (w - w_hi.astype(F32)).astype(BF16)
    acc = _dot(a_hi, w_hi) + (_dot(a_hi, w_lo) + _dot(a_lo, w_hi))
    o_ref[...] = acc + b_ref[...]


def _ada(cvec, w_ada, b_ada):
    depth = w_ada.shape[0]
    rows = cvec.shape[0]
    n_blk = w_ada.shape[2] // D_MODEL
    return pl.pallas_call(
        _ada_kernel,
        out_shape=jax.ShapeDtypeStruct((depth, rows, n_blk * D_MODEL), F32),
        grid=(depth, n_blk),
        in_specs=[
            pl.BlockSpec((rows, D_MODEL), lambda l, j: (0, 0)),
            pl.BlockSpec((None, D_MODEL, D_MODEL), lambda l, j: (l, 0, j)),
            pl.BlockSpec((None, None, 1, D_MODEL), lambda l, j: (l, j, 0, 0)),
        ],
        out_specs=pl.BlockSpec((None, rows, D_MODEL), lambda l, j: (l, 0, j)),
        compiler_params=_params(("arbitrary", "arbitrary")),
        name="ada",
    )(cvec, w_ada, b_ada.reshape(depth, n_blk, 1, D_MODEL))


def _rope128(v, cos, s1, s2, half):
    up = pltpu.roll(v, LANE - half, 1)
    dn = pltpu.roll(v, half, 1)
    return v * cos + up * s1 + dn * s2


def _head_rms(v, bo, gain):
    sq = v * v
    hi = sq.astype(BF16)
    lo = (sq - hi.astype(F32)).astype(BF16)
    ms = _dot(hi, bo) + _dot(lo, bo)
    return v * lax.rsqrt(ms + EPS) * gain


def _feat_kernel(*refs, use_rope):
    (x_ref, sh_ref, sc_ref, gpre_ref, win_ref, wq_ref, wkk_ref, wkv_ref,
     aqn_ref, akn_ref, bqn_ref, bkvn_ref, bo_ref, lb_ref) = refs[:14]
    rest = refs[14:]
    if use_rope:
        ca_ref, s1a_ref, s2a_ref, cb_ref, s1b_ref, s2b_ref = rest[:6]
        rest = rest[6:]
    aq_o, ak_o, avt_o, bq_o, bk_o, bvt_o, c_o, g_o = rest

    x = x_ref[...]
    ms = jnp.mean(x * x, axis=-1, keepdims=True)
    h = x * lax.rsqrt(ms + EPS) * gpre_ref[...]
    h = h * (1.0 + sc_ref[...]) + sh_ref[...]
    p = _dot(h.astype(BF16), win_ref[...])
    bo = bo_ref[...]

    def rope_a(v):
        if not use_rope:
            return v
        return _rope128(v, ca_ref[...], s1a_ref[...], s2a_ref[...], HEAD_DIM // 2)

    def rope_b(v):
        if not use_rope:
            return v
        return _rope128(v, cb_ref[...], s1b_ref[...], s2b_ref[...], B_ROPE // 2)

    aqn = aqn_ref[...]
    for half in range(2):
        v = _head_rms(p[:, OFF_AQ + 256 * half:OFF_AQ + 256 * (half + 1)], bo, aqn)
        for s in range(2):
            blk = rope_a(v[:, LANE * s:LANE * (s + 1)]) * (HEAD_DIM ** -0.5 * LOG2_E)
            lo = 256 * half + LANE * s
            aq_o[lo:lo + LANE, :] = blk.T.astype(BF16)
    v = _head_rms(p[:, OFF_AK:OFF_AK + 256], bo, akn_ref[...])
    for s in range(2):
        ak_o[:, LANE * s:LANE * (s + 1)] = rope_a(v[:, LANE * s:LANE * (s + 1)]).astype(BF16)
    avt_o[...] = p[:, OFF_AV:OFF_AV + LANE].T.astype(BF16)

    bqd = p[:, OFF_BQD:OFF_BQD + 256]
    ms = jnp.sum(bqd * bqd, axis=-1, keepdims=True) * (1.0 / B_Q_RANK)
    qn = (bqd * lax.rsqrt(ms + EPS) * bqn_ref[...]).astype(BF16)
    bq = _dot(qn, wq_ref[...])
    bkvd = p[:, OFF_BKVD:OFF_BKVD + LANE]
    ms = jnp.mean(bkvd * bkvd, axis=-1, keepdims=True)
    kvn = (bkvd * lax.rsqrt(ms + EPS) * bkvn_ref[...]).astype(BF16)
    bkn = _dot(kvn, wkk_ref[...])
    bv = _dot(kvn, wkv_ref[...])
    kpe = rope_b(p[:, OFF_KPE:OFF_KPE + LANE])
    b_scale = (B_NOPE + B_ROPE) ** -0.5 * LOG2_E
    for hh in range(B_HEADS):
        sl = slice(LANE * hh, LANE * (hh + 1))
        bq_o[sl, :] = (rope_b(bq[:, sl]) * b_scale).T.astype(BF16)
        bk_o[:, sl] = (bkn[:, sl] + kpe).astype(BF16)
    bvt_o[...] = bv.T.astype(BF16)

    c_o[:, C_W * C_Q:C_W * (C_Q + 1)] = _silu(p[:, OFF_CQ:OFF_CQ + C_W])
    for d, (off, ck) in enumerate(((OFF_CFF, C_KF), (OFF_CFB, C_KB))):
        lb = lb_ref[d:d + 1, :]
        sp, sn = _sigmoid_pair(p[:, off:off + C_W])
        f = lb + (1.0 - lb) * sp
        g2 = jnp.log2(jnp.maximum(f, F_TINY))
        hi = g2.astype(BF16)
        g_o[:, C_W * 2 * d:C_W * (2 * d + 1)] = hi
        g_o[:, C_W * (2 * d + 1):C_W * (2 * d + 2)] = (g2 - hi.astype(F32)).astype(BF16)
        c_o[:, C_W * ck:C_W * (ck + 1)] = (1.0 - lb) * sn
    c_o[:, C_W * C_V:C_W * (C_V + 1)] = p[:, OFF_CI:OFF_CI + C_W]
    c_o[:, C_W * C_GATE:C_W * (C_GATE + 1)] = p[:, OFF_CG:OFF_CG + C_W]


def _features(xs, mod, w, layer, mod_row, rope):
    bsz, t, _ = xs.shape
    tm = min(512, t)
    nt = t // tm
    nb_rows = w["n_mod_rows"]

    def mod_spec(j):
        return pl.BlockSpec((None, 1, D_MODEL),
                            lambda b, i: ((layer * nb_rows + mod_row(b)) * 6 + j, 0, 0))

    in_specs = [
        pl.BlockSpec((None, tm, D_MODEL), lambda b, i: (b, i, 0)),
        mod_spec(0), mod_spec(1),
        _const_spec((None, 1, D_MODEL), (layer, 0, 0)),
        _const_spec((None, D_MODEL, N_COL), (layer, 0, 0)),
        _const_spec((None, 256, 512), (layer, 0, 0)),
        _const_spec((None, LANE, 512), (layer, 0, 0)),
        _const_spec((None, LANE, 256), (layer, 0, 0)),
        _const_spec((None, 1, 256), (layer, 0, 0)),
        _const_spec((None, 1, 256), (layer, 0, 0)),
        _const_spec((None, 1, 256), (layer, 0, 0)),
        _const_spec((None, 1, LANE), (layer, 0, 0)),
        _const_spec((256, 256), (0, 0)),
        _const_spec((None, 2, C_W), (layer, 0, 0)),
    ]
    args = [xs, mod, mod, w["g_pre_mix"], w["w_in"], w["w_q_up"], w["w_kv_k"], w["w_kv_v"],
            w["a_q_norm"], w["a_k_norm"], w["b_q_norm"], w["b_kv_norm"], w["block_ones"], w["lower"]]
    if rope is not None:
        in_specs += [pl.BlockSpec((tm, LANE), lambda b, i: (i, 0))] * 6
        args += list(rope)
    out_shape = [
        jax.ShapeDtypeStruct((bsz, 512, t), BF16),
        jax.ShapeDtypeStruct((bsz, t, 256), BF16),
        jax.ShapeDtypeStruct((bsz, nt, LANE, tm), BF16),
        jax.ShapeDtypeStruct((bsz, 512, t), BF16),
        jax.ShapeDtypeStruct((bsz, t, 512), BF16),
        jax.ShapeDtypeStruct((bsz, nt, 256, tm), BF16),
        jax.ShapeDtypeStruct((bsz, t, N_SLAB * C_W), F32),
        jax.ShapeDtypeStruct((bsz, t, 4 * C_W), BF16),
    ]
    out_specs = [
        pl.BlockSpec((None, 512, tm), lambda b, i: (b, 0, i)),
        pl.BlockSpec((None, tm, 256), lambda b, i: (b, i, 0)),
        pl.BlockSpec((None, None, LANE, tm), lambda b, i: (b, i, 0, 0)),
        pl.BlockSpec((None, 512, tm), lambda b, i: (b, 0, i)),
        pl.BlockSpec((None, tm, 512), lambda b, i: (b, i, 0)),
        pl.BlockSpec((None, None, 256, tm), lambda b, i: (b, i, 0, 0)),
        pl.BlockSpec((None, tm, N_SLAB * C_W), lambda b, i: (b, i, 0)),
        pl.BlockSpec((None, tm, 4 * C_W), lambda b, i: (b, i, 0)),
    ]
    outs = pl.pallas_call(
        functools.partial(_feat_kernel, use_rope=rope is not None),
        out_shape=out_shape, grid=(bsz, nt), in_specs=in_specs, out_specs=out_specs,
        compiler_params=_params(("parallel", "parallel")),
        name="feat_rope" if rope is not None else "feat_ctx",
    )(*args)
    return dict(zip(("aq", "ak", "avt", "bq", "bk", "bvt", "c", "g"), outs))


ACC_ROWS = B_V + 16
NEG_BIG = -1e30
LOG2_E = 1.4426950408889634


def _attn_kernel(*refs, n_src, n_heads, shared_kv):
    q_ref = refs[0]
    src_refs = refs[1:1 + 2 * n_src]
    o_ref = refs[1 + 2 * n_src]
    rhs_scr, m_scr, acc_scr, st_scr, mx_scr = refs[2 + 2 * n_src:]
    tq = q_ref.shape[1]
    n_grp, k_dim, grp_w = rhs_scr.shape
    heads_per_grp = n_heads // n_grp

    if shared_kv:
        for h in range(n_heads):
            rhs_scr[0, :, tq * h:tq * (h + 1)] = q_ref[HEAD_DIM * h:HEAD_DIM * (h + 1), :]
    else:
        rhs_scr[...] = jnp.zeros(rhs_scr.shape, BF16)
        for h in range(n_heads):
            g, j = divmod(h, heads_per_grp)
            rhs_scr[g, LANE * j:LANE * (j + 1), tq * j:tq * (j + 1)] = q_ref[LANE * h:LANE * (h + 1), :]
    m_scr[...] = jnp.full(m_scr.shape, NEG_BIG, F32)
    acc_scr[...] = jnp.zeros(acc_scr.shape, F32)

    def stage(s, c, slot):
        k_ref, vt_ref = src_refs[2 * s], src_refs[2 * s + 1]
        tk = vt_ref.shape[2]
        rows = pl.ds(pl.multiple_of(c * tk, tk), tk)
        for g in range(n_grp):
            cols = slice(grp_w * g, grp_w * (g + 1))
            kc = k_ref[rows, 0:k_dim] if shared_kv else k_ref[rows, k_dim * g:k_dim * (g + 1)]
            st = _dot(kc, rhs_scr[g])
            st_scr[slot, 0:tk, cols] = st
            mx_scr[slot, :, cols] = jnp.max(st, axis=0, keepdims=True)

    def consume(s, c, slot):
        vt_ref = src_refs[2 * s + 1]
        tk = vt_ref.shape[2]
        ones = jnp.ones((ACC_ROWS - B_V, tk), BF16)
        for g in range(n_grp):
            cols = slice(grp_w * g, grp_w * (g + 1))
            m_old = m_scr[:, cols]
            m_new = jnp.maximum(m_old, mx_scr[slot, :, cols])
            pt = jnp.exp2(st_scr[slot, 0:tk, cols] - m_new).astype(BF16)
            alpha = jnp.exp2(m_old - m_new)
            m_scr[:, cols] = m_new
            if shared_kv:
                vt = jnp.concatenate([vt_ref[c], ones], axis=0)
                acc_scr[:, cols] = alpha * acc_scr[:, cols] + _dot(vt, pt)
            else:
                for j in range(heads_per_grp):
                    h = g * heads_per_grp + j
                    hc = slice(tq * h, tq * (h + 1))
                    lc = slice(tq * j, tq * (j + 1))
                    vt = jnp.concatenate([vt_ref[c, B_V * h:B_V * (h + 1), :], ones], axis=0)
                    acc_scr[:, hc] = alpha[:, lc] * acc_scr[:, hc] + _dot(vt, pt[:, lc])

    n0 = src_refs[1].shape[0]
    n_loop = (n0 - 2) // 2 if n0 >= 4 else 0
    stage(0, 0, 0)
    if n_loop:
        def pair(i, carry):
            stage(0, 2 * i + 1, 1)
            consume(0, 2 * i, 0)
            stage(0, 2 * i + 2, 0)
            consume(0, 2 * i + 1, 1)
            return carry
        lax.fori_loop(0, n_loop, pair, 0)
    tail = [(0, c) for c in range(2 * n_loop, n0)]
    tail += [(s, c) for s in range(1, n_src) for c in range(src_refs[2 * s + 1].shape[0])]
    for i, (s, c) in enumerate(tail):
        if i + 1 < len(tail):
            stage(*tail[i + 1], (i + 1) % 2)
        consume(s, c, i % 2)

    for h in range(n_heads):
        acc = acc_scr[:, tq * h:tq * (h + 1)]
        o_ref[B_V * h:B_V * (h + 1), :] = (acc[:B_V] * (1.0 / acc[B_V:B_V + 1])).astype(BF16)


def _attention(q, srcs, *, n_kv, n_heads, shared_kv, name):
    bsz, _, t = q.shape
    tq = min(256, t)
    q_r = (HEAD_DIM if shared_kv else LANE) * n_heads
    k_w = LANE * (1 if shared_kv else n_heads)
    v_r = B_V * (1 if shared_kv else n_heads)
    rhs_shape = (1, HEAD_DIM, n_heads * tq) if shared_kv else (n_heads // 2, 2 * LANE, 2 * tq)
    in_specs = [pl.BlockSpec((None, q_r, tq), lambda b, g, i: (b, g, i))]
    args = [q]
    for k, vt in srcs:
        length = k.shape[1]
        n_chunk, tk = vt.shape[1], vt.shape[3]
        in_specs.append(pl.BlockSpec((None, length, k_w), lambda b, g, i: (b, 0, g)))
        in_specs.append(pl.BlockSpec((None, n_chunk, v_r, tk), lambda b, g, i: (b, 0, g, 0)))
        args += [k, vt]
    rows = B_V * n_heads
    return pl.pallas_call(
        functools.partial(_attn_kernel, n_src=len(srcs), n_heads=n_heads, shared_kv=shared_kv),
        out_shape=jax.ShapeDtypeStruct((bsz, rows * n_kv, t), BF16),
        grid=(bsz, n_kv, t // tq),
        in_specs=in_specs,
        out_specs=pl.BlockSpec((None, rows, tq), lambda b, g, i: (b, g, i)),
        scratch_shapes=[pltpu.VMEM(rhs_shape, BF16),
                        pltpu.VMEM((1, n_heads * tq), F32),
                        pltpu.VMEM((ACC_ROWS, n_heads * tq + LANE), F32),
                        pltpu.VMEM((2, max(vt.shape[3] for _, vt in srcs), n_heads * tq + LANE), F32),
                        pltpu.VMEM((2, 1, n_heads * tq), F32)],
        compiler_params=_params(("parallel", "parallel", "parallel")),
        name=name,
    )(*args)


def _scan_constants(reverse):
    c = SCAN_C
    t = np.arange(c)[:, None]
    s = np.arange(c)[None, :]
    cum = (s >= t) if reverse else (s <= t)
    msk = np.zeros((N_LEVELS + 1, c, c), np.float32)
    for lvl in range(N_LEVELS):
        w = (c // 2) >> lvl
        same = (t // (2 * w)) == (s // (2 * w))
        t_hi = (t % (2 * w)) >= w
        s_hi = (s % (2 * w)) >= w
        msk[lvl] = (same & ~t_hi & s_hi) if reverse else (same & t_hi & ~s_hi)
    msk[N_LEVELS] = (t == s)
    return jnp.asarray(cum, BF16), jnp.asarray(np.tile(msk, (1, 1, C_HEADS)), F32)


def _boundary_rows(b, lvl, reverse):
    w = (SCAN_C // 2) >> lvl
    width = b.shape[1]
    off = w if reverse else w - 1
    if w == 1:
        odd = lax.broadcasted_iota(jnp.int32, b.shape, 0) % 2 == 1
        if reverse:
            return jnp.where(odd, b, pltpu.roll(b, SCAN_C - 1, 0))
        return jnp.where(odd, pltpu.roll(b, 1, 0), b)
    if w == 2:
        low = lax.broadcasted_iota(jnp.int32, (8, width), 0) < 4
        pieces = []
        for r0 in range(0, SCAN_C, 8):
            first = jnp.broadcast_to(b[r0 + off:r0 + off + 1, :], (8, width))
            second = jnp.broadcast_to(b[r0 + 4 + off:r0 + 5 + off, :], (8, width))
            pieces.append(jnp.where(low, first, second))
        return jnp.concatenate(pieces, axis=0)
    pieces = [jnp.broadcast_to(b[r0 + off:r0 + off + 1, :], (2 * w, width))
              for r0 in range(0, SCAN_C, 2 * w)]
    return pieces[0] if len(pieces) == 1 else jnp.concatenate(pieces, axis=0)


def _hgrn_kernel(q_ref, k_ref, ghi_ref, glo_ref, v_ref, s0_ref, cum_ref, msk_ref, bm_ref,
                 o_ref, sfin_ref, st_ref, *, reverse, n_chunk):
    i = pl.program_id(1)

    @pl.when(i == 0)
    def _():
        st_ref[...] = s0_ref[...]

    bm = bm_ref[...]
    bm16 = bm.astype(BF16)
    cum = cum_ref[...]

    def stack_heads(a):
        a16 = a.astype(BF16)
        return jnp.concatenate([a16] * C_HEADS, axis=0) * bm16

    grp = min(n_chunk, SCAN_GROUP)
    n_grp = n_chunk // grp

    def group(jg, carry):
        jgg = (n_grp - 1 - jg) if reverse else jg
        base = jgg * (grp * SCAN_C)
        order = list(range(grp - 1, -1, -1)) if reverse else list(range(grp))
        rows = [pl.ds(pl.multiple_of(base + c * SCAN_C, SCAN_C), SCAN_C) for c in range(grp)]
        q = [q_ref[r, :] for r in rows]
        k = [k_ref[r, :] for r in rows]
        v = [v_ref[r, :] for r in rows]
        b = [_dot(cum, ghi_ref[r, :]) + _dot(cum, glo_ref[r, :]) for r in rows]
        b_all = [bc[0:1, :] if reverse else bc[SCAN_C - 1:SCAN_C, :] for bc in b]

        q16 = [qc.astype(BF16) for qc in q]
        kst = [stack_heads(kc) for kc in k]
        sc = [msk_ref[N_LEVELS] * _dot_nt(q16[c], kst[c]) for c in range(grp)]
        for lvl in range(N_LEVELS):
            for c in range(grp):
                d = b[c] - _boundary_rows(b[c], lvl, reverse)
                e16 = jnp.exp2(-jnp.abs(d)).astype(BF16)
                kl = kst[c] * jnp.concatenate([e16] * C_HEADS, axis=0)
                sc[c] = sc[c] + msk_ref[lvl] * _dot_nt(q16[c] * e16, kl)

        ds = [_dot_tn(v[c].astype(BF16), (k[c] * jnp.exp2(b_all[c] - b[c])).astype(BF16)) * bm
              for c in range(grp)]
        st = st_ref[...]
        st_in = [None] * grp
        for c in order:
            st_in[c] = st
            st = st * jnp.exp2(b_all[c]) + ds[c]
        st_ref[...] = st

        for c in order:
            qdec = q[c] * jnp.exp2(b[c])
            o_ref[rows[c], :] = (_dot(sc[c].astype(BF16), stack_heads(v[c]))
                                 + _dot_nt(qdec.astype(BF16), st_in[c].astype(BF16)))
        return carry

    if n_grp == 1:
        group(0, 0)
    else:
        lax.fori_loop(0, n_grp, group, 0)

    @pl.when(i == pl.num_programs(1) - 1)
    def _():
        sfin_ref[...] = st_ref[...]


def _hgrn(c_slab, g16, s0, consts, block_mask, reverse):
    bsz, t, _ = c_slab.shape
    tb = min(512, t)
    nblk = t // tb
    cum, msk = consts

    def blk(i):
        return (nblk - 1 - i) if reverse else i

    def slab_spec(j):
        return pl.BlockSpec((None, tb, C_W), lambda b, i: (b, blk(i), j))

    g0 = 2 if reverse else 0

    return pl.pallas_call(
        functools.partial(_hgrn_kernel, reverse=reverse, n_chunk=tb // SCAN_C),
        out_shape=[jax.ShapeDtypeStruct((bsz, t, C_W), F32),
                   jax.ShapeDtypeStruct((bsz, C_W, C_W), F32)],
        grid=(bsz, nblk),
        in_specs=[
            slab_spec(C_Q), slab_spec(C_KB if reverse else C_KF), slab_spec(g0), slab_spec(g0 + 1),
            slab_spec(C_V),
            pl.BlockSpec((None, C_W, C_W), lambda b, i: (b, 0, 0)),
            _const_spec(cum.shape, (0, 0)),
            _const_spec(msk.shape, (0, 0, 0)),
            _const_spec(block_mask.shape, (0, 0)),
        ],
        out_specs=[pl.BlockSpec((None, tb, C_W), lambda b, i: (b, blk(i), 0)),
                   pl.BlockSpec((None, C_W, C_W), lambda b, i: (b, 0, 0))],
        scratch_shapes=[pltpu.VMEM((C_W, C_W), F32)],
        compiler_params=_params(("parallel", "arbitrary")),
        name="hgrn_bwd" if reverse else "hgrn_fwd",
    )(c_slab, c_slab, g16, g16, c_slab, s0, cum, msk, block_mask)


def _rms(v, gain):
    ms = jnp.mean(v * v, axis=-1, keepdims=True)
    return v * lax.rsqrt(ms + EPS) * gain


def _post_kernel(x_ref, ya_ref, yb_ref, of_ref, ob_ref, gate_ref,
                 gtm_ref, shf_ref, scf_ref, gtf_ref,
                 gpm_ref, gpf_ref, gqf_ref, con_ref, bo_ref,
                 woa_ref, wob_ref, woc_ref, w1_ref, w2_ref, o_ref):
    o = of_ref[...] + ob_ref[...]
    yc = (_head_rms(o, bo_ref[...], con_ref[...]) * _silu(gate_ref[...])).astype(BF16)
    mix = (_dot_tn(ya_ref[...], woa_ref[...]) + _dot_tn(yb_ref[...], wob_ref[...])
           + _dot(yc, woc_ref[...]))
    x1 = x_ref[...] + gtm_ref[...] * _rms(mix, gpm_ref[...])
    h = _rms(x1, gpf_ref[...]) * (1.0 + scf_ref[...]) + shf_ref[...]
    u = jnp.maximum(_dot(h.astype(BF16), w1_ref[...]), 0.0)
    ff = _dot((u * u).astype(BF16), w2_ref[...])
    o_ref[...] = x1 + gtf_ref[...] * _rms(ff, gqf_ref[...])


def _post(xs, yta, ytb, o_f, o_b, c_slab, mod, w, layer, mod_row):
    bsz, t, _ = xs.shape
    tm = min(512, t)
    nb_rows = w["n_mod_rows"]

    def mod_spec(j):
        return pl.BlockSpec((None, 1, D_MODEL),
                            lambda b, i: ((layer * nb_rows + mod_row(b)) * 6 + j, 0, 0))

    def vec_spec():
        return _const_spec((None, 1, D_MODEL), (layer, 0, 0))

    in_specs = [
        pl.BlockSpec((None, tm, D_MODEL), lambda b, i: (b, i, 0)),
        pl.BlockSpec((None, A_OUT, tm), lambda b, i: (b, 0, i)),
        pl.BlockSpec((None, B_OUT, tm), lambda b, i: (b, 0, i)),
        pl.BlockSpec((None, tm, C_W), lambda b, i: (b, i, 0)),
        pl.BlockSpec((None, tm, C_W), lambda b, i: (b, i, 0)),
        pl.BlockSpec((None, tm, C_W), lambda b, i: (b, i, C_GATE)),
        mod_spec(2), mod_spec(3), mod_spec(4), mod_spec(5),
        vec_spec(), vec_spec(), vec_spec(),
        _const_spec((None, 1, C_W), (layer, 0, 0)),
        _const_spec((256, 256), (0, 0)),
        _const_spec((None, A_OUT, D_MODEL), (layer, 0, 0)),
        _const_spec((None, B_OUT, D_MODEL), (layer, 0, 0)),
        _const_spec((None, C_OUT, D_MODEL), (layer, 0, 0)),
        _const_spec((None, D_MODEL, D_FF), (layer, 0, 0)),
        _const_spec((None, D_FF, D_MODEL), (layer, 0, 0)),
    ]
    return pl.pallas_call(
        _post_kernel,
        out_shape=jax.ShapeDtypeStruct((bsz, t, D_MODEL), F32),
        grid=(bsz, t // tm),
        in_specs=in_specs,
        out_specs=pl.BlockSpec((None, tm, D_MODEL), lambda b, i: (b, i, 0)),
        compiler_params=_params(("parallel", "parallel")),
        name="post",
    )(xs, yta, ytb, o_f, o_b, c_slab, mod, mod, mod, mod,
      w["g_post_mix"], w["g_pre_ffn"], w["g_post_ffn"], w["c_out_norm"], w["block_ones"],
      w["w_out_a"], w["w_out_b"], w["w_out_c"], w["w_ff1"], w["w_ff2"])


def _w_in_columns():
    src = np.full((N_COL,), -1, np.int64)
    a_k0 = A_HEADS * HEAD_DIM
    a_v0 = a_k0 + A_KV_HEADS * HEAD_DIM
    b_qd0 = a_v0 + A_KV_HEADS * HEAD_DIM
    b_kv0 = b_qd0 + B_Q_RANK
    b_kr0 = b_kv0 + B_KV_RANK
    c0 = b_kr0 + B_ROPE
    src[OFF_AQ:OFF_AQ + 512] = np.arange(512)
    for g in range(A_KV_HEADS):
        for rep in range(2):
            lo = OFF_AK + (2 * g + rep) * HEAD_DIM
            src[lo:lo + HEAD_DIM] = a_k0 + g * HEAD_DIM + np.arange(HEAD_DIM)
    src[OFF_AV:OFF_AV + 128] = a_v0 + np.arange(128)
    src[OFF_BQD:OFF_BQD + B_Q_RANK] = b_qd0 + np.arange(B_Q_RANK)
    src[OFF_BKVD:OFF_BKVD + B_KV_RANK] = b_kv0 + np.arange(B_KV_RANK)
    src[OFF_KPE + B_NOPE:OFF_KPE + B_NOPE + B_ROPE] = b_kr0 + np.arange(B_ROPE)
    src[OFF_CQ:OFF_CQ + 5 * C_W] = c0 + np.arange(5 * C_W)
    return src


def _gather_cols(w, src):
    keep = jnp.asarray(src >= 0)
    return jnp.where(keep, jnp.take(w, jnp.asarray(np.maximum(src, 0)), axis=-1), 0.0)


def _prepare_weights(p, n_mod_rows):
    depth = p["w_in"].shape[0]
    w = {"n_mod_rows": n_mod_rows}
    w["w_in"] = _gather_cols(p["w_in"], _w_in_columns()).astype(BF16)
    src = np.full((B_HEADS * LANE,), -1, np.int64)
    for hh in range(B_HEADS):
        src[hh * LANE:hh * LANE + B_NOPE + B_ROPE] = hh * (B_NOPE + B_ROPE) + np.arange(B_NOPE + B_ROPE)
    wq = _gather_cols(p["w_q_up"], src)
    w["w_q_up"] = jnp.pad(wq, ((0, 0), (0, 256 - B_Q_RANK), (0, 0))).astype(BF16)
    src = np.full((B_HEADS * LANE,), -1, np.int64)
    srcv = np.zeros((B_HEADS * B_V,), np.int64)
    for hh in range(B_HEADS):
        src[hh * LANE:hh * LANE + B_NOPE] = hh * (B_NOPE + B_V) + np.arange(B_NOPE)
        srcv[hh * B_V:(hh + 1) * B_V] = hh * (B_NOPE + B_V) + B_NOPE + np.arange(B_V)
    w["w_kv_k"] = _gather_cols(p["w_kv_up"], src).astype(BF16)
    w["w_kv_v"] = _gather_cols(p["w_kv_up"], srcv).astype(BF16)
    w["a_q_norm"] = jnp.tile(p["a_q_norm"], (1, 4))[:, None, :]
    w["a_k_norm"] = jnp.tile(p["a_k_norm"], (1, 4))[:, None, :]
    w["b_q_norm"] = jnp.pad(p["b_q_norm"], ((0, 0), (0, 256 - B_Q_RANK)))[:, None, :]
    w["b_kv_norm"] = p["b_kv_norm"][:, None, :]
    w["c_out_norm"] = jnp.tile(p["c_out_norm"], (1, C_HEADS))[:, None, :]
    for name in ("g_pre_mix", "g_post_mix", "g_pre_ffn", "g_post_ffn"):
        w[name] = p[name][:, None, :]
    head = np.arange(256) // HEAD_DIM
    same_head = head[:, None] == head[None, :]
    w["block_ones"] = jnp.asarray(same_head / float(HEAD_DIM), BF16)
    w["block_mask"] = jnp.asarray(same_head, F32)
    p_lb = jax.nn.softmax(p["c_lower_bounds"].astype(F32), axis=0)
    w["lower"] = jnp.cumsum(p_lb, axis=0) - p_lb[:1]
    w["w_out_a"] = p["w_out"][:, :A_OUT].astype(BF16)
    w["w_out_b"] = p["w_out"][:, A_OUT:A_OUT + B_OUT].astype(BF16)
    w["w_out_c"] = p["w_out"][:, A_OUT + B_OUT:].astype(BF16)
    w["w_ff1"] = p["w_ff1"].astype(BF16)
    w["w_ff2"] = p["w_ff2"].astype(BF16)
    del depth
    return w


def _rope_tables(n_tok):
    tok = np.arange(n_tok)
    row = (tok // GRID_W).astype(np.float32)[:, None]
    col = (tok % GRID_W).astype(np.float32)[:, None]

    def angles(rot_dim):
        n_freq = rot_dim // 4
        inv = jnp.asarray(ROPE_THETA, F32) ** (-jnp.arange(n_freq, dtype=F32) / n_freq)
        ang = jnp.concatenate([jnp.asarray(row) * inv, jnp.asarray(col) * inv], axis=-1)
        return jnp.cos(ang), jnp.sin(ang)

    zeros = lambda n: jnp.zeros((n_tok, n), F32)
    ones = lambda n: jnp.ones((n_tok, n), F32)
    cos, sin = angles(HEAD_DIM)
    ca = jnp.concatenate([cos, cos, cos, cos], axis=-1)
    s1a = jnp.concatenate([-sin, zeros(32), -sin, zeros(32)], axis=-1)
    s2a = jnp.concatenate([zeros(32), sin, zeros(32), sin], axis=-1)
    cos, sin = angles(B_ROPE)
    cb = jnp.concatenate([ones(64), cos, cos, ones(32)], axis=-1)
    s1b = jnp.concatenate([zeros(64), -sin, zeros(16), zeros(32)], axis=-1)
    s2b = jnp.concatenate([zeros(64), zeros(16), sin, zeros(32)], axis=-1)
    return ca, s1a, s2a, cb, s1b, s2b


def kernel(x, c, ctx, c_ctx, w_ada, b_ada, g_pre_mix, g_post_mix, g_pre_ffn, g_post_ffn, w_in, a_q_norm, a_k_norm, b_q_norm, w_q_up, b_kv_norm, w_kv_up, c_lower_bounds, c_out_norm, w_out, w_ff1, w_ff2):
    bsz, n_lat, _ = x.shape
    depth = w_in.shape[0]
    n_mod_rows = -(-(bsz + 1) // 8) * 8
    params = dict(w_in=w_in, a_q_norm=a_q_norm, a_k_norm=a_k_norm, b_q_norm=b_q_norm, w_q_up=w_q_up,
                  b_kv_norm=b_kv_norm, w_kv_up=w_kv_up, c_lower_bounds=c_lower_bounds,
                  c_out_norm=c_out_norm, w_out=w_out, w_ff1=w_ff1, w_ff2=w_ff2,
                  g_pre_mix=g_pre_mix, g_post_mix=g_post_mix, g_pre_ffn=g_pre_ffn, g_post_ffn=g_post_ffn)
    w = _prepare_weights(params, n_mod_rows)
    rope = _rope_tables(n_lat)
    scan_f = _scan_constants(False)
    scan_b = _scan_constants(True)

    cvec = jnp.concatenate([c, c_ctx[None, :], jnp.zeros((n_mod_rows - bsz - 1, D_MODEL), F32)], axis=0)
    mod = _ada(cvec, w_ada, b_ada).reshape(depth * n_mod_rows * 6, 1, D_MODEL)

    lat_row = lambda b: b
    ctx_row = lambda b: bsz
    zero_state = jnp.zeros((bsz, C_W, C_W), F32)
    attn_a = functools.partial(_attention, n_kv=A_KV_HEADS, n_heads=A_HEADS // A_KV_HEADS, shared_kv=True)
    attn_b = functools.partial(_attention, n_kv=1, n_heads=B_HEADS, shared_kv=False)

    xc = ctx
    for layer in range(depth):
        need_ctx = layer < depth - 1
        fl = _features(x, mod, w, layer, lat_row, rope)
        fc = _features(xc, mod, w, layer, ctx_row, None)
        yta = attn_a(fl["aq"], [(fl["ak"], fl["avt"]), (fc["ak"], fc["avt"])], name="attn_a")
        ytb = attn_b(fl["bq"], [(fl["bk"], fl["bvt"]), (fc["bk"], fc["bvt"])], name="attn_b")
        ocf, s_f = _hgrn(fc["c"], fc["g"], zero_state, scan_f, w["block_mask"], False)
        ocb, s_b = _hgrn(fc["c"], fc["g"], zero_state, scan_b, w["block_mask"], True)
        olf, _ = _hgrn(fl["c"], fl["g"], s_f, scan_f, w["block_mask"], False)
        olb, _ = _hgrn(fl["c"], fl["g"], s_b, scan_b, w["block_mask"], True)
        x_new = _post(x, yta, ytb, olf, olb, fl["c"], mod, w, layer, lat_row)
        if need_ctx:
            yta_c = attn_a(fc["aq"], [(fc["ak"], fc["avt"])], name="attn_a_ctx")
            ytb_c = attn_b(fc["bq"], [(fc["bk"], fc["bvt"])], name="attn_b_ctx")
            xc = _post(xc, yta_c, ytb_c, ocf, ocb, fc["c"], mod, w, layer, ctx_row)
        x = x_new
    return x
```

```python
import functools

import numpy as np
import jax
import jax.numpy as jnp
from jax import lax
from jax.experimental import pallas as pl
from jax.experimental.pallas import tpu as pltpu

F32 = jnp.float32
BF16 = jnp.bfloat16

D_MODEL = 1024
GRID_W = 64
HEAD_DIM = 64
A_HEADS = 8
A_KV_HEADS = 2
B_HEADS = 4
B_Q_RANK = 192
B_KV_RANK = 128
B_NOPE = 64
B_ROPE = 32
B_V = 64
C_HEADS = 4
C_DK = 64
C_DV = 64
D_FF = 4 * D_MODEL
A_OUT = A_HEADS * HEAD_DIM
B_OUT = B_HEADS * B_V
C_OUT = C_HEADS * C_DV
C_W = C_HEADS * C_DK
ROPE_THETA = 10000.0
EPS = 1e-6
F_TINY = 1e-30

LANE = 128
VMEM_LIMIT = 56 * 1024 * 1024

OFF_AQ = 0
OFF_AK = 512
OFF_AV = 768
OFF_BQD = 896
OFF_BKVD = 1152
OFF_KPE = 1280
OFF_CQ = 1408
OFF_CFF = 1664
OFF_CFB = 1920
OFF_CI = 2176
OFF_CG = 2432
N_COL = 2688

C_Q, C_KF, C_KB, C_V, C_GATE = range(5)
N_SLAB = 5

SCAN_C = 64
N_LEVELS = 6
SCAN_GROUP = 8


def _dot(a, b):
    return jnp.dot(a, b, preferred_element_type=F32)


def _dot_nt(a, b):
    return lax.dot_general(a, b, (((1,), (1,)), ((), ())), preferred_element_type=F32)


def _dot_tn(a, b):
    return lax.dot_general(a, b, (((0,), (0,)), ((), ())), preferred_element_type=F32)


def _sigmoid_pair(z):
    e = jnp.exp(-jnp.abs(z))
    inv = 1.0 / (1.0 + e)
    small = e * inv
    pos = z >= 0
    return jnp.where(pos, inv, small), jnp.where(pos, small, inv)


def _silu(z):
    s, _ = _sigmoid_pair(z)
    return z * s


def _const_spec(shape, index):
    return pl.BlockSpec(shape, lambda *_: index, pipeline_mode=pl.Buffered(1))


def _params(sem):
    return pltpu.CompilerParams(dimension_semantics=sem, vmem_limit_bytes=VMEM_LIMIT)


def _ada_kernel(c_ref, w_ref, b_ref, o_ref):
    a = _silu(c_ref[...])
    w = w_ref[...]
    a_hi = a.astype(BF16)
    a_lo = (a - a_hi.astype(F32)).astype(BF16)
    w_hi = w.astype(BF16)
    w_lo = (w - w_hi.astype(F32)).astype(BF16)
    acc = _dot(a_hi, w_hi) + (_dot(a_hi, w_lo) + _dot(a_lo, w_hi))
    o_ref[...] = acc + b_ref[...]


def _ada(cvec, w_ada, b_ada):
    depth = w_ada.shape[0]
    rows = cvec.shape[0]
    n_blk = w_ada.shape[2] // D_MODEL
    return pl.pallas_call(
        _ada_kernel,
        out_shape=jax.ShapeDtypeStruct((depth, rows, n_blk * D_MODEL), F32),
        grid=(depth, n_blk),
        in_specs=[
            pl.BlockSpec((rows, D_MODEL), lambda l, j: (0, 0)),
            pl.BlockSpec((None, D_MODEL, D_MODEL), lambda l, j: (l, 0, j)),
            pl.BlockSpec((None, None, 1, D_MODEL), lambda l, j: (l, j, 0, 0)),
        ],
        out_specs=pl.BlockSpec((None, rows, D_MODEL), lambda l, j: (l, 0, j)),
        compiler_params=_params(("arbitrary", "arbitrary")),
        name="ada",
    )(cvec, w_ada, b_ada.reshape(depth, n_blk, 1, D_MODEL))


def _rope128(v, cos, s1, s2, half):
    up = pltpu.roll(v, LANE - half, 1)
    dn = pltpu.roll(v, half, 1)
    return v * cos + up * s1 + dn * s2


def _head_rms(v, bo, gain):
    sq = v * v
    hi = sq.astype(BF16)
    lo = (sq - hi.astype(F32)).astype(BF16)
    ms = _dot(hi, bo) + _dot(lo, bo)
    return v * lax.rsqrt(ms + EPS) * gain


FEAT_SUB = 256


def _feat_kernel(*refs, use_rope):
    (x_ref, sh_ref, sc_ref, gpre_ref, win_ref, wq_ref, wkk_ref, wkv_ref,
     aqn_ref, akn_ref, bqn_ref, bkvn_ref, bo_ref, lb_ref) = refs[:14]
    rest = refs[14:]
    if use_rope:
        ca_ref, s1a_ref, s2a_ref, cb_ref, s1b_ref, s2b_ref = rest[:6]
        rest = rest[6:]
    aq_o, ak_o, avt_o, bq_o, bk_o, bvt_o, c_o, g_o = rest
    bo = bo_ref[...]
    tm = x_ref.shape[0]
    sub = min(tm, FEAT_SUB)

    for r0 in range(0, tm, sub):
        rs = slice(r0, r0 + sub)

        def rope_a(v, rs=rs):
            if not use_rope:
                return v
            return _rope128(v, ca_ref[rs, :], s1a_ref[rs, :], s2a_ref[rs, :], HEAD_DIM // 2)

        def rope_b(v, rs=rs):
            if not use_rope:
                return v
            return _rope128(v, cb_ref[rs, :], s1b_ref[rs, :], s2b_ref[rs, :], B_ROPE // 2)

        x = x_ref[rs, :]
        ms = jnp.mean(x * x, axis=-1, keepdims=True)
        h = x * lax.rsqrt(ms + EPS) * gpre_ref[...]
        h = h * (1.0 + sc_ref[...]) + sh_ref[...]
        p = _dot(h.astype(BF16), win_ref[...])

        aqn = aqn_ref[...]
        for half in range(2):
            v = _head_rms(p[:, OFF_AQ + 256 * half:OFF_AQ + 256 * (half + 1)], bo, aqn)
            for s in range(2):
                blk = rope_a(v[:, LANE * s:LANE * (s + 1)]) * (HEAD_DIM ** -0.5 * LOG2_E)
                lo = 256 * half + LANE * s
                aq_o[lo:lo + LANE, rs] = blk.T.astype(BF16)
        v = _head_rms(p[:, OFF_AK:OFF_AK + 256], bo, akn_ref[...])
        for s in range(2):
            ak_o[rs, LANE * s:LANE * (s + 1)] = rope_a(v[:, LANE * s:LANE * (s + 1)]).astype(BF16)
        avt_o[:, rs] = p[:, OFF_AV:OFF_AV + LANE].T.astype(BF16)

        bqd = p[:, OFF_BQD:OFF_BQD + 256]
        ms = jnp.sum(bqd * bqd, axis=-1, keepdims=True) * (1.0 / B_Q_RANK)
        qn = (bqd * lax.rsqrt(ms + EPS) * bqn_ref[...]).astype(BF16)
        bq = _dot(qn, wq_ref[...])
        bkvd = p[:, OFF_BKVD:OFF_BKVD + LANE]
        ms = jnp.mean(bkvd * bkvd, axis=-1, keepdims=True)
        kvn = (bkvd * lax.rsqrt(ms + EPS) * bkvn_ref[...]).astype(BF16)
        bkn = _dot(kvn, wkk_ref[...])
        bv = _dot(kvn, wkv_ref[...])
        kpe = rope_b(p[:, OFF_KPE:OFF_KPE + LANE])
        b_scale = (B_NOPE + B_ROPE) ** -0.5 * LOG2_E
        for hh in range(B_HEADS):
            sl = slice(LANE * hh, LANE * (hh + 1))
            bq_o[sl, rs] = (rope_b(bq[:, sl]) * b_scale).T.astype(BF16)
            bk_o[rs, sl] = (bkn[:, sl] + kpe).astype(BF16)
        bvt_o[:, rs] = bv.T.astype(BF16)

        c_o[rs, C_W * C_Q:C_W * (C_Q + 1)] = _silu(p[:, OFF_CQ:OFF_CQ + C_W])
        for d, (off, ck) in enumerate(((OFF_CFF, C_KF), (OFF_CFB, C_KB))):
            lb = lb_ref[d:d + 1, :]
            sp, sn = _sigmoid_pair(p[:, off:off + C_W])
            f = lb + (1.0 - lb) * sp
            g2 = jnp.log2(jnp.maximum(f, F_TINY))
            hi = g2.astype(BF16)
            g_o[rs, C_W * 2 * d:C_W * (2 * d + 1)] = hi
            g_o[rs, C_W * (2 * d + 1):C_W * (2 * d + 2)] = (g2 - hi.astype(F32)).astype(BF16)
            c_o[rs, C_W * ck:C_W * (ck + 1)] = (1.0 - lb) * sn
        c_o[rs, C_W * C_V:C_W * (C_V + 1)] = p[:, OFF_CI:OFF_CI + C_W]
        c_o[rs, C_W * C_GATE:C_W * (C_GATE + 1)] = p[:, OFF_CG:OFF_CG + C_W]


def _features(xs, mod, w, layer, mod_row, rope):
    bsz, t, _ = xs.shape
    tm = min(512, t)
    nt = t // tm
    nb_rows = w["n_mod_rows"]

    def mod_spec(j):
        return pl.BlockSpec((None, 1, D_MODEL),
                            lambda b, i: ((layer * nb_rows + mod_row(b)) * 6 + j, 0, 0))

    in_specs = [
        pl.BlockSpec((None, tm, D_MODEL), lambda b, i: (b, i, 0)),
        mod_spec(0), mod_spec(1),
        _const_spec((None, 1, D_MODEL), (layer, 0, 0)),
        _const_spec((None, D_MODEL, N_COL), (layer, 0, 0)),
        _const_spec((None, 256, 512), (layer, 0, 0)),
        _const_spec((None, LANE, 512), (layer, 0, 0)),
        _const_spec((None, LANE, 256), (layer, 0, 0)),
        _const_spec((None, 1, 256), (layer, 0, 0)),
        _const_spec((None, 1, 256), (layer, 0, 0)),
        _const_spec((None, 1, 256), (layer, 0, 0)),
        _const_spec((None, 1, LANE), (layer, 0, 0)),
        _const_spec((256, 256), (0, 0)),
        _const_spec((None, 2, C_W), (layer, 0, 0)),
    ]
    args = [xs, mod, mod, w["g_pre_mix"], w["w_in"], w["w_q_up"], w["w_kv_k"], w["w_kv_v"],
            w["a_q_norm"], w["a_k_norm"], w["b_q_norm"], w["b_kv_norm"], w["block_ones"], w["lower"]]
    if rope is not None:
        in_specs += [pl.BlockSpec((tm, LANE), lambda b, i: (i, 0))] * 6
        args += list(rope)
    out_shape = [
        jax.ShapeDtypeStruct((bsz, 512, t), BF16),
        jax.ShapeDtypeStruct((bsz, t, 256), BF16),
        jax.ShapeDtypeStruct((bsz, nt, LANE, tm), BF16),
        jax.ShapeDtypeStruct((bsz, 512, t), BF16),
        jax.ShapeDtypeStruct((bsz, t, 512), BF16),
        jax.ShapeDtypeStruct((bsz, nt, 256, tm), BF16),
        jax.ShapeDtypeStruct((bsz, t, N_SLAB * C_W), F32),
        jax.ShapeDtypeStruct((bsz, t, 4 * C_W), BF16),
    ]
    out_specs = [
        pl.BlockSpec((None, 512, tm), lambda b, i: (b, 0, i)),
        pl.BlockSpec((None, tm, 256), lambda b, i: (b, i, 0)),
        pl.BlockSpec((None, None, LANE, tm), lambda b, i: (b, i, 0, 0)),
        pl.BlockSpec((None, 512, tm), lambda b, i: (b, 0, i)),
        pl.BlockSpec((None, tm, 512), lambda b, i: (b, i, 0)),
        pl.BlockSpec((None, None, 256, tm), lambda b, i: (b, i, 0, 0)),
        pl.BlockSpec((None, tm, N_SLAB * C_W), lambda b, i: (b, i, 0)),
        pl.BlockSpec((None, tm, 4 * C_W), lambda b, i: (b, i, 0)),
    ]
    outs = pl.pallas_call(
        functools.partial(_feat_kernel, use_rope=rope is not None),
        out_shape=out_shape, grid=(bsz, nt), in_specs=in_specs, out_specs=out_specs,
        compiler_params=_params(("parallel", "parallel")),
        name="feat_rope" if rope is not None else "feat_ctx",
    )(*args)
    return dict(zip(("aq", "ak", "avt", "bq", "bk", "bvt", "c", "g"), outs))


ACC_ROWS = B_V + 16
NEG_BIG = -1e30
LOG2_E = 1.4426950408889634
ATTN_TQ = 256
ATTN_TILES = 2


def _attn_kernel(*refs, n_src, n_heads, shared_kv):
    q_ref = refs[0]
    src_refs = refs[1:1 + 2 * n_src]
    o_ref = refs[1 + 2 * n_src]
    rhs_scr, m_scr, acc_scr, st_scr, mx_scr = refs[2 + 2 * n_src:]
    n_grp, k_dim, grp_w = rhs_scr.shape
    heads_per_grp = n_heads // n_grp
    tq = grp_w // heads_per_grp
    n_tiles = q_ref.shape[1] // tq

    def load_q(t):
        lanes = slice(tq * t, tq * (t + 1))
        for h in range(n_heads):
            if shared_kv:
                rhs_scr[0, :, tq * h:tq * (h + 1)] = q_ref[HEAD_DIM * h:HEAD_DIM * (h + 1), lanes]
            else:
                g, j = divmod(h, heads_per_grp)
                rhs_scr[g, LANE * j:LANE * (j + 1), tq * j:tq * (j + 1)] = q_ref[LANE * h:LANE * (h + 1), lanes]

    def reset():
        m_scr[...] = jnp.full(m_scr.shape, NEG_BIG, F32)
        acc_scr[...] = jnp.zeros(acc_scr.shape, F32)

    def finalize(t):
        for h in range(n_heads):
            acc = acc_scr[:, tq * h:tq * (h + 1)]
            o_ref[B_V * h:B_V * (h + 1), tq * t:tq * (t + 1)] = (
                acc[:B_V] * (1.0 / acc[B_V:B_V + 1])).astype(BF16)

    if not shared_kv:
        rhs_scr[...] = jnp.zeros(rhs_scr.shape, BF16)

    def stage(s, c, slot):
        k_ref, vt_ref = src_refs[2 * s], src_refs[2 * s + 1]
        tk = vt_ref.shape[2]
        rows = pl.ds(pl.multiple_of(c * tk, tk), tk)
        for g in range(n_grp):
            cols = slice(grp_w * g, grp_w * (g + 1))
            kc = k_ref[rows, 0:k_dim] if shared_kv else k_ref[rows, k_dim * g:k_dim * (g + 1)]
            st = _dot(kc, rhs_scr[g])
            st_scr[slot, 0:tk, cols] = st
            mx_scr[slot, :, cols] = jnp.max(st, axis=0, keepdims=True)

    def consume(s, c, slot):
        vt_ref = src_refs[2 * s + 1]
        tk = vt_ref.shape[2]
        ones = jnp.ones((ACC_ROWS - B_V, tk), BF16)
        for g in range(n_grp):
            cols = slice(grp_w * g, grp_w * (g + 1))
            m_old = m_scr[:, cols]
            m_new = jnp.maximum(m_old, mx_scr[slot, :, cols])
            pt = jnp.exp2(st_scr[slot, 0:tk, cols] - m_new).astype(BF16)
            alpha = jnp.exp2(m_old - m_new)
            m_scr[:, cols] = m_new
            if shared_kv:
                vt = jnp.concatenate([vt_ref[c], ones], axis=0)
                acc_scr[:, cols] = alpha * acc_scr[:, cols] + _dot(vt, pt)
            else:
                for j in range(heads_per_grp):
                    h = g * heads_per_grp + j
                    hc = slice(tq * h, tq * (h + 1))
                    lc = slice(tq * j, tq * (j + 1))
                    vt = jnp.concatenate([vt_ref[c, B_V * h:B_V * (h + 1), :], ones], axis=0)
                    acc_scr[:, hc] = alpha[:, lc] * acc_scr[:, hc] + _dot(vt, pt[:, lc])

    n0 = src_refs[1].shape[0]
    n_loop = (n0 - 2) // 2 if n0 >= 4 else 0
    slot = 0
    load_q(0)
    reset()
    stage(0, 0, slot)
    for t in range(n_tiles):
        if n_loop:
            def pair(i, carry, p=slot):
                stage(0, 2 * i + 1, 1 - p)
                consume(0, 2 * i, p)
                stage(0, 2 * i + 2, p)
                consume(0, 2 * i + 1, 1 - p)
                return carry
            lax.fori_loop(0, n_loop, pair, 0)
        tail = [(0, c) for c in range(2 * n_loop, n0)]
        tail += [(s, c) for s in range(1, n_src) for c in range(src_refs[2 * s + 1].shape[0])]
        for i, (s, c) in enumerate(tail):
            if i + 1 < len(tail):
                stage(*tail[i + 1], 1 - slot)
            elif t + 1 < n_tiles:
                load_q(t + 1)
                stage(0, 0, 1 - slot)
            consume(s, c, slot)
            slot = 1 - slot
        finalize(t)
        if t + 1 < n_tiles:
            reset()


def _attention(q, srcs, *, n_kv, n_heads, shared_kv, name):
    bsz, _, t = q.shape
    tq = min(ATTN_TQ, t)
    tstep = min(ATTN_TQ * ATTN_TILES, t)
    q_r = (HEAD_DIM if shared_kv else LANE) * n_heads
    k_w = LANE * (1 if shared_kv else n_heads)
    v_r = B_V * (1 if shared_kv else n_heads)
    rhs_shape = (1, HEAD_DIM, n_heads * tq) if shared_kv else (n_heads // 2, 2 * LANE, 2 * tq)
    in_specs = [pl.BlockSpec((None, q_r, tstep), lambda b, g, i: (b, g, i))]
    args = [q]
    for k, vt in srcs:
        length = k.shape[1]
        n_chunk, tk = vt.shape[1], vt.shape[3]
        in_specs.append(pl.BlockSpec((None, length, k_w), lambda b, g, i: (b, 0, g)))
        in_specs.append(pl.BlockSpec((None, n_chunk, v_r, tk), lambda b, g, i: (b, 0, g, 0)))
        args += [k, vt]
    rows = B_V * n_heads
    return pl.pallas_call(
        functools.partial(_attn_kernel, n_src=len(srcs), n_heads=n_heads, shared_kv=shared_kv),
        out_shape=jax.ShapeDtypeStruct((bsz, rows * n_kv, t), BF16),
        grid=(bsz, n_kv, t // tstep),
        in_specs=in_specs,
        out_specs=pl.BlockSpec((None, rows, tstep), lambda b, g, i: (b, g, i)),
        scratch_shapes=[pltpu.VMEM(rhs_shape, BF16),
                        pltpu.VMEM((1, n_heads * tq), F32),
                        pltpu.VMEM((ACC_ROWS, n_heads * tq), F32),
                        pltpu.VMEM((2, max(vt.shape[3] for _, vt in srcs), n_heads * tq), F32),
                        pltpu.VMEM((2, 1, n_heads * tq), F32)],
        compiler_params=_params(("parallel", "parallel", "parallel")),
        name=name,
    )(*args)


def _scan_constants(reverse):
    c = SCAN_C
    t = np.arange(c)[:, None]
    s = np.arange(c)[None, :]
    cum = (s >= t) if reverse else (s <= t)
    msk = np.zeros((N_LEVELS + 1, c, c), np.float32)
    for lvl in range(N_LEVELS):
        w = (c // 2) >> lvl
        same = (t // (2 * w)) == (s // (2 * w))
        t_hi = (t % (2 * w)) >= w
        s_hi = (s % (2 * w)) >= w
        msk[lvl] = (same & ~t_hi & s_hi) if reverse else (same & t_hi & ~s_hi)
    msk[N_LEVELS] = (t == s)
    return jnp.asarray(cum, BF16), jnp.asarray(np.tile(msk, (1, 1, C_HEADS)), F32)


def _boundary_rows(b, lvl, reverse):
    w = (SCAN_C // 2) >> lvl
    width = b.shape[1]
    off = w if reverse else w - 1
    if w == 1:
        odd = lax.broadcasted_iota(jnp.int32, b.shape, 0) % 2 == 1
        if reverse:
            return jnp.where(odd, b, pltpu.roll(b, SCAN_C - 1, 0))
        return jnp.where(odd, pltpu.roll(b, 1, 0), b)
    if w == 2:
        low = lax.broadcasted_iota(jnp.int32, (8, width), 0) < 4
        pieces = []
        for r0 in range(0, SCAN_C, 8):
            first = jnp.broadcast_to(b[r0 + off:r0 + off + 1, :], (8, width))
            second = jnp.broadcast_to(b[r0 + 4 + off:r0 + 5 + off, :], (8, width))
            pieces.append(jnp.where(low, first, second))
        return jnp.concatenate(pieces, axis=0)
    pieces = [jnp.broadcast_to(b[r0 + off:r0 + off + 1, :], (2 * w, width))
              for r0 in range(0, SCAN_C, 2 * w)]
    return pieces[0] if len(pieces) == 1 else jnp.concatenate(pieces, axis=0)


def _hgrn_kernel(q_ref, k_ref, ghi_ref, glo_ref, v_ref, s0_ref, cum_ref, msk_ref, bm_ref,
                 o_ref, sfin_ref, st_ref, *, reverse, n_chunk):
    i = pl.program_id(1)

    @pl.when(i == 0)
    def _():
        st_ref[...] = s0_ref[...]

    bm = bm_ref[...]
    bm16 = bm.astype(BF16)
    cum = cum_ref[...]

    def stack_heads(a):
        a16 = a.astype(BF16)
        return jnp.concatenate([a16] * C_HEADS, axis=0) * bm16

    grp = min(n_chunk, SCAN_GROUP)
    n_grp = n_chunk // grp

    def group(jg, carry):
        jgg = (n_grp - 1 - jg) if reverse else jg
        base = jgg * (grp * SCAN_C)
        order = list(range(grp - 1, -1, -1)) if reverse else list(range(grp))
        rows = [pl.ds(pl.multiple_of(base + c * SCAN_C, SCAN_C), SCAN_C) for c in range(grp)]
        q = [q_ref[r, :] for r in rows]
        k = [k_ref[r, :] for r in rows]
        v = [v_ref[r, :] for r in rows]
        b = [_dot(cum, ghi_ref[r, :]) + _dot(cum, glo_ref[r, :]) for r in rows]
        b_all = [bc[0:1, :] if reverse else bc[SCAN_C - 1:SCAN_C, :] for bc in b]

        q16 = [qc.astype(BF16) for qc in q]
        kst = [stack_heads(kc) for kc in k]
        sc = [msk_ref[N_LEVELS] * _dot_nt(q16[c], kst[c]) for c in range(grp)]
        for lvl in range(N_LEVELS):
            for c in range(grp):
                d = b[c] - _boundary_rows(b[c], lvl, reverse)
                e16 = jnp.exp2(-jnp.abs(d)).astype(BF16)
                kl = kst[c] * jnp.concatenate([e16] * C_HEADS, axis=0)
                sc[c] = sc[c] + msk_ref[lvl] * _dot_nt(q16[c] * e16, kl)

        ds = [_dot_tn(v[c].astype(BF16), (k[c] * jnp.exp2(b_all[c] - b[c])).astype(BF16)) * bm
              for c in range(grp)]
        st = st_ref[...]
        st_in = [None] * grp
        for c in order:
            st_in[c] = st
            st = st * jnp.exp2(b_all[c]) + ds[c]
        st_ref[...] = st

        for c in order:
            qdec = q[c] * jnp.exp2(b[c])
            o_ref[rows[c], :] = (_dot(sc[c].astype(BF16), stack_heads(v[c]))
                                 + _dot_nt(qdec.astype(BF16), st_in[c].astype(BF16)))
        return carry

    if n_grp == 1:
        group(0, 0)
    else:
        lax.fori_loop(0, n_grp, group, 0)

    @pl.when(i == pl.num_programs(1) - 1)
    def _():
        sfin_ref[...] = st_ref[...]


def _hgrn(c_slab, g16, s0, consts, block_mask, reverse):
    bsz, t, _ = c_slab.shape
    tb = min(512, t)
    nblk = t // tb
    cum, msk = consts

    def blk(i):
        return (nblk - 1 - i) if reverse else i

    def slab_spec(j):
        return pl.BlockSpec((None, tb, C_W), lambda b, i: (b, blk(i), j))

    g0 = 2 if reverse else 0

    return pl.pallas_call(
        functools.partial(_hgrn_kernel, reverse=reverse, n_chunk=tb // SCAN_C),
        out_shape=[jax.ShapeDtypeStruct((bsz, t, C_W), F32),
                   jax.ShapeDtypeStruct((bsz, C_W, C_W), F32)],
        grid=(bsz, nblk),
        in_specs=[
            slab_spec(C_Q), slab_spec(C_KB if reverse else C_KF), slab_spec(g0), slab_spec(g0 + 1),
            slab_spec(C_V),
            pl.BlockSpec((None, C_W, C_W), lambda b, i: (b, 0, 0)),
            _const_spec(cum.shape, (0, 0)),
            _const_spec(msk.shape, (0, 0, 0)),
            _const_spec(block_mask.shape, (0, 0)),
        ],
        out_specs=[pl.BlockSpec((None, tb, C_W), lambda b, i: (b, blk(i), 0)),
                   pl.BlockSpec((None, C_W, C_W), lambda b, i: (b, 0, 0))],
        scratch_shapes=[pltpu.VMEM((C_W, C_W), F32)],
        compiler_params=_params(("parallel", "arbitrary")),
        name="hgrn_bwd" if reverse else "hgrn_fwd",
    )(c_slab, c_slab, g16, g16, c_slab, s0, cum, msk, block_mask)


POST_SUB = 256


def _rms(v, gain):
    ms = jnp.mean(v * v, axis=-1, keepdims=True)
    return v * lax.rsqrt(ms + EPS) * gain


def _post_kernel(x_ref, ya_ref, yb_ref, of_ref, ob_ref, gate_ref,
                 gtm_ref, shf_ref, scf_ref, gtf_ref,
                 gpm_ref, gpf_ref, gqf_ref, con_ref, bo_ref,
                 woa_ref, wob_ref, woc_ref, w1_ref, w2_ref, o_ref):
    tm = x_ref.shape[0]
    sub = min(tm, POST_SUB)
    for r0 in range(0, tm, sub):
        rs = slice(r0, r0 + sub)
        o = of_ref[rs, :] + ob_ref[rs, :]
        yc = (_head_rms(o, bo_ref[...], con_ref[...]) * _silu(gate_ref[rs, :])).astype(BF16)
        mix = (_dot_tn(ya_ref[:, rs], woa_ref[...]) + _dot_tn(yb_ref[:, rs], wob_ref[...])
               + _dot(yc, woc_ref[...]))
        x1 = x_ref[rs, :] + gtm_ref[...] * _rms(mix, gpm_ref[...])
        h = _rms(x1, gpf_ref[...]) * (1.0 + scf_ref[...]) + shf_ref[...]
        u = jnp.maximum(_dot(h.astype(BF16), w1_ref[...]), 0.0)
        ff = _dot((u * u).astype(BF16), w2_ref[...])
        o_ref[rs, :] = x1 + gtf_ref[...] * _rms(ff, gqf_ref[...])


def _post(xs, yta, ytb, o_f, o_b, c_slab, mod, w, layer, mod_row):
    bsz, t, _ = xs.shape
    tm = min(512, t)
    nb_rows = w["n_mod_rows"]

    def mod_spec(j):
        return pl.BlockSpec((None, 1, D_MODEL),
                            lambda b, i: ((layer * nb_rows + mod_row(b)) * 6 + j, 0, 0))

    def vec_spec():
        return _const_spec((None, 1, D_MODEL), (layer, 0, 0))

    in_specs = [
        pl.BlockSpec((None, tm, D_MODEL), lambda b, i: (b, i, 0)),
        pl.BlockSpec((None, A_OUT, tm), lambda b, i: (b, 0, i)),
        pl.BlockSpec((None, B_OUT, tm), lambda b, i: (b, 0, i)),
        pl.BlockSpec((None, tm, C_W), lambda b, i: (b, i, 0)),
        pl.BlockSpec((None, tm, C_W), lambda b, i: (b, i, 0)),
        pl.BlockSpec((None, tm, C_W), lambda b, i: (b, i, C_GATE)),
        mod_spec(2), mod_spec(3), mod_spec(4), mod_spec(5),
        vec_spec(), vec_spec(), vec_spec(),
        _const_spec((None, 1, C_W), (layer, 0, 0)),
        _const_spec((256, 256), (0, 0)),
        _const_spec((None, A_OUT, D_MODEL), (layer, 0, 0)),
        _const_spec((None, B_OUT, D_MODEL), (layer, 0, 0)),
        _const_spec((None, C_OUT, D_MODEL), (layer, 0, 0)),
        _const_spec((None, D_MODEL, D_FF), (layer, 0, 0)),
        _const_spec((None, D_FF, D_MODEL), (layer, 0, 0)),
    ]
    return pl.pallas_call(
        _post_kernel,
        out_shape=jax.ShapeDtypeStruct((bsz, t, D_MODEL), F32),
        grid=(bsz, t // tm),
        in_specs=in_specs,
        out_specs=pl.BlockSpec((None, tm, D_MODEL), lambda b, i: (b, i, 0)),
        compiler_params=_params(("parallel", "parallel")),
        name="post",
    )(xs, yta, ytb, o_f, o_b, c_slab, mod, mod, mod, mod,
      w["g_post_mix"], w["g_pre_ffn"], w["g_post_ffn"], w["c_out_norm"], w["block_ones"],
      w["w_out_a"], w["w_out_b"], w["w_out_c"], w["w_ff1"], w["w_ff2"])


def _w_in_columns():
    src = np.full((N_COL,), -1, np.int64)
    a_k0 = A_HEADS * HEAD_DIM
    a_v0 = a_k0 + A_KV_HEADS * HEAD_DIM
    b_qd0 = a_v0 + A_KV_HEADS * HEAD_DIM
    b_kv0 = b_qd0 + B_Q_RANK
    b_kr0 = b_kv0 + B_KV_RANK
    c0 = b_kr0 + B_ROPE
    src[OFF_AQ:OFF_AQ + 512] = np.arange(512)
    for g in range(A_KV_HEADS):
        for rep in range(2):
            lo = OFF_AK + (2 * g + rep) * HEAD_DIM
            src[lo:lo + HEAD_DIM] = a_k0 + g * HEAD_DIM + np.arange(HEAD_DIM)
    src[OFF_AV:OFF_AV + 128] = a_v0 + np.arange(128)
    src[OFF_BQD:OFF_BQD + B_Q_RANK] = b_qd0 + np.arange(B_Q_RANK)
    src[OFF_BKVD:OFF_BKVD + B_KV_RANK] = b_kv0 + np.arange(B_KV_RANK)
    src[OFF_KPE + B_NOPE:OFF_KPE + B_NOPE + B_ROPE] = b_kr0 + np.arange(B_ROPE)
    src[OFF_CQ:OFF_CQ + 5 * C_W] = c0 + np.arange(5 * C_W)
    return src


def _gather_cols(w, src):
    keep = jnp.asarray(src >= 0)
    return jnp.where(keep, jnp.take(w, jnp.asarray(np.maximum(src, 0)), axis=-1), 0.0)


def _prepare_weights(p, n_mod_rows):
    depth = p["w_in"].shape[0]
    w = {"n_mod_rows": n_mod_rows}
    w["w_in"] = _gather_cols(p["w_in"], _w_in_columns()).astype(BF16)
    src = np.full((B_HEADS * LANE,), -1, np.int64)
    for hh in range(B_HEADS):
        src[hh * LANE:hh * LANE + B_NOPE + B_ROPE] = hh * (B_NOPE + B_ROPE) + np.arange(B_NOPE + B_ROPE)
    wq = _gather_cols(p["w_q_up"], src)
    w["w_q_up"] = jnp.pad(wq, ((0, 0), (0, 256 - B_Q_RANK), (0, 0))).astype(BF16)
    src = np.full((B_HEADS * LANE,), -1, np.int64)
    srcv = np.zeros((B_HEADS * B_V,), np.int64)
    for hh in range(B_HEADS):
        src[hh * LANE:hh * LANE + B_NOPE] = hh * (B_NOPE + B_V) + np.arange(B_NOPE)
        srcv[hh * B_V:(hh + 1) * B_V] = hh * (B_NOPE + B_V) + B_NOPE + np.arange(B_V)
    w["w_kv_k"] = _gather_cols(p["w_kv_up"], src).astype(BF16)
    w["w_kv_v"] = _gather_cols(p["w_kv_up"], srcv).astype(BF16)
    w["a_q_norm"] = jnp.tile(p["a_q_norm"], (1, 4))[:, None, :]
    w["a_k_norm"] = jnp.tile(p["a_k_norm"], (1, 4))[:, None, :]
    w["b_q_norm"] = jnp.pad(p["b_q_norm"], ((0, 0), (0, 256 - B_Q_RANK)))[:, None, :]
    w["b_kv_norm"] = p["b_kv_norm"][:, None, :]
    w["c_out_norm"] = jnp.tile(p["c_out_norm"], (1, C_HEADS))[:, None, :]
    for name in ("g_pre_mix", "g_post_mix", "g_pre_ffn", "g_post_ffn"):
        w[name] = p[name][:, None, :]
    head = np.arange(256) // HEAD_DIM
    same_head = head[:, None] == head[None, :]
    w["block_ones"] = jnp.asarray(same_head / float(HEAD_DIM), BF16)
    w["block_mask"] = jnp.asarray(same_head, F32)
    p_lb = jax.nn.softmax(p["c_lower_bounds"].astype(F32), axis=0)
    w["lower"] = jnp.cumsum(p_lb, axis=0) - p_lb[:1]
    w["w_out_a"] = p["w_out"][:, :A_OUT].astype(BF16)
    w["w_out_b"] = p["w_out"][:, A_OUT:A_OUT + B_OUT].astype(BF16)
    w["w_out_c"] = p["w_out"][:, A_OUT + B_OUT:].astype(BF16)
    w["w_ff1"] = p["w_ff1"].astype(BF16)
    w["w_ff2"] = p["w_ff2"].astype(BF16)
    del depth
    return w


def _rope_tables(n_tok):
    tok = np.arange(n_tok)
    row = (tok // GRID_W).astype(np.float32)[:, None]
    col = (tok % GRID_W).astype(np.float32)[:, None]

    def angles(rot_dim):
        n_freq = rot_dim // 4
        inv = jnp.asarray(ROPE_THETA, F32) ** (-jnp.arange(n_freq, dtype=F32) / n_freq)
        ang = jnp.concatenate([jnp.asarray(row) * inv, jnp.asarray(col) * inv], axis=-1)
        return jnp.cos(ang), jnp.sin(ang)

    zeros = lambda n: jnp.zeros((n_tok, n), F32)
    ones = lambda n: jnp.ones((n_tok, n), F32)
    cos, sin = angles(HEAD_DIM)
    ca = jnp.concatenate([cos, cos, cos, cos], axis=-1)
    s1a = jnp.concatenate([-sin, zeros(32), -sin, zeros(32)], axis=-1)
    s2a = jnp.concatenate([zeros(32), sin, zeros(32), sin], axis=-1)
    cos, sin = angles(B_ROPE)
    cb = jnp.concatenate([ones(64), cos, cos, ones(32)], axis=-1)
    s1b = jnp.concatenate([zeros(64), -sin, zeros(16), zeros(32)], axis=-1)
    s2b = jnp.concatenate([zeros(64), zeros(16), sin, zeros(32)], axis=-1)
    return ca, s1a, s2a, cb, s1b, s2b


def kernel(x, c, ctx, c_ctx, w_ada, b_ada, g_pre_mix, g_post_mix, g_pre_ffn, g_post_ffn, w_in, a_q_norm, a_k_norm, b_q_norm, w_q_up, b_kv_norm, w_kv_up, c_lower_bounds, c_out_norm, w_out, w_ff1, w_ff2):
    bsz, n_lat, _ = x.shape
    depth = w_in.shape[0]
    n_mod_rows = -(-(bsz + 1) // 8) * 8
    params = dict(w_in=w_in, a_q_norm=a_q_norm, a_k_norm=a_k_norm, b_q_norm=b_q_norm, w_q_up=w_q_up,
                  b_kv_norm=b_kv_norm, w_kv_up=w_kv_up, c_lower_bounds=c_lower_bounds,
                  c_out_norm=c_out_norm, w_out=w_out, w_ff1=w_ff1, w_ff2=w_ff2,
                  g_pre_mix=g_pre_mix, g_post_mix=g_post_mix, g_pre_ffn=g_pre_ffn, g_post_ffn=g_post_ffn)
    w = _prepare_weights(params, n_mod_rows)
    rope = _rope_tables(n_lat)
    scan_f = _scan_constants(False)
    scan_b = _scan_constants(True)

    cvec = jnp.concatenate([c, c_ctx[None, :], jnp.zeros((n_mod_rows - bsz - 1, D_MODEL), F32)], axis=0)
    mod = _ada(cvec, w_ada, b_ada).reshape(depth * n_mod_rows * 6, 1, D_MODEL)

    lat_row = lambda b: b
    ctx_row = lambda b: bsz
    zero_state = jnp.zeros((bsz, C_W, C_W), F32)
    attn_a = functools.partial(_attention, n_kv=A_KV_HEADS, n_heads=A_HEADS // A_KV_HEADS, shared_kv=True)
    attn_b = functools.partial(_attention, n_kv=1, n_heads=B_HEADS, shared_kv=False)

    xc = ctx
    for layer in range(depth):
        need_ctx = layer < depth - 1
        fl = _features(x, mod, w, layer, lat_row, rope)
        fc = _features(xc, mod, w, layer, ctx_row, None)
        yta = attn_a(fl["aq"], [(fl["ak"], fl["avt"]), (fc["ak"], fc["avt"])], name="attn_a")
        ytb = attn_b(fl["bq"], [(fl["bk"], fl["bvt"]), (fc["bk"], fc["bvt"])], name="attn_b")
        ocf, s_f = _hgrn(fc["c"], fc["g"], zero_state, scan_f, w["block_mask"], False)
        ocb, s_b = _hgrn(fc["c"], fc["g"], zero_state, scan_b, w["block_mask"], True)
        olf, _ = _hgrn(fl["c"], fl["g"], s_f, scan_f, w["block_mask"], False)
        olb, _ = _hgrn(fl["c"], fl["g"], s_b, scan_b, w["block_mask"], True)
        x_new = _post(x, yta, ytb, olf, olb, fl["c"], mod, w, layer, lat_row)
        if need_ctx:
            yta_c = attn_a(fc["aq"], [(fc["ak"], fc["avt"])], name="attn_a_ctx")
            ytb_c = attn_b(fc["bq"], [(fc["bk"], fc["bvt"])], name="attn_b_ctx")
            xc = _post(xc, yta_c, ytb_c, ocf, ocb, fc["c"], mod, w, layer, ctx_row)
        x = x_new
    return x
```

```python
import functools

import numpy as np
import jax
import jax.numpy as jnp
from jax import lax
from jax.experimental import pallas as pl
from jax.experimental.pallas import tpu as pltpu

F32 = jnp.float32
BF16 = jnp.bfloat16

D_MODEL = 1024
GRID_W = 64
HEAD_DIM = 64
A_HEADS = 8
A_KV_HEADS = 2
B_HEADS = 4
B_Q_RANK = 192
B_KV_RANK = 128
B_NOPE = 64
B_ROPE = 32
B_V = 64
C_HEADS = 4
C_DK = 64
C_DV = 64
D_FF = 4 * D_MODEL
A_OUT = A_HEADS * HEAD_DIM
B_OUT = B_HEADS * B_V
C_OUT = C_HEADS * C_DV
C_W = C_HEADS * C_DK
ROPE_THETA = 10000.0
EPS = 1e-6
F_TINY = 1e-30

LANE = 128
VMEM_LIMIT = 56 * 1024 * 1024

OFF_AQ = 0
OFF_AK = 512
OFF_AV = 768
OFF_BQD = 896
OFF_BKVD = 1152
OFF_KPE = 1280
OFF_CQ = 1408
OFF_CFF = 1664
OFF_CFB = 1920
OFF_CI = 2176
OFF_CG = 2432
N_COL = 2688

C_Q, C_KF, C_KB, C_V, C_GATE = range(5)
N_SLAB = 5

SCAN_C = 64
N_LEVELS = 6
SCAN_GROUP = 8


def _dot(a, b):
    return jnp.dot(a, b, preferred_element_type=F32)


def _dot_nt(a, b):
    return lax.dot_general(a, b, (((1,), (1,)), ((), ())), preferred_element_type=F32)


def _dot_tn(a, b):
    return lax.dot_general(a, b, (((0,), (0,)), ((), ())), preferred_element_type=F32)


def _sigmoid_pair(z):
    e = jnp.exp(-jnp.abs(z))
    inv = 1.0 / (1.0 + e)
    small = e * inv
    pos = z >= 0
    return jnp.where(pos, inv, small), jnp.where(pos, small, inv)


def _silu(z):
    s, _ = _sigmoid_pair(z)
    return z * s


def _const_spec(shape, index):
    return pl.BlockSpec(shape, lambda *_: index, pipeline_mode=pl.Buffered(1))


def _params(sem):
    return pltpu.CompilerParams(dimension_semantics=sem, vmem_limit_bytes=VMEM_LIMIT)


def _ada_kernel(c_ref, w_ref, b_ref, o_ref):
    a = _silu(c_ref[...])
    w = w_ref[...]
    a_hi = a.astype(BF16)
    a_lo = (a - a_hi.astype(F32)).astype(BF16)
    w_hi = w.astype(BF16)
    w_lo = (w - w_hi.astype(F32)).astype(BF16)
    acc = _dot(a_hi, w_hi) + (_dot(a_hi, w_lo) + _dot(a_lo, w_hi))
    o_ref[...] = acc + b_ref[...]


def _ada(cvec, w_ada, b_ada):
    depth = w_ada.shape[0]
    rows = cvec.shape[0]
    n_blk = w_ada.shape[2] // D_MODEL
    return pl.pallas_call(
        _ada_kernel,
        out_shape=jax.ShapeDtypeStruct((depth, rows, n_blk * D_MODEL), F32),
        grid=(depth, n_blk),
        in_specs=[
            pl.BlockSpec((rows, D_MODEL), lambda l, j: (0, 0)),
            pl.BlockSpec((None, D_MODEL, D_MODEL), lambda l, j: (l, 0, j)),
            pl.BlockSpec((None, None, 1, D_MODEL), lambda l, j: (l, j, 0, 0)),
        ],
        out_specs=pl.BlockSpec((None, rows, D_MODEL), lambda l, j: (l, 0, j)),
        compiler_params=_params(("arbitrary", "arbitrary")),
        name="ada",
    )(cvec, w_ada, b_ada.reshape(depth, n_blk, 1, D_MODEL))


def _rope128(v, cos, s1, s2, half):
    up = pltpu.roll(v, LANE - half, 1)
    dn = pltpu.roll(v, half, 1)
    return v * cos + up * s1 + dn * s2


def _head_rms(v, bo, gain):
    sq = v * v
    hi = sq.astype(BF16)
    lo = (sq - hi.astype(F32)).astype(BF16)
    ms = _dot(hi, bo) + _dot(lo, bo)
    return v * lax.rsqrt(ms + EPS) * gain


FEAT_SUB = 256


def _feat_kernel(*refs, use_rope):
    (x_ref, sh_ref, sc_ref, gpre_ref, win_ref, wq_ref, wkk_ref, wkv_ref,
     aqn_ref, akn_ref, bqn_ref, bkvn_ref, bo_ref, lb_ref) = refs[:14]
    rest = refs[14:]
    if use_rope:
        ca_ref, s1a_ref, s2a_ref, cb_ref, s1b_ref, s2b_ref = rest[:6]
        rest = rest[6:]
    aq_o, ak_o, avt_o, bq_o, bk_o, bvt_o, c_o, g_o = rest
    bo = bo_ref[...]
    tm = x_ref.shape[0]
    sub = min(tm, FEAT_SUB)

    for r0 in range(0, tm, sub):
        rs = slice(r0, r0 + sub)

        def rope_a(v, rs=rs):
            if not use_rope:
                return v
            return _rope128(v, ca_ref[rs, :], s1a_ref[rs, :], s2a_ref[rs, :], HEAD_DIM // 2)

        def rope_b(v, rs=rs):
            if not use_rope:
                return v
            return _rope128(v, cb_ref[rs, :], s1b_ref[rs, :], s2b_ref[rs, :], B_ROPE // 2)

        x = x_ref[rs, :]
        ms = jnp.mean(x * x, axis=-1, keepdims=True)
        h = x * lax.rsqrt(ms + EPS) * gpre_ref[...]
        h = h * (1.0 + sc_ref[...]) + sh_ref[...]
        p = _dot(h.astype(BF16), win_ref[...])

        aqn = aqn_ref[...]
        for half in range(2):
            v = _head_rms(p[:, OFF_AQ + 256 * half:OFF_AQ + 256 * (half + 1)], bo, aqn)
            for s in range(2):
                blk = rope_a(v[:, LANE * s:LANE * (s + 1)]) * (HEAD_DIM ** -0.5 * LOG2_E)
                lo = 256 * half + LANE * s
                aq_o[lo:lo + LANE, rs] = blk.T.astype(BF16)
        v = _head_rms(p[:, OFF_AK:OFF_AK + 256], bo, akn_ref[...])
        for s in range(2):
            ak_o[rs, LANE * s:LANE * (s + 1)] = rope_a(v[:, LANE * s:LANE * (s + 1)]).astype(BF16)
        avt_o[:, rs] = p[:, OFF_AV:OFF_AV + LANE].T.astype(BF16)

        bqd = p[:, OFF_BQD:OFF_BQD + 256]
        ms = jnp.sum(bqd * bqd, axis=-1, keepdims=True) * (1.0 / B_Q_RANK)
        qn = (bqd * lax.rsqrt(ms + EPS) * bqn_ref[...]).astype(BF16)
        bq = _dot(qn, wq_ref[...])
        bkvd = p[:, OFF_BKVD:OFF_BKVD + LANE]
        ms = jnp.mean(bkvd * bkvd, axis=-1, keepdims=True)
        kvn = (bkvd * lax.rsqrt(ms + EPS) * bkvn_ref[...]).astype(BF16)
        bkn = _dot(kvn, wkk_ref[...])
        bv = _dot(kvn, wkv_ref[...])
        kpe = rope_b(p[:, OFF_KPE:OFF_KPE + LANE])
        b_scale = (B_NOPE + B_ROPE) ** -0.5 * LOG2_E
        for hh in range(B_HEADS):
            sl = slice(LANE * hh, LANE * (hh + 1))
            bq_o[sl, rs] = (rope_b(bq[:, sl]) * b_scale).T.astype(BF16)
            bk_o[rs, sl] = (bkn[:, sl] + kpe).astype(BF16)
        bvt_o[:, rs] = bv.T.astype(BF16)

        c_o[rs, C_W * C_Q:C_W * (C_Q + 1)] = _silu(p[:, OFF_CQ:OFF_CQ + C_W])
        for d, (off, ck) in enumerate(((OFF_CFF, C_KF), (OFF_CFB, C_KB))):
            lb = lb_ref[d:d + 1, :]
            sp, sn = _sigmoid_pair(p[:, off:off + C_W])
            f = lb + (1.0 - lb) * sp
            g2 = jnp.log2(jnp.maximum(f, F_TINY))
            hi = g2.astype(BF16)
            g_o[rs, C_W * 2 * d:C_W * (2 * d + 1)] = hi
            g_o[rs, C_W * (2 * d + 1):C_W * (2 * d + 2)] = (g2 - hi.astype(F32)).astype(BF16)
            c_o[rs, C_W * ck:C_W * (ck + 1)] = (1.0 - lb) * sn
        c_o[rs, C_W * C_V:C_W * (C_V + 1)] = p[:, OFF_CI:OFF_CI + C_W]
        c_o[rs, C_W * C_GATE:C_W * (C_GATE + 1)] = p[:, OFF_CG:OFF_CG + C_W]


def _features(xs, mod, w, layer, mod_row, rope):
    bsz, t, _ = xs.shape
    tm = min(512, t)
    nt = t // tm
    nb_rows = w["n_mod_rows"]

    def mod_spec(j):
        return pl.BlockSpec((None, 1, D_MODEL),
                            lambda b, i: ((layer * nb_rows + mod_row(b)) * 6 + j, 0, 0))

    in_specs = [
        pl.BlockSpec((None, tm, D_MODEL), lambda b, i: (b, i, 0)),
        mod_spec(0), mod_spec(1),
        _const_spec((None, 1, D_MODEL), (layer, 0, 0)),
        _const_spec((None, D_MODEL, N_COL), (layer, 0, 0)),
        _const_spec((None, 256, 512), (layer, 0, 0)),
        _const_spec((None, LANE, 512), (layer, 0, 0)),
        _const_spec((None, LANE, 256), (layer, 0, 0)),
        _const_spec((None, 1, 256), (layer, 0, 0)),
        _const_spec((None, 1, 256), (layer, 0, 0)),
        _const_spec((None, 1, 256), (layer, 0, 0)),
        _const_spec((None, 1, LANE), (layer, 0, 0)),
        _const_spec((256, 256), (0, 0)),
        _const_spec((None, 2, C_W), (layer, 0, 0)),
    ]
    args = [xs, mod, mod, w["g_pre_mix"], w["w_in"], w["w_q_up"], w["w_kv_k"], w["w_kv_v"],
            w["a_q_norm"], w["a_k_norm"], w["b_q_norm"], w["b_kv_norm"], w["block_ones"], w["lower"]]
    if rope is not None:
        in_specs += [pl.BlockSpec((tm, LANE), lambda b, i: (i, 0))] * 6
        args += list(rope)
    out_shape = [
        jax.ShapeDtypeStruct((bsz, 512, t), BF16),
        jax.ShapeDtypeStruct((bsz, t, 256), BF16),
        jax.ShapeDtypeStruct((bsz, nt, LANE, tm), BF16),
        jax.ShapeDtypeStruct((bsz, 512, t), BF16),
        jax.ShapeDtypeStruct((bsz, t, 512), BF16),
        jax.ShapeDtypeStruct((bsz, nt, 256, tm), BF16),
        jax.ShapeDtypeStruct((bsz, t, N_SLAB * C_W), F32),
        jax.ShapeDtypeStruct((bsz, t, 4 * C_W), BF16),
    ]
    out_specs = [
        pl.BlockSpec((None, 512, tm), lambda b, i: (b, 0, i)),
        pl.BlockSpec((None, tm, 256), lambda b, i: (b, i, 0)),
        pl.BlockSpec((None, None, LANE, tm), lambda b, i: (b, i, 0, 0)),
        pl.BlockSpec((None, 512, tm), lambda b, i: (b, 0, i)),
        pl.BlockSpec((None, tm, 512), lambda b, i: (b, i, 0)),
        pl.BlockSpec((None, None, 256, tm), lambda b, i: (b, i, 0, 0)),
        pl.BlockSpec((None, tm, N_SLAB * C_W), lambda b, i: (b, i, 0)),
        pl.BlockSpec((None, tm, 4 * C_W), lambda b, i: (b, i, 0)),
    ]
    outs = pl.pallas_call(
        functools.partial(_feat_kernel, use_rope=rope is not None),
        out_shape=out_shape, grid=(bsz, nt), in_specs=in_specs, out_specs=out_specs,
        compiler_params=_params(("parallel", "parallel")),
        name="feat_rope" if rope is not None else "feat_ctx",
    )(*args)
    return dict(zip(("aq", "ak", "avt", "bq", "bk", "bvt", "c", "g"), outs))


ACC_ROWS = B_V + 16
NEG_BIG = -1e30
LOG2_E = 1.4426950408889634
ATTN_TQ = 256


def _attn_kernel(*refs, n_src, n_heads, shared_kv):
    q_ref = refs[0]
    src_refs = refs[1:1 + 2 * n_src]
    o_ref = refs[1 + 2 * n_src]
    rhs_scr, m_scr, acc_scr, st_scr, mx_scr = refs[2 + 2 * n_src:]
    tq = q_ref.shape[1]
    n_grp, k_dim, grp_w = rhs_scr.shape
    heads_per_grp = n_heads // n_grp

    if shared_kv:
        for h in range(n_heads):
            rhs_scr[0, :, tq * h:tq * (h + 1)] = q_ref[HEAD_DIM * h:HEAD_DIM * (h + 1), :]
    else:
        rhs_scr[...] = jnp.zeros(rhs_scr.shape, BF16)
        for h in range(n_heads):
            g, j = divmod(h, heads_per_grp)
            rhs_scr[g, LANE * j:LANE * (j + 1), tq * j:tq * (j + 1)] = q_ref[LANE * h:LANE * (h + 1), :]
    m_scr[...] = jnp.full(m_scr.shape, NEG_BIG, F32)
    acc_scr[...] = jnp.zeros(acc_scr.shape, F32)

    def stage(s, c, slot):
        k_ref, vt_ref = src_refs[2 * s], src_refs[2 * s + 1]
        tk = vt_ref.shape[2]
        rows = pl.ds(pl.multiple_of(c * tk, tk), tk)
        for g in range(n_grp):
            cols = slice(grp_w * g, grp_w * (g + 1))
            kc = k_ref[rows, 0:k_dim] if shared_kv else k_ref[rows, k_dim * g:k_dim * (g + 1)]
            st = _dot(kc, rhs_scr[g])
            st_scr[slot, 0:tk, cols] = st
            mx_scr[slot, :, cols] = jnp.max(st, axis=0, keepdims=True)

    def consume(s, c, slot):
        vt_ref = src_refs[2 * s + 1]
        tk = vt_ref.shape[2]
        ones = jnp.ones((ACC_ROWS - B_V, tk), BF16)
        for g in range(n_grp):
            cols = slice(grp_w * g, grp_w * (g + 1))
            m_old = m_scr[:, cols]
            m_new = jnp.maximum(m_old, mx_scr[slot, :, cols])
            pt = jnp.exp2(st_scr[slot, 0:tk, cols] - m_new).astype(BF16)
            alpha = jnp.exp2(m_old - m_new)
            m_scr[:, cols] = m_new
            if shared_kv:
                vt = jnp.concatenate([vt_ref[c], ones], axis=0)
                acc_scr[:, cols] = alpha * acc_scr[:, cols] + _dot(vt, pt)
            else:
                for j in range(heads_per_grp):
                    h = g * heads_per_grp + j
                    hc = slice(tq * h, tq * (h + 1))
                    lc = slice(tq * j, tq * (j + 1))
                    vt = jnp.concatenate([vt_ref[c, B_V * h:B_V * (h + 1), :], ones], axis=0)
                    acc_scr[:, hc] = alpha[:, lc] * acc_scr[:, hc] + _dot(vt, pt[:, lc])

    n0 = src_refs[1].shape[0]
    n_loop = (n0 - 2) // 2 if n0 >= 4 else 0
    stage(0, 0, 0)
    if n_loop:
        def pair(i, carry):
            stage(0, 2 * i + 1, 1)
            consume(0, 2 * i, 0)
            stage(0, 2 * i + 2, 0)
            consume(0, 2 * i + 1, 1)
            return carry
        lax.fori_loop(0, n_loop, pair, 0)
    tail = [(0, c) for c in range(2 * n_loop, n0)]
    tail += [(s, c) for s in range(1, n_src) for c in range(src_refs[2 * s + 1].shape[0])]
    for i, (s, c) in enumerate(tail):
        if i + 1 < len(tail):
            stage(*tail[i + 1], (i + 1) % 2)
        consume(s, c, i % 2)

    for h in range(n_heads):
        acc = acc_scr[:, tq * h:tq * (h + 1)]
        o_ref[B_V * h:B_V * (h + 1), :] = (acc[:B_V] * (1.0 / acc[B_V:B_V + 1])).astype(BF16)


def _attention(q, srcs, *, n_kv, n_heads, shared_kv, name):
    bsz, _, t = q.shape
    tq = min(ATTN_TQ, t)
    q_r = (HEAD_DIM if shared_kv else LANE) * n_heads
    k_w = LANE * (1 if shared_kv else n_heads)
    v_r = B_V * (1 if shared_kv else n_heads)
    rhs_shape = (1, HEAD_DIM, n_heads * tq) if shared_kv else (n_heads // 2, 2 * LANE, 2 * tq)
    in_specs = [pl.BlockSpec((None, q_r, tq), lambda b, g, i: (b, g, i))]
    args = [q]
    for k, vt in srcs:
        length = k.shape[1]
        n_chunk, tk = vt.shape[1], vt.shape[3]
        in_specs.append(pl.BlockSpec((None, length, k_w), lambda b, g, i: (b, 0, g)))
        in_specs.append(pl.BlockSpec((None, n_chunk, v_r, tk), lambda b, g, i: (b, 0, g, 0)))
        args += [k, vt]
    rows = B_V * n_heads
    return pl.pallas_call(
        functools.partial(_attn_kernel, n_src=len(srcs), n_heads=n_heads, shared_kv=shared_kv),
        out_shape=jax.ShapeDtypeStruct((bsz, rows * n_kv, t), BF16),
        grid=(bsz, n_kv, t // tq),
        in_specs=in_specs,
        out_specs=pl.BlockSpec((None, rows, tq), lambda b, g, i: (b, g, i)),
        scratch_shapes=[pltpu.VMEM(rhs_shape, BF16),
                        pltpu.VMEM((1, n_heads * tq), F32),
                        pltpu.VMEM((ACC_ROWS, n_heads * tq), F32),
                        pltpu.VMEM((2, max(vt.shape[3] for _, vt in srcs), n_heads * tq), F32),
                        pltpu.VMEM((2, 1, n_heads * tq), F32)],
        compiler_params=_params(("parallel", "parallel", "parallel")),
        name=name,
    )(*args)


def _scan_constants(reverse):
    c = SCAN_C
    t = np.arange(c)[:, None]
    s = np.arange(c)[None, :]
    cum = (s >= t) if reverse else (s <= t)
    msk = np.zeros((N_LEVELS + 1, c, c), np.float32)
    for lvl in range(N_LEVELS):
        w = (c // 2) >> lvl
        same = (t // (2 * w)) == (s // (2 * w))
        t_hi = (t % (2 * w)) >= w
        s_hi = (s % (2 * w)) >= w
        msk[lvl] = (same & ~t_hi & s_hi) if reverse else (same & t_hi & ~s_hi)
    msk[N_LEVELS] = (t == s)
    return jnp.asarray(cum, BF16), jnp.asarray(np.tile(msk, (1, 1, C_HEADS)), F32)


def _boundary_rows(b, lvl, reverse):
    w = (SCAN_C // 2) >> lvl
    width = b.shape[1]
    off = w if reverse else w - 1
    if w == 1:
        odd = lax.broadcasted_iota(jnp.int32, b.shape, 0) % 2 == 1
        if reverse:
            return jnp.where(odd, b, pltpu.roll(b, SCAN_C - 1, 0))
        return jnp.where(odd, pltpu.roll(b, 1, 0), b)
    if w == 2:
        low = lax.broadcasted_iota(jnp.int32, (8, width), 0) < 4
        pieces = []
        for r0 in range(0, SCAN_C, 8):
            first = jnp.broadcast_to(b[r0 + off:r0 + off + 1, :], (8, width))
            second = jnp.broadcast_to(b[r0 + 4 + off:r0 + 5 + off, :], (8, width))
            pieces.append(jnp.where(low, first, second))
        return jnp.concatenate(pieces, axis=0)
    pieces = [jnp.broadcast_to(b[r0 + off:r0 + off + 1, :], (2 * w, width))
              for r0 in range(0, SCAN_C, 2 * w)]
    return pieces[0] if len(pieces) == 1 else jnp.concatenate(pieces, axis=0)


def _hgrn_kernel(q_ref, k_ref, ghi_ref, glo_ref, v_ref, s0_ref, cum_ref, msk_ref, bm_ref,
                 o_ref, sfin_ref, st_ref, *, reverse, n_chunk):
    i = pl.program_id(1)

    @pl.when(i == 0)
    def _():
        st_ref[...] = s0_ref[...]

    bm = bm_ref[...]
    bm16 = bm.astype(BF16)
    cum = cum_ref[...]

    def stack_heads(a):
        a16 = a.astype(BF16)
        return jnp.concatenate([a16] * C_HEADS, axis=0) * bm16

    grp = min(n_chunk, SCAN_GROUP)
    n_grp = n_chunk // grp

    def group(jg, carry):
        jgg = (n_grp - 1 - jg) if reverse else jg
        base = jgg * (grp * SCAN_C)
        order = list(range(grp - 1, -1, -1)) if reverse else list(range(grp))
        rows = [pl.ds(pl.multiple_of(base + c * SCAN_C, SCAN_C), SCAN_C) for c in range(grp)]
        q = [q_ref[r, :] for r in rows]
        k = [k_ref[r, :] for r in rows]
        v = [v_ref[r, :] for r in rows]
        b = [_dot(cum, ghi_ref[r, :]) + _dot(cum, glo_ref[r, :]) for r in rows]
        b_all = [bc[0:1, :] if reverse else bc[SCAN_C - 1:SCAN_C, :] for bc in b]

        q16 = [qc.astype(BF16) for qc in q]
        kst = [stack_heads(kc) for kc in k]
        sc = [msk_ref[N_LEVELS] * _dot_nt(q16[c], kst[c]) for c in range(grp)]
        for lvl in range(N_LEVELS):
            for c in range(grp):
                d = b[c] - _boundary_rows(b[c], lvl, reverse)
                e16 = jnp.exp2(-jnp.abs(d)).astype(BF16)
                kl = kst[c] * jnp.concatenate([e16] * C_HEADS, axis=0)
                sc[c] = sc[c] + msk_ref[lvl] * _dot_nt(q16[c] * e16, kl)

        ds = [_dot_tn(v[c].astype(BF16), (k[c] * jnp.exp2(b_all[c] - b[c])).astype(BF16)) * bm
              for c in range(grp)]
        st = st_ref[...]
        st_in = [None] * grp
        for c in order:
            st_in[c] = st
            st = st * jnp.exp2(b_all[c]) + ds[c]
        st_ref[...] = st

        for c in order:
            qdec = q[c] * jnp.exp2(b[c])
            o_ref[rows[c], :] = (_dot(sc[c].astype(BF16), stack_heads(v[c]))
                                 + _dot_nt(qdec.astype(BF16), st_in[c].astype(BF16)))
        return carry

    if n_grp == 1:
        group(0, 0)
    else:
        lax.fori_loop(0, n_grp, group, 0)

    @pl.when(i == pl.num_programs(1) - 1)
    def _():
        sfin_ref[...] = st_ref[...]


def _hgrn(c_slab, g16, s0, consts, block_mask, reverse):
    bsz, t, _ = c_slab.shape
    tb = min(512, t)
    nblk = t // tb
    cum, msk = consts

    def blk(i):
        return (nblk - 1 - i) if reverse else i

    def slab_spec(j):
        return pl.BlockSpec((None, tb, C_W), lambda b, i: (b, blk(i), j))

    g0 = 2 if reverse else 0

    return pl.pallas_call(
        functools.partial(_hgrn_kernel, reverse=reverse, n_chunk=tb // SCAN_C),
        out_shape=[jax.ShapeDtypeStruct((bsz, t, C_W), F32),
                   jax.ShapeDtypeStruct((bsz, C_W, C_W), F32)],
        grid=(bsz, nblk),
        in_specs=[
            slab_spec(C_Q), slab_spec(C_KB if reverse else C_KF), slab_spec(g0), slab_spec(g0 + 1),
            slab_spec(C_V),
            pl.BlockSpec((None, C_W, C_W), lambda b, i: (b, 0, 0)),
            _const_spec(cum.shape, (0, 0)),
            _const_spec(msk.shape, (0, 0, 0)),
            _const_spec(block_mask.shape, (0, 0)),
        ],
        out_specs=[pl.BlockSpec((None, tb, C_W), lambda b, i: (b, blk(i), 0)),
                   pl.BlockSpec((None, C_W, C_W), lambda b, i: (b, 0, 0))],
        scratch_shapes=[pltpu.VMEM((C_W, C_W), F32)],
        compiler_params=_params(("parallel", "arbitrary")),
        name="hgrn_bwd" if reverse else "hgrn_fwd",
    )(c_slab, c_slab, g16, g16, c_slab, s0, cum, msk, block_mask)


POST_SUB = 256


def _rms(v, gain):
    ms = jnp.mean(v * v, axis=-1, keepdims=True)
    return v * lax.rsqrt(ms + EPS) * gain


def _post_kernel(x_ref, ya_ref, yb_ref, of_ref, ob_ref, gate_ref,
                 gtm_ref, shf_ref, scf_ref, gtf_ref,
                 gpm_ref, gpf_ref, gqf_ref, con_ref, bo_ref,
                 woa_ref, wob_ref, woc_ref, w1_ref, w2_ref, o_ref):
    tm = x_ref.shape[0]
    sub = min(tm, POST_SUB)
    for r0 in range(0, tm, sub):
        rs = slice(r0, r0 + sub)
        o = of_ref[rs, :] + ob_ref[rs, :]
        yc = (_head_rms(o, bo_ref[...], con_ref[...]) * _silu(gate_ref[rs, :])).astype(BF16)
        mix = (_dot_tn(ya_ref[:, rs], woa_ref[...]) + _dot_tn(yb_ref[:, rs], wob_ref[...])
               + _dot(yc, woc_ref[...]))
        x1 = x_ref[rs, :] + gtm_ref[...] * _rms(mix, gpm_ref[...])
        h = _rms(x1, gpf_ref[...]) * (1.0 + scf_ref[...]) + shf_ref[...]
        u = jnp.maximum(_dot(h.astype(BF16), w1_ref[...]), 0.0)
        ff = _dot((u * u).astype(BF16), w2_ref[...])
        o_ref[rs, :] = x1 + gtf_ref[...] * _rms(ff, gqf_ref[...])


def _post(xs, yta, ytb, o_f, o_b, c_slab, mod, w, layer, mod_row):
    bsz, t, _ = xs.shape
    tm = min(512, t)
    nb_rows = w["n_mod_rows"]

    def mod_spec(j):
        return pl.BlockSpec((None, 1, D_MODEL),
                            lambda b, i: ((layer * nb_rows + mod_row(b)) * 6 + j, 0, 0))

    def vec_spec():
        return _const_spec((None, 1, D_MODEL), (layer, 0, 0))

    in_specs = [
        pl.BlockSpec((None, tm, D_MODEL), lambda b, i: (b, i, 0)),
        pl.BlockSpec((None, A_OUT, tm), lambda b, i: (b, 0, i)),
        pl.BlockSpec((None, B_OUT, tm), lambda b, i: (b, 0, i)),
        pl.BlockSpec((None, tm, C_W), lambda b, i: (b, i, 0)),
        pl.BlockSpec((None, tm, C_W), lambda b, i: (b, i, 0)),
        pl.BlockSpec((None, tm, C_W), lambda b, i: (b, i, C_GATE)),
        mod_spec(2), mod_spec(3), mod_spec(4), mod_spec(5),
        vec_spec(), vec_spec(), vec_spec(),
        _const_spec((None, 1, C_W), (layer, 0, 0)),
        _const_spec((256, 256), (0, 0)),
        _const_spec((None, A_OUT, D_MODEL), (layer, 0, 0)),
        _const_spec((None, B_OUT, D_MODEL), (layer, 0, 0)),
        _const_spec((None, C_OUT, D_MODEL), (layer, 0, 0)),
        _const_spec((None, D_MODEL, D_FF), (layer, 0, 0)),
        _const_spec((None, D_FF, D_MODEL), (layer, 0, 0)),
    ]
    return pl.pallas_call(
        _post_kernel,
        out_shape=jax.ShapeDtypeStruct((bsz, t, D_MODEL), F32),
        grid=(bsz, t // tm),
        in_specs=in_specs,
        out_specs=pl.BlockSpec((None, tm, D_MODEL), lambda b, i: (b, i, 0)),
        compiler_params=_params(("parallel", "parallel")),
        name="post",
    )(xs, yta, ytb, o_f, o_b, c_slab, mod, mod, mod, mod,
      w["g_post_mix"], w["g_pre_ffn"], w["g_post_ffn"], w["c_out_norm"], w["block_ones"],
      w["w_out_a"], w["w_out_b"], w["w_out_c"], w["w_ff1"], w["w_ff2"])


def _w_in_columns():
    src = np.full((N_COL,), -1, np.int64)
    a_k0 = A_HEADS * HEAD_DIM
    a_v0 = a_k0 + A_KV_HEADS * HEAD_DIM
    b_qd0 = a_v0 + A_KV_HEADS * HEAD_DIM
    b_kv0 = b_qd0 + B_Q_RANK
    b_kr0 = b_kv0 + B_KV_RANK
    c0 = b_kr0 + B_ROPE
    src[OFF_AQ:OFF_AQ + 512] = np.arange(512)
    for g in range(A_KV_HEADS):
        for rep in range(2):
            lo = OFF_AK + (2 * g + rep) * HEAD_DIM
            src[lo:lo + HEAD_DIM] = a_k0 + g * HEAD_DIM + np.arange(HEAD_DIM)
    src[OFF_AV:OFF_AV + 128] = a_v0 + np.arange(128)
    src[OFF_BQD:OFF_BQD + B_Q_RANK] = b_qd0 + np.arange(B_Q_RANK)
    src[OFF_BKVD:OFF_BKVD + B_KV_RANK] = b_kv0 + np.arange(B_KV_RANK)
    src[OFF_KPE + B_NOPE:OFF_KPE + B_NOPE + B_ROPE] = b_kr0 + np.arange(B_ROPE)
    src[OFF_CQ:OFF_CQ + 5 * C_W] = c0 + np.arange(5 * C_W)
    return src


def _gather_cols(w, src):
    keep = jnp.asarray(src >= 0)
    return jnp.where(keep, jnp.take(w, jnp.asarray(np.maximum(src, 0)), axis=-1), 0.0)


def _prepare_weights(p, n_mod_rows):
    depth = p["w_in"].shape[0]
    w = {"n_mod_rows": n_mod_rows}
    w["w_in"] = _gather_cols(p["w_in"], _w_in_columns()).astype(BF16)
    src = np.full((B_HEADS * LANE,), -1, np.int64)
    for hh in range(B_HEADS):
        src[hh * LANE:hh * LANE + B_NOPE + B_ROPE] = hh * (B_NOPE + B_ROPE) + np.arange(B_NOPE + B_ROPE)
    wq = _gather_cols(p["w_q_up"], src)
    w["w_q_up"] = jnp.pad(wq, ((0, 0), (0, 256 - B_Q_RANK), (0, 0))).astype(BF16)
    src = np.full((B_HEADS * LANE,), -1, np.int64)
    srcv = np.zeros((B_HEADS * B_V,), np.int64)
    for hh in range(B_HEADS):
        src[hh * LANE:hh * LANE + B_NOPE] = hh * (B_NOPE + B_V) + np.arange(B_NOPE)
        srcv[hh * B_V:(hh + 1) * B_V] = hh * (B_NOPE + B_V) + B_NOPE + np.arange(B_V)
    w["w_kv_k"] = _gather_cols(p["w_kv_up"], src).astype(BF16)
    w["w_kv_v"] = _gather_cols(p["w_kv_up"], srcv).astype(BF16)
    w["a_q_norm"] = jnp.tile(p["a_q_norm"], (1, 4))[:, None, :]
    w["a_k_norm"] = jnp.tile(p["a_k_norm"], (1, 4))[:, None, :]
    w["b_q_norm"] = jnp.pad(p["b_q_norm"], ((0, 0), (0, 256 - B_Q_RANK)))[:, None, :]
    w["b_kv_norm"] = p["b_kv_norm"][:, None, :]
    w["c_out_norm"] = jnp.tile(p["c_out_norm"], (1, C_HEADS))[:, None, :]
    for name in ("g_pre_mix", "g_post_mix", "g_pre_ffn", "g_post_ffn"):
        w[name] = p[name][:, None, :]
    head = np.arange(256) // HEAD_DIM
    same_head = head[:, None] == head[None, :]
    w["block_ones"] = jnp.asarray(same_head / float(HEAD_DIM), BF16)
    w["block_mask"] = jnp.asarray(same_head, F32)
    p_lb = jax.nn.softmax(p["c_lower_bounds"].astype(F32), axis=0)
    w["lower"] = jnp.cumsum(p_lb, axis=0) - p_lb[:1]
    w["w_out_a"] = p["w_out"][:, :A_OUT].astype(BF16)
    w["w_out_b"] = p["w_out"][:, A_OUT:A_OUT + B_OUT].astype(BF16)
    w["w_out_c"] = p["w_out"][:, A_OUT + B_OUT:].astype(BF16)
    w["w_ff1"] = p["w_ff1"].astype(BF16)
    w["w_ff2"] = p["w_ff2"].astype(BF16)
    del depth
    return w


def _rope_tables(n_tok):
    tok = np.arange(n_tok)
    row = (tok // GRID_W).astype(np.float32)[:, None]
    col = (tok % GRID_W).astype(np.float32)[:, None]

    def angles(rot_dim):
        n_freq = rot_dim // 4
        inv = jnp.asarray(ROPE_THETA, F32) ** (-jnp.arange(n_freq, dtype=F32) / n_freq)
        ang = jnp.concatenate([jnp.asarray(row) * inv, jnp.asarray(col) * inv], axis=-1)
        return jnp.cos(ang), jnp.sin(ang)

    zeros = lambda n: jnp.zeros((n_tok, n), F32)
    ones = lambda n: jnp.ones((n_tok, n), F32)
    cos, sin = angles(HEAD_DIM)
    ca = jnp.concatenate([cos, cos, cos, cos], axis=-1)
    s1a = jnp.concatenate([-sin, zeros(32), -sin, zeros(32)], axis=-1)
    s2a = jnp.concatenate([zeros(32), sin, zeros(32), sin], axis=-1)
    cos, sin = angles(B_ROPE)
    cb = jnp.concatenate([ones(64), cos, cos, ones(32)], axis=-1)
    s1b = jnp.concatenate([zeros(64), -sin, zeros(16), zeros(32)], axis=-1)
    s2b = jnp.concatenate([zeros(64), zeros(16), sin, zeros(32)], axis=-1)
    return ca, s1a, s2a, cb, s1b, s2b


def kernel(x, c, ctx, c_ctx, w_ada, b_ada, g_pre_mix, g_post_mix, g_pre_ffn, g_post_ffn, w_in, a_q_norm, a_k_norm, b_q_norm, w_q_up, b_kv_norm, w_kv_up, c_lower_bounds, c_out_norm, w_out, w_ff1, w_ff2):
    bsz, n_lat, _ = x.shape
    depth = w_in.shape[0]
    n_mod_rows = -(-(bsz + 1) // 8) * 8
    params = dict(w_in=w_in, a_q_norm=a_q_norm, a_k_norm=a_k_norm, b_q_norm=b_q_norm, w_q_up=w_q_up,
                  b_kv_norm=b_kv_norm, w_kv_up=w_kv_up, c_lower_bounds=c_lower_bounds,
                  c_out_norm=c_out_norm, w_out=w_out, w_ff1=w_ff1, w_ff2=w_ff2,
                  g_pre_mix=g_pre_mix, g_post_mix=g_post_mix, g_pre_ffn=g_pre_ffn, g_post_ffn=g_post_ffn)
    w = _prepare_weights(params, n_mod_rows)
    rope = _rope_tables(n_lat)
    scan_f = _scan_constants(False)
    scan_b = _scan_constants(True)

    cvec = jnp.concatenate([c, c_ctx[None, :], jnp.zeros((n_mod_rows - bsz - 1, D_MODEL), F32)], axis=0)
    mod = _ada(cvec, w_ada, b_ada).reshape(depth * n_mod_rows * 6, 1, D_MODEL)

    lat_row = lambda b: b
    ctx_row = lambda b: bsz
    zero_state = jnp.zeros((bsz, C_W, C_W), F32)
    attn_a = functools.partial(_attention, n_kv=A_KV_HEADS, n_heads=A_HEADS // A_KV_HEADS, shared_kv=True)
    attn_b = functools.partial(_attention, n_kv=1, n_heads=B_HEADS, shared_kv=False)

    xc = ctx
    for layer in range(depth):
        need_ctx = layer < depth - 1
        fl = _features(x, mod, w, layer, lat_row, rope)
        fc = _features(xc, mod, w, layer, ctx_row, None)
        yta = attn_a(fl["aq"], [(fl["ak"], fl["avt"]), (fc["ak"], fc["avt"])], name="attn_a")
        ytb = attn_b(fl["bq"], [(fl["bk"], fl["bvt"]), (fc["bk"], fc["bvt"])], name="attn_b")
        ocf, s_f = _hgrn(fc["c"], fc["g"], zero_state, scan_f, w["block_mask"], False)
        ocb, s_b = _hgrn(fc["c"], fc["g"], zero_state, scan_b, w["block_mask"], True)
        olf, _ = _hgrn(fl["c"], fl["g"], s_f, scan_f, w["block_mask"], False)
        olb, _ = _hgrn(fl["c"], fl["g"], s_b, scan_b, w["block_mask"], True)
        x_new = _post(x, yta, ytb, olf, olb, fl["c"], mod, w, layer, lat_row)
        if need_ctx:
            yta_c = attn_a(fc["aq"], [(fc["ak"], fc["avt"])], name="attn_a_ctx")
            ytb_c = attn_b(fc["bq"], [(fc["bk"], fc["bvt"])], name="attn_b_ctx")
            xc = _post(xc, yta_c, ytb_c, ocf, ocb, fc["c"], mod, w, layer, ctx_row)
        x = x_new
    return x
```

```python
import functools

import numpy as np
import jax
import jax.numpy as jnp
from jax import lax
from jax.experimental import pallas as pl
from jax.experimental.pallas import tpu as pltpu

F32 = jnp.float32
BF16 = jnp.bfloat16

D_MODEL = 1024
GRID_W = 64
HEAD_DIM = 64
A_HEADS = 8
A_KV_HEADS = 2
B_HEADS = 4
B_Q_RANK = 192
B_KV_RANK = 128
B_NOPE = 64
B_ROPE = 32
B_V = 64
C_HEADS = 4
C_DK = 64
C_DV = 64
D_FF = 4 * D_MODEL
A_OUT = A_HEADS * HEAD_DIM
B_OUT = B_HEADS * B_V
C_OUT = C_HEADS * C_DV
C_W = C_HEADS * C_DK
ROPE_THETA = 10000.0
EPS = 1e-6
F_TINY = 1e-30

LANE = 128
VMEM_LIMIT = 56 * 1024 * 1024

OFF_AQ = 0
OFF_AK = 512
OFF_AV = 768
OFF_BQD = 896
OFF_BKVD = 1152
OFF_KPE = 1280
OFF_CQ = 1408
OFF_CFF = 1664
OFF_CFB = 1920
OFF_CI = 2176
OFF_CG = 2432
N_COL = 2688

C_Q, C_KF, C_KB, C_V, C_GATE = range(5)
N_SLAB = 5

SCAN_C = 64
N_LEVELS = 6
SCAN_BLOCK = 512


def _dot(a, b):
    return jnp.dot(a, b, preferred_element_type=F32)


def _dot_nt(a, b):
    return lax.dot_general(a, b, (((1,), (1,)), ((), ())), preferred_element_type=F32)


def _dot_tn(a, b):
    return lax.dot_general(a, b, (((0,), (0,)), ((), ())), preferred_element_type=F32)


def _sigmoid_pair(z):
    e = jnp.exp(-jnp.abs(z))
    inv = 1.0 / (1.0 + e)
    small = e * inv
    pos = z >= 0
    return jnp.where(pos, inv, small), jnp.where(pos, small, inv)


def _silu(z):
    s, _ = _sigmoid_pair(z)
    return z * s


def _const_spec(shape, index):
    return pl.BlockSpec(shape, lambda *_: index, pipeline_mode=pl.Buffered(1))


def _params(sem):
    return pltpu.CompilerParams(dimension_semantics=sem, vmem_limit_bytes=VMEM_LIMIT)


def _ada_kernel(c_ref, w_ref, b_ref, o_ref):
    a = _silu(c_ref[...])
    w = w_ref[...]
    a_hi = a.astype(BF16)
    a_lo = (a - a_hi.astype(F32)).astype(BF16)
    w_hi = w.astype(BF16)
    w_lo = (w - w_hi.astype(F32)).astype(BF16)
    acc = _dot(a_hi, w_hi) + (_dot(a_hi, w_lo) + _dot(a_lo, w_hi))
    o_ref[...] = acc + b_ref[...]


def _ada(cvec, w_ada, b_ada):
    depth = w_ada.shape[0]
    rows = cvec.shape[0]
    n_blk = w_ada.shape[2] // D_MODEL
    return pl.pallas_call(
        _ada_kernel,
        out_shape=jax.ShapeDtypeStruct((depth, rows, n_blk * D_MODEL), F32),
        grid=(depth, n_blk),
        in_specs=[
            pl.BlockSpec((rows, D_MODEL), lambda l, j: (0, 0)),
            pl.BlockSpec((None, D_MODEL, D_MODEL), lambda l, j: (l, 0, j)),
            pl.BlockSpec((None, None, 1, D_MODEL), lambda l, j: (l, j, 0, 0)),
        ],
        out_specs=pl.BlockSpec((None, rows, D_MODEL), lambda l, j: (l, 0, j)),
        compiler_params=_params(("arbitrary", "arbitrary")),
        name="ada",
    )(cvec, w_ada, b_ada.reshape(depth, n_blk, 1, D_MODEL))


def _rope128(v, cos, s1, s2, half):
    up = pltpu.roll(v, LANE - half, 1)
    dn = pltpu.roll(v, half, 1)
    return v * cos + up * s1 + dn * s2


def _head_rms(v, bo, gain):
    sq = v * v
    hi = sq.astype(BF16)
    lo = (sq - hi.astype(F32)).astype(BF16)
    ms = _dot(hi, bo) + _dot(lo, bo)
    return v * lax.rsqrt(ms + EPS) * gain


FEAT_SUB = 256


def _feat_kernel(*refs, use_rope):
    (x_ref, sh_ref, sc_ref, gpre_ref, win_ref, wq_ref, wkk_ref, wkv_ref,
     aqn_ref, akn_ref, bqn_ref, bkvn_ref, bo_ref, lb_ref) = refs[:14]
    rest = refs[14:]
    if use_rope:
        ca_ref, s1a_ref, s2a_ref, cb_ref, s1b_ref, s2b_ref = rest[:6]
        rest = rest[6:]
    aq_o, ak_o, avt_o, bq_o, bk_o, bvt_o, c_o, g_o = rest
    bo = bo_ref[...]
    tm = x_ref.shape[0]
    sub = min(tm, FEAT_SUB)

    for r0 in range(0, tm, sub):
        rs = slice(r0, r0 + sub)

        def rope_a(v, rs=rs):
            if not use_rope:
                return v
            return _rope128(v, ca_ref[rs, :], s1a_ref[rs, :], s2a_ref[rs, :], HEAD_DIM // 2)

        def rope_b(v, rs=rs):
            if not use_rope:
                return v
            return _rope128(v, cb_ref[rs, :], s1b_ref[rs, :], s2b_ref[rs, :], B_ROPE // 2)

        x = x_ref[rs, :]
        ms = jnp.mean(x * x, axis=-1, keepdims=True)
        h = x * lax.rsqrt(ms + EPS) * gpre_ref[...]
        h = h * (1.0 + sc_ref[...]) + sh_ref[...]
        p = _dot(h.astype(BF16), win_ref[...])

        aqn = aqn_ref[...]
        for half in range(2):
            v = _head_rms(p[:, OFF_AQ + 256 * half:OFF_AQ + 256 * (half + 1)], bo, aqn)
            for s in range(2):
                blk = rope_a(v[:, LANE * s:LANE * (s + 1)]) * (HEAD_DIM ** -0.5 * LOG2_E)
                lo = 256 * half + LANE * s
                aq_o[lo:lo + LANE, rs] = blk.T.astype(BF16)
        v = _head_rms(p[:, OFF_AK:OFF_AK + 256], bo, akn_ref[...])
        for s in range(2):
            ak_o[rs, LANE * s:LANE * (s + 1)] = rope_a(v[:, LANE * s:LANE * (s + 1)]).astype(BF16)
        avt_o[:, rs] = p[:, OFF_AV:OFF_AV + LANE].T.astype(BF16)

        bqd = p[:, OFF_BQD:OFF_BQD + 256]
        ms = jnp.sum(bqd * bqd, axis=-1, keepdims=True) * (1.0 / B_Q_RANK)
        qn = (bqd * lax.rsqrt(ms + EPS) * bqn_ref[...]).astype(BF16)
        bq = _dot(qn, wq_ref[...])
        bkvd = p[:, OFF_BKVD:OFF_BKVD + LANE]
        ms = jnp.mean(bkvd * bkvd, axis=-1, keepdims=True)
        kvn = (bkvd * lax.rsqrt(ms + EPS) * bkvn_ref[...]).astype(BF16)
        bkn = _dot(kvn, wkk_ref[...])
        bv = _dot(kvn, wkv_ref[...])
        kpe = rope_b(p[:, OFF_KPE:OFF_KPE + LANE])
        b_scale = (B_NOPE + B_ROPE) ** -0.5 * LOG2_E
        for hh in range(B_HEADS):
            sl = slice(LANE * hh, LANE * (hh + 1))
            bq_o[sl, rs] = (rope_b(bq[:, sl]) * b_scale).T.astype(BF16)
            bk_o[rs, sl] = (bkn[:, sl] + kpe).astype(BF16)
        bvt_o[:, rs] = bv.T.astype(BF16)

        c_o[rs, C_W * C_Q:C_W * (C_Q + 1)] = _silu(p[:, OFF_CQ:OFF_CQ + C_W])
        for d, (off, ck) in enumerate(((OFF_CFF, C_KF), (OFF_CFB, C_KB))):
            lb = lb_ref[d:d + 1, :]
            sp, sn = _sigmoid_pair(p[:, off:off + C_W])
            f = lb + (1.0 - lb) * sp
            g2 = jnp.log2(jnp.maximum(f, F_TINY))
            hi = g2.astype(BF16)
            g_o[rs, C_W * 2 * d:C_W * (2 * d + 1)] = hi
            g_o[rs, C_W * (2 * d + 1):C_W * (2 * d + 2)] = (g2 - hi.astype(F32)).astype(BF16)
            c_o[rs, C_W * ck:C_W * (ck + 1)] = (1.0 - lb) * sn
        c_o[rs, C_W * C_V:C_W * (C_V + 1)] = p[:, OFF_CI:OFF_CI + C_W]
        c_o[rs, C_W * C_GATE:C_W * (C_GATE + 1)] = p[:, OFF_CG:OFF_CG + C_W]


def _features(xs, mod, w, layer, mod_row, rope):
    bsz, t, _ = xs.shape
    tm = min(512, t)
    nt = t // tm
    nb_rows = w["n_mod_rows"]

    def mod_spec(j):
        return pl.BlockSpec((None, 1, D_MODEL),
                            lambda b, i: ((layer * nb_rows + mod_row(b)) * 6 + j, 0, 0))

    in_specs = [
        pl.BlockSpec((None, tm, D_MODEL), lambda b, i: (b, i, 0)),
        mod_spec(0), mod_spec(1),
        _const_spec((None, 1, D_MODEL), (layer, 0, 0)),
        _const_spec((None, D_MODEL, N_COL), (layer, 0, 0)),
        _const_spec((None, 256, 512), (layer, 0, 0)),
        _const_spec((None, LANE, 512), (layer, 0, 0)),
        _const_spec((None, LANE, 256), (layer, 0, 0)),
        _const_spec((None, 1, 256), (layer, 0, 0)),
        _const_spec((None, 1, 256), (layer, 0, 0)),
        _const_spec((None, 1, 256), (layer, 0, 0)),
        _const_spec((None, 1, LANE), (layer, 0, 0)),
        _const_spec((256, 256), (0, 0)),
        _const_spec((None, 2, C_W), (layer, 0, 0)),
    ]
    args = [xs, mod, mod, w["g_pre_mix"], w["w_in"], w["w_q_up"], w["w_kv_k"], w["w_kv_v"],
            w["a_q_norm"], w["a_k_norm"], w["b_q_norm"], w["b_kv_norm"], w["block_ones"], w["lower"]]
    if rope is not None:
        in_specs += [pl.BlockSpec((tm, LANE), lambda b, i: (i, 0))] * 6
        args += list(rope)
    out_shape = [
        jax.ShapeDtypeStruct((bsz, 512, t), BF16),
        jax.ShapeDtypeStruct((bsz, t, 256), BF16),
        jax.ShapeDtypeStruct((bsz, nt, LANE, tm), BF16),
        jax.ShapeDtypeStruct((bsz, 512, t), BF16),
        jax.ShapeDtypeStruct((bsz, t, 512), BF16),
        jax.ShapeDtypeStruct((bsz, nt, 256, tm), BF16),
        jax.ShapeDtypeStruct((bsz, t, N_SLAB * C_W), F32),
        jax.ShapeDtypeStruct((bsz, t, 4 * C_W), BF16),
    ]
    out_specs = [
        pl.BlockSpec((None, 512, tm), lambda b, i: (b, 0, i)),
        pl.BlockSpec((None, tm, 256), lambda b, i: (b, i, 0)),
        pl.BlockSpec((None, None, LANE, tm), lambda b, i: (b, i, 0, 0)),
        pl.BlockSpec((None, 512, tm), lambda b, i: (b, 0, i)),
        pl.BlockSpec((None, tm, 512), lambda b, i: (b, i, 0)),
        pl.BlockSpec((None, None, 256, tm), lambda b, i: (b, i, 0, 0)),
        pl.BlockSpec((None, tm, N_SLAB * C_W), lambda b, i: (b, i, 0)),
        pl.BlockSpec((None, tm, 4 * C_W), lambda b, i: (b, i, 0)),
    ]
    outs = pl.pallas_call(
        functools.partial(_feat_kernel, use_rope=rope is not None),
        out_shape=out_shape, grid=(bsz, nt), in_specs=in_specs, out_specs=out_specs,
        compiler_params=_params(("parallel", "parallel")),
        name="feat_rope" if rope is not None else "feat_ctx",
    )(*args)
    return dict(zip(("aq", "ak", "avt", "bq", "bk", "bvt", "c", "g"), outs))


ACC_ROWS = B_V + 16
NEG_BIG = -1e30
LOG2_E = 1.4426950408889634
ATTN_TQ = 1024


def _attn_kernel(*refs, n_src, n_heads, shared_kv):
    q_ref = refs[0]
    src_refs = refs[1:1 + 2 * n_src]
    o_ref = refs[1 + 2 * n_src]
    rhs_scr, m_scr, acc_scr, st_scr, mx_scr = refs[2 + 2 * n_src:]
    tq = q_ref.shape[1]
    n_grp, k_dim, grp_w = rhs_scr.shape
    heads_per_grp = n_heads // n_grp

    if shared_kv:
        for h in range(n_heads):
            rhs_scr[0, :, tq * h:tq * (h + 1)] = q_ref[HEAD_DIM * h:HEAD_DIM * (h + 1), :]
    else:
        rhs_scr[...] = jnp.zeros(rhs_scr.shape, BF16)
        for h in range(n_heads):
            g, j = divmod(h, heads_per_grp)
            rhs_scr[g, LANE * j:LANE * (j + 1), tq * j:tq * (j + 1)] = q_ref[LANE * h:LANE * (h + 1), :]
    m_scr[...] = jnp.full(m_scr.shape, NEG_BIG, F32)
    acc_scr[...] = jnp.zeros(acc_scr.shape, F32)

    def stage(s, c, slot):
        k_ref, vt_ref = src_refs[2 * s], src_refs[2 * s + 1]
        tk = vt_ref.shape[2]
        rows = pl.ds(pl.multiple_of(c * tk, tk), tk)
        for g in range(n_grp):
            cols = slice(grp_w * g, grp_w * (g + 1))
            kc = k_ref[rows, 0:k_dim] if shared_kv else k_ref[rows, k_dim * g:k_dim * (g + 1)]
            st = _dot(kc, rhs_scr[g])
            st_scr[slot, 0:tk, cols] = st
            mx_scr[slot, :, cols] = jnp.max(st, axis=0, keepdims=True)

    def consume(s, c, slot):
        vt_ref = src_refs[2 * s + 1]
        tk = vt_ref.shape[2]
        ones = jnp.ones((ACC_ROWS - B_V, tk), BF16)
        for g in range(n_grp):
            cols = slice(grp_w * g, grp_w * (g + 1))
            m_old = m_scr[:, cols]
            m_new = jnp.maximum(m_old, mx_scr[slot, :, cols])
            pt = jnp.exp2(st_scr[slot, 0:tk, cols] - m_new).astype(BF16)
            alpha = jnp.exp2(m_old - m_new)
            m_scr[:, cols] = m_new
            if shared_kv:
                vt = jnp.concatenate([vt_ref[c], ones], axis=0)
                acc_scr[:, cols] = alpha * acc_scr[:, cols] + _dot(vt, pt)
            else:
                for j in range(heads_per_grp):
                    h = g * heads_per_grp + j
                    hc = slice(tq * h, tq * (h + 1))
                    lc = slice(tq * j, tq * (j + 1))
                    vt = jnp.concatenate([vt_ref[c, B_V * h:B_V * (h + 1), :], ones], axis=0)
                    acc_scr[:, hc] = alpha[:, lc] * acc_scr[:, hc] + _dot(vt, pt[:, lc])

    n0 = src_refs[1].shape[0]
    n_loop = (n0 - 2) // 2 if n0 >= 4 else 0
    stage(0, 0, 0)
    if n_loop:
        def pair(i, carry):
            stage(0, 2 * i + 1, 1)
            consume(0, 2 * i, 0)
            stage(0, 2 * i + 2, 0)
            consume(0, 2 * i + 1, 1)
            return carry
        lax.fori_loop(0, n_loop, pair, 0)
    tail = [(0, c) for c in range(2 * n_loop, n0)]
    tail += [(s, c) for s in range(1, n_src) for c in range(src_refs[2 * s + 1].shape[0])]
    for i, (s, c) in enumerate(tail):
        if i + 1 < len(tail):
            stage(*tail[i + 1], (i + 1) % 2)
        consume(s, c, i % 2)

    for h in range(n_heads):
        acc = acc_scr[:, tq * h:tq * (h + 1)]
        o_ref[B_V * h:B_V * (h + 1), :] = (acc[:B_V] * (1.0 / acc[B_V:B_V + 1])).astype(BF16)


def _attention(q, srcs, *, n_kv, n_heads, shared_kv, name):
    bsz, _, t = q.shape
    tq = min(ATTN_TQ, t)
    q_r = (HEAD_DIM if shared_kv else LANE) * n_heads
    k_w = LANE * (1 if shared_kv else n_heads)
    v_r = B_V * (1 if shared_kv else n_heads)
    rhs_shape = (1, HEAD_DIM, n_heads * tq) if shared_kv else (n_heads // 2, 2 * LANE, 2 * tq)
    in_specs = [pl.BlockSpec((None, q_r, tq), lambda b, g, i: (b, g, i))]
    args = [q]
    for k, vt in srcs:
        length = k.shape[1]
        n_chunk, tk = vt.shape[1], vt.shape[3]
        in_specs.append(pl.BlockSpec((None, length, k_w), lambda b, g, i: (b, 0, g)))
        in_specs.append(pl.BlockSpec((None, n_chunk, v_r, tk), lambda b, g, i: (b, 0, g, 0)))
        args += [k, vt]
    rows = B_V * n_heads
    return pl.pallas_call(
        functools.partial(_attn_kernel, n_src=len(srcs), n_heads=n_heads, shared_kv=shared_kv),
        out_shape=jax.ShapeDtypeStruct((bsz, rows * n_kv, t), BF16),
        grid=(bsz, n_kv, t // tq),
        in_specs=in_specs,
        out_specs=pl.BlockSpec((None, rows, tq), lambda b, g, i: (b, g, i)),
        scratch_shapes=[pltpu.VMEM(rhs_shape, BF16),
                        pltpu.VMEM((1, n_heads * tq), F32),
                        pltpu.VMEM((ACC_ROWS, n_heads * tq), F32),
                        pltpu.VMEM((2, max(vt.shape[3] for _, vt in srcs), n_heads * tq), F32),
                        pltpu.VMEM((2, 1, n_heads * tq), F32)],
        compiler_params=_params(("parallel", "parallel", "parallel")),
        name=name,
    )(*args)


def _scan_constants(reverse):
    c = SCAN_C
    t = np.arange(c)[:, None]
    s = np.arange(c)[None, :]
    cum = (s >= t) if reverse else (s <= t)
    msk = np.zeros((N_LEVELS + 1, c, c), np.float32)
    for lvl in range(N_LEVELS):
        w = (c // 2) >> lvl
        same = (t // (2 * w)) == (s // (2 * w))
        t_hi = (t % (2 * w)) >= w
        s_hi = (s % (2 * w)) >= w
        msk[lvl] = (same & ~t_hi & s_hi) if reverse else (same & t_hi & ~s_hi)
    msk[N_LEVELS] = (t == s)
    return jnp.asarray(cum, BF16), jnp.asarray(np.tile(msk, (1, 1, C_HEADS)), F32)


def _boundary_rows(b, lvl, reverse):
    w = (SCAN_C // 2) >> lvl
    width = b.shape[1]
    off = w if reverse else w - 1
    if w == 1:
        odd = lax.broadcasted_iota(jnp.int32, b.shape, 0) % 2 == 1
        if reverse:
            return jnp.where(odd, b, pltpu.roll(b, SCAN_C - 1, 0))
        return jnp.where(odd, pltpu.roll(b, 1, 0), b)
    if w == 2:
        low = lax.broadcasted_iota(jnp.int32, (8, width), 0) < 4
        pieces = []
        for r0 in range(0, SCAN_C, 8):
            first = jnp.broadcast_to(b[r0 + off:r0 + off + 1, :], (8, width))
            second = jnp.broadcast_to(b[r0 + 4 + off:r0 + 5 + off, :], (8, width))
            pieces.append(jnp.where(low, first, second))
        return jnp.concatenate(pieces, axis=0)
    pieces = [jnp.broadcast_to(b[r0 + off:r0 + off + 1, :], (2 * w, width))
              for r0 in range(0, SCAN_C, 2 * w)]
    return pieces[0] if len(pieces) == 1 else jnp.concatenate(pieces, axis=0)


def _hgrn_kernel(q_ref, k_ref, ghi_ref, glo_ref, v_ref, s0_ref, cum_ref, msk_ref, bm_ref,
                 o_ref, sfin_ref, st_ref, *, reverse, n_chunk):
    i = pl.program_id(1)

    @pl.when(i == 0)
    def _():
        st_ref[...] = s0_ref[...]

    bm = bm_ref[...]
    bm16 = bm.astype(BF16)
    cum = cum_ref[...]

    def stack_heads(a):
        a16 = a.astype(BF16)
        return jnp.concatenate([a16] * C_HEADS, axis=0) * bm16

    order = list(range(n_chunk - 1, -1, -1)) if reverse else list(range(n_chunk))
    rows = [slice(c * SCAN_C, (c + 1) * SCAN_C) for c in range(n_chunk)]
    q = [q_ref[r, :] for r in rows]
    k = [k_ref[r, :] for r in rows]
    v = [v_ref[r, :] for r in rows]
    b = [_dot(cum, ghi_ref[r, :]) + _dot(cum, glo_ref[r, :]) for r in rows]
    b_all = [bc[0:1, :] if reverse else bc[SCAN_C - 1:SCAN_C, :] for bc in b]

    q16 = [qc.astype(BF16) for qc in q]
    kst = [stack_heads(kc) for kc in k]
    sc = [msk_ref[N_LEVELS] * _dot_nt(q16[c], kst[c]) for c in range(n_chunk)]
    for lvl in range(N_LEVELS):
        for c in range(n_chunk):
            d = b[c] - _boundary_rows(b[c], lvl, reverse)
            e16 = jnp.exp2(-jnp.abs(d)).astype(BF16)
            kl = kst[c] * jnp.concatenate([e16] * C_HEADS, axis=0)
            sc[c] = sc[c] + msk_ref[lvl] * _dot_nt(q16[c] * e16, kl)

    ds = [_dot_tn(v[c].astype(BF16), (k[c] * jnp.exp2(b_all[c] - b[c])).astype(BF16)) * bm
          for c in range(n_chunk)]
    st = st_ref[...]
    st_in = [None] * n_chunk
    for c in order:
        st_in[c] = st
        st = st * jnp.exp2(b_all[c]) + ds[c]
    st_ref[...] = st

    for c in order:
        qdec = q[c] * jnp.exp2(b[c])
        o_ref[rows[c], :] = (_dot(sc[c].astype(BF16), stack_heads(v[c]))
                             + _dot_nt(qdec.astype(BF16), st_in[c].astype(BF16)))

    @pl.when(i == pl.num_programs(1) - 1)
    def _():
        sfin_ref[...] = st_ref[...]


def _hgrn(c_slab, g16, s0, consts, block_mask, reverse):
    bsz, t, _ = c_slab.shape
    tb = min(SCAN_BLOCK, t)
    nblk = t // tb
    cum, msk = consts

    def blk(i):
        return (nblk - 1 - i) if reverse else i

    def slab_spec(j):
        return pl.BlockSpec((None, tb, C_W), lambda b, i: (b, blk(i), j))

    g0 = 2 if reverse else 0

    return pl.pallas_call(
        functools.partial(_hgrn_kernel, reverse=reverse, n_chunk=tb // SCAN_C),
        out_shape=[jax.ShapeDtypeStruct((bsz, t, C_W), F32),
                   jax.ShapeDtypeStruct((bsz, C_W, C_W), F32)],
        grid=(bsz, nblk),
        in_specs=[
            slab_spec(C_Q), slab_spec(C_KB if reverse else C_KF), slab_spec(g0), slab_spec(g0 + 1),
            slab_spec(C_V),
            pl.BlockSpec((None, C_W, C_W), lambda b, i: (b, 0, 0)),
            _const_spec(cum.shape, (0, 0)),
            _const_spec(msk.shape, (0, 0, 0)),
            _const_spec(block_mask.shape, (0, 0)),
        ],
        out_specs=[pl.BlockSpec((None, tb, C_W), lambda b, i: (b, blk(i), 0)),
                   pl.BlockSpec((None, C_W, C_W), lambda b, i: (b, 0, 0))],
        scratch_shapes=[pltpu.VMEM((C_W, C_W), F32)],
        compiler_params=_params(("parallel", "arbitrary")),
        name="hgrn_bwd" if reverse else "hgrn_fwd",
    )(c_slab, c_slab, g16, g16, c_slab, s0, cum, msk, block_mask)


POST_SUB = 256


def _rms(v, gain):
    ms = jnp.mean(v * v, axis=-1, keepdims=True)
    return v * lax.rsqrt(ms + EPS) * gain


def _post_kernel(x_ref, ya_ref, yb_ref, of_ref, ob_ref, gate_ref,
                 gtm_ref, shf_ref, scf_ref, gtf_ref,
                 gpm_ref, gpf_ref, gqf_ref, con_ref, bo_ref,
                 woa_ref, wob_ref, woc_ref, w1_ref, w2_ref, o_ref):
    tm = x_ref.shape[0]
    sub = min(tm, POST_SUB)
    for r0 in range(0, tm, sub):
        rs = slice(r0, r0 + sub)
        o = of_ref[rs, :] + ob_ref[rs, :]
        yc = (_head_rms(o, bo_ref[...], con_ref[...]) * _silu(gate_ref[rs, :])).astype(BF16)
        mix = (_dot_tn(ya_ref[:, rs], woa_ref[...]) + _dot_tn(yb_ref[:, rs], wob_ref[...])
               + _dot(yc, woc_ref[...]))
        x1 = x_ref[rs, :] + gtm_ref[...] * _rms(mix, gpm_ref[...])
        h = _rms(x1, gpf_ref[...]) * (1.0 + scf_ref[...]) + shf_ref[...]
        u = jnp.maximum(_dot(h.astype(BF16), w1_ref[...]), 0.0)
        ff = _dot((u * u).astype(BF16), w2_ref[...])
        o_ref[rs, :] = x1 + gtf_ref[...] * _rms(ff, gqf_ref[...])


def _post(xs, yta, ytb, o_f, o_b, c_slab, mod, w, layer, mod_row):
    bsz, t, _ = xs.shape
    tm = min(512, t)
    nb_rows = w["n_mod_rows"]

    def mod_spec(j):
        return pl.BlockSpec((None, 1, D_MODEL),
                            lambda b, i: ((layer * nb_rows + mod_row(b)) * 6 + j, 0, 0))

    def vec_spec():
        return _const_spec((None, 1, D_MODEL), (layer, 0, 0))

    in_specs = [
        pl.BlockSpec((None, tm, D_MODEL), lambda b, i: (b, i, 0)),
        pl.BlockSpec((None, A_OUT, tm), lambda b, i: (b, 0, i)),
        pl.BlockSpec((None, B_OUT, tm), lambda b, i: (b, 0, i)),
        pl.BlockSpec((None, tm, C_W), lambda b, i: (b, i, 0)),
        pl.BlockSpec((None, tm, C_W), lambda b, i: (b, i, 0)),
        pl.BlockSpec((None, tm, C_W), lambda b, i: (b, i, C_GATE)),
        mod_spec(2), mod_spec(3), mod_spec(4), mod_spec(5),
        vec_spec(), vec_spec(), vec_spec(),
        _const_spec((None, 1, C_W), (layer, 0, 0)),
        _const_spec((256, 256), (0, 0)),
        _const_spec((None, A_OUT, D_MODEL), (layer, 0, 0)),
        _const_spec((None, B_OUT, D_MODEL), (layer, 0, 0)),
        _const_spec((None, C_OUT, D_MODEL), (layer, 0, 0)),
        _const_spec((None, D_MODEL, D_FF), (layer, 0, 0)),
        _const_spec((None, D_FF, D_MODEL), (layer, 0, 0)),
    ]
    return pl.pallas_call(
        _post_kernel,
        out_shape=jax.ShapeDtypeStruct((bsz, t, D_MODEL), F32),
        grid=(bsz, t // tm),
        in_specs=in_specs,
        out_specs=pl.BlockSpec((None, tm, D_MODEL), lambda b, i: (b, i, 0)),
        compiler_params=_params(("parallel", "parallel")),
        name="post",
    )(xs, yta, ytb, o_f, o_b, c_slab, mod, mod, mod, mod,
      w["g_post_mix"], w["g_pre_ffn"], w["g_post_ffn"], w["c_out_norm"], w["block_ones"],
      w["w_out_a"], w["w_out_b"], w["w_out_c"], w["w_ff1"], w["w_ff2"])


def _w_in_columns():
    src = np.full((N_COL,), -1, np.int64)
    a_k0 = A_HEADS * HEAD_DIM
    a_v0 = a_k0 + A_KV_HEADS * HEAD_DIM
    b_qd0 = a_v0 + A_KV_HEADS * HEAD_DIM
    b_kv0 = b_qd0 + B_Q_RANK
    b_kr0 = b_kv0 + B_KV_RANK
    c0 = b_kr0 + B_ROPE
    src[OFF_AQ:OFF_AQ + 512] = np.arange(512)
    for g in range(A_KV_HEADS):
        for rep in range(2):
            lo = OFF_AK + (2 * g + rep) * HEAD_DIM
            src[lo:lo + HEAD_DIM] = a_k0 + g * HEAD_DIM + np.arange(HEAD_DIM)
    src[OFF_AV:OFF_AV + 128] = a_v0 + np.arange(128)
    src[OFF_BQD:OFF_BQD + B_Q_RANK] = b_qd0 + np.arange(B_Q_RANK)
    src[OFF_BKVD:OFF_BKVD + B_KV_RANK] = b_kv0 + np.arange(B_KV_RANK)
    src[OFF_KPE + B_NOPE:OFF_KPE + B_NOPE + B_ROPE] = b_kr0 + np.arange(B_ROPE)
    src[OFF_CQ:OFF_CQ + 5 * C_W] = c0 + np.arange(5 * C_W)
    return src


def _gather_cols(w, src):
    keep = jnp.asarray(src >= 0)
    return jnp.where(keep, jnp.take(w, jnp.asarray(np.maximum(src, 0)), axis=-1), 0.0)


def _prepare_weights(p, n_mod_rows):
    depth = p["w_in"].shape[0]
    w = {"n_mod_rows": n_mod_rows}
    w["w_in"] = _gather_cols(p["w_in"], _w_in_columns()).astype(BF16)
    src = np.full((B_HEADS * LANE,), -1, np.int64)
    for hh in range(B_HEADS):
        src[hh * LANE:hh * LANE + B_NOPE + B_ROPE] = hh * (B_NOPE + B_ROPE) + np.arange(B_NOPE + B_ROPE)
    wq = _gather_cols(p["w_q_up"], src)
    w["w_q_up"] = jnp.pad(wq, ((0, 0), (0, 256 - B_Q_RANK), (0, 0))).astype(BF16)
    src = np.full((B_HEADS * LANE,), -1, np.int64)
    srcv = np.zeros((B_HEADS * B_V,), np.int64)
    for hh in range(B_HEADS):
        src[hh * LANE:hh * LANE + B_NOPE] = hh * (B_NOPE + B_V) + np.arange(B_NOPE)
        srcv[hh * B_V:(hh + 1) * B_V] = hh * (B_NOPE + B_V) + B_NOPE + np.arange(B_V)
    w["w_kv_k"] = _gather_cols(p["w_kv_up"], src).astype(BF16)
    w["w_kv_v"] = _gather_cols(p["w_kv_up"], srcv).astype(BF16)
    w["a_q_norm"] = jnp.tile(p["a_q_norm"], (1, 4))[:, None, :]
    w["a_k_norm"] = jnp.tile(p["a_k_norm"], (1, 4))[:, None, :]
    w["b_q_norm"] = jnp.pad(p["b_q_norm"], ((0, 0), (0, 256 - B_Q_RANK)))[:, None, :]
    w["b_kv_norm"] = p["b_kv_norm"][:, None, :]
    w["c_out_norm"] = jnp.tile(p["c_out_norm"], (1, C_HEADS))[:, None, :]
    for name in ("g_pre_mix", "g_post_mix", "g_pre_ffn", "g_post_ffn"):
        w[name] = p[name][:, None, :]
    head = np.arange(256) // HEAD_DIM
    same_head = head[:, None] == head[None, :]
    w["block_ones"] = jnp.asarray(same_head / float(HEAD_DIM), BF16)
    w["block_mask"] = jnp.asarray(same_head, F32)
    p_lb = jax.nn.softmax(p["c_lower_bounds"].astype(F32), axis=0)
    w["lower"] = jnp.cumsum(p_lb, axis=0) - p_lb[:1]
    w["w_out_a"] = p["w_out"][:, :A_OUT].astype(BF16)
    w["w_out_b"] = p["w_out"][:, A_OUT:A_OUT + B_OUT].astype(BF16)
    w["w_out_c"] = p["w_out"][:, A_OUT + B_OUT:].astype(BF16)
    w["w_ff1"] = p["w_ff1"].astype(BF16)
    w["w_ff2"] = p["w_ff2"].astype(BF16)
    del depth
    return w


def _rope_tables(n_tok):
    tok = np.arange(n_tok)
    row = (tok // GRID_W).astype(np.float32)[:, None]
    col = (tok % GRID_W).astype(np.float32)[:, None]

    def angles(rot_dim):
        n_freq = rot_dim // 4
        inv = jnp.asarray(ROPE_THETA, F32) ** (-jnp.arange(n_freq, dtype=F32) / n_freq)
        ang = jnp.concatenate([jnp.asarray(row) * inv, jnp.asarray(col) * inv], axis=-1)
        return jnp.cos(ang), jnp.sin(ang)

    zeros = lambda n: jnp.zeros((n_tok, n), F32)
    ones = lambda n: jnp.ones((n_tok, n), F32)
    cos, sin = angles(HEAD_DIM)
    ca = jnp.concatenate([cos, cos, cos, cos], axis=-1)
    s1a = jnp.concatenate([-sin, zeros(32), -sin, zeros(32)], axis=-1)
    s2a = jnp.concatenate([zeros(32), sin, zeros(32), sin], axis=-1)
    cos, sin = angles(B_ROPE)
    cb = jnp.concatenate([ones(64), cos, cos, ones(32)], axis=-1)
    s1b = jnp.concatenate([zeros(64), -sin, zeros(16), zeros(32)], axis=-1)
    s2b = jnp.concatenate([zeros(64), zeros(16), sin, zeros(32)], axis=-1)
    return ca, s1a, s2a, cb, s1b, s2b


def kernel(x, c, ctx, c_ctx, w_ada, b_ada, g_pre_mix, g_post_mix, g_pre_ffn, g_post_ffn, w_in, a_q_norm, a_k_norm, b_q_norm, w_q_up, b_kv_norm, w_kv_up, c_lower_bounds, c_out_norm, w_out, w_ff1, w_ff2):
    bsz, n_lat, _ = x.shape
    depth = w_in.shape[0]
    n_mod_rows = -(-(bsz + 1) // 8) * 8
    params = dict(w_in=w_in, a_q_norm=a_q_norm, a_k_norm=a_k_norm, b_q_norm=b_q_norm, w_q_up=w_q_up,
                  b_kv_norm=b_kv_norm, w_kv_up=w_kv_up, c_lower_bounds=c_lower_bounds,
                  c_out_norm=c_out_norm, w_out=w_out, w_ff1=w_ff1, w_ff2=w_ff2,
                  g_pre_mix=g_pre_mix, g_post_mix=g_post_mix, g_pre_ffn=g_pre_ffn, g_post_ffn=g_post_ffn)
    w = _prepare_weights(params, n_mod_rows)
    rope = _rope_tables(n_lat)
    scan_f = _scan_constants(False)
    scan_b = _scan_constants(True)

    cvec = jnp.concatenate([c, c_ctx[None, :], jnp.zeros((n_mod_rows - bsz - 1, D_MODEL), F32)], axis=0)
    mod = _ada(cvec, w_ada, b_ada).reshape(depth * n_mod_rows * 6, 1, D_MODEL)

    lat_row = lambda b: b
    ctx_row = lambda b: bsz
    zero_state = jnp.zeros((bsz, C_W, C_W), F32)
    attn_a = functools.partial(_attention, n_kv=A_KV_HEADS, n_heads=A_HEADS // A_KV_HEADS, shared_kv=True)
    attn_b = functools.partial(_attention, n_kv=1, n_heads=B_HEADS, shared_kv=False)

    xc = ctx
    for layer in range(depth):
        need_ctx = layer < depth - 1
        fl = _features(x, mod, w, layer, lat_row, rope)
        fc = _features(xc, mod, w, layer, ctx_row, None)
        yta = attn_a(fl["aq"], [(fl["ak"], fl["avt"]), (fc["ak"], fc["avt"])], name="attn_a")
        ytb = attn_b(fl["bq"], [(fl["bk"], fl["bvt"]), (fc["bk"], fc["bvt"])], name="attn_b")
        ocf, s_f = _hgrn(fc["c"], fc["g"], zero_state, scan_f, w["block_mask"], False)
        ocb, s_b = _hgrn(fc["c"], fc["g"], zero_state, scan_b, w["block_mask"], True)
        olf, _ = _hgrn(fl["c"], fl["g"], s_f, scan_f, w["block_mask"], False)
        olb, _ = _hgrn(fl["c"], fl["g"], s_b, scan_b, w["block_mask"], True)
        x_new = _post(x, yta, ytb, olf, olb, fl["c"], mod, w, layer, lat_row)
        if need_ctx:
            yta_c = attn_a(fc["aq"], [(fc["ak"], fc["avt"])], name="attn_a_ctx")
            ytb_c = attn_b(fc["bq"], [(fc["bk"], fc["bvt"])], name="attn_b_ctx")
            xc = _post(xc, yta_c, ytb_c, ocf, ocb, fc["c"], mod, w, layer, ctx_row)
        x = x_new
    return x
```

```python
import functools

import numpy as np
import jax
import jax.numpy as jnp
from jax import lax
from jax.experimental import pallas as pl
from jax.experimental.pallas import tpu as pltpu

F32 = jnp.float32
BF16 = jnp.bfloat16

D_MODEL = 1024
GRID_W = 64
HEAD_DIM = 64
A_HEADS = 8
A_KV_HEADS = 2
B_HEADS = 4
B_Q_RANK = 192
B_KV_RANK = 128
B_NOPE = 64
B_ROPE = 32
B_V = 64
C_HEADS = 4
C_DK = 64
C_DV = 64
D_FF = 4 * D_MODEL
A_OUT = A_HEADS * HEAD_DIM
B_OUT = B_HEADS * B_V
C_OUT = C_HEADS * C_DV
C_W = C_HEADS * C_DK
ROPE_THETA = 10000.0
EPS = 1e-6
F_TINY = 1e-30

LANE = 128
VMEM_LIMIT = 56 * 1024 * 1024

OFF_AQ = 0
OFF_AK = 512
OFF_AV = 768
OFF_BQD = 896
OFF_BKVD = 1152
OFF_KPE = 1280
OFF_CQ = 1408
OFF_CFF = 1664
OFF_CFB = 1920
OFF_CI = 2176
OFF_CG = 2432
N_COL = 2688

C_Q, C_KF, C_KB, C_V, C_GATE = range(5)
N_SLAB = 5

SCAN_C = 64
N_LEVELS = 6
SCAN_BLOCK = 2048
SCAN_GROUP = 8


def _dot(a, b):
    return jnp.dot(a, b, preferred_element_type=F32)


def _dot_nt(a, b):
    return lax.dot_general(a, b, (((1,), (1,)), ((), ())), preferred_element_type=F32)


def _dot_tn(a, b):
    return lax.dot_general(a, b, (((0,), (0,)), ((), ())), preferred_element_type=F32)


def _sigmoid_pair(z):
    e = jnp.exp(-jnp.abs(z))
    inv = 1.0 / (1.0 + e)
    small = e * inv
    pos = z >= 0
    return jnp.where(pos, inv, small), jnp.where(pos, small, inv)


def _silu(z):
    s, _ = _sigmoid_pair(z)
    return z * s


def _const_spec(shape, index):
    return pl.BlockSpec(shape, lambda *_: index, pipeline_mode=pl.Buffered(1))


def _params(sem):
    return pltpu.CompilerParams(dimension_semantics=sem, vmem_limit_bytes=VMEM_LIMIT)


def _ada_kernel(c_ref, w_ref, b_ref, o_ref):
    a = _silu(c_ref[...])
    w = w_ref[...]
    a_hi = a.astype(BF16)
    a_lo = (a - a_hi.astype(F32)).astype(BF16)
    w_hi = w.astype(BF16)
    w_lo = (w - w_hi.astype(F32)).astype(BF16)
    acc = _dot(a_hi, w_hi) + (_dot(a_hi, w_lo) + _dot(a_lo, w_hi))
    o_ref[...] = acc + b_ref[...]


def _ada(cvec, w_ada, b_ada):
    depth = w_ada.shape[0]
    rows = cvec.shape[0]
    n_blk = w_ada.shape[2] // D_MODEL
    return pl.pallas_call(
        _ada_kernel,
        out_shape=jax.ShapeDtypeStruct((depth, rows, n_blk * D_MODEL), F32),
        grid=(depth, n_blk),
        in_specs=[
            pl.BlockSpec((rows, D_MODEL), lambda l, j: (0, 0)),
            pl.BlockSpec((None, D_MODEL, D_MODEL), lambda l, j: (l, 0, j)),
            pl.BlockSpec((None, None, 1, D_MODEL), lambda l, j: (l, j, 0, 0)),
        ],
        out_specs=pl.BlockSpec((None, rows, D_MODEL), lambda l, j: (l, 0, j)),
        compiler_params=_params(("arbitrary", "arbitrary")),
        name="ada",
    )(cvec, w_ada, b_ada.reshape(depth, n_blk, 1, D_MODEL))


def _rope128(v, cos, s1, s2, half):
    up = pltpu.roll(v, LANE - half, 1)
    dn = pltpu.roll(v, half, 1)
    return v * cos + up * s1 + dn * s2


def _head_rms(v, bo, gain):
    sq = v * v
    hi = sq.astype(BF16)
    lo = (sq - hi.astype(F32)).astype(BF16)
    ms = _dot(hi, bo) + _dot(lo, bo)
    return v * lax.rsqrt(ms + EPS) * gain


FEAT_TM = 1024
FEAT_SUB = 256
KV_CHUNK = 512


def _feat_kernel(*refs, use_rope):
    (x_ref, sh_ref, sc_ref, gpre_ref, win_ref, wq_ref, wkk_ref, wkv_ref,
     aqn_ref, akn_ref, bqn_ref, bkvn_ref, bo_ref, lb_ref) = refs[:14]
    rest = refs[14:]
    if use_rope:
        ca_ref, s1a_ref, s2a_ref, cb_ref, s1b_ref, s2b_ref = rest[:6]
        rest = rest[6:]
    aq_o, ak_o, avt_o, bq_o, bk_o, bvt_o, c_o, g_o = rest
    bo = bo_ref[...]
    tm = x_ref.shape[0]
    sub = min(tm, FEAT_SUB)

    for r0 in range(0, tm, sub):
        rs = slice(r0, r0 + sub)

        def rope_a(v, rs=rs):
            if not use_rope:
                return v
            return _rope128(v, ca_ref[rs, :], s1a_ref[rs, :], s2a_ref[rs, :], HEAD_DIM // 2)

        def rope_b(v, rs=rs):
            if not use_rope:
                return v
            return _rope128(v, cb_ref[rs, :], s1b_ref[rs, :], s2b_ref[rs, :], B_ROPE // 2)

        x = x_ref[rs, :]
        ms = jnp.mean(x * x, axis=-1, keepdims=True)
        h = x * lax.rsqrt(ms + EPS) * gpre_ref[...]
        h = h * (1.0 + sc_ref[...]) + sh_ref[...]
        p = _dot(h.astype(BF16), win_ref[...])

        aqn = aqn_ref[...]
        for half in range(2):
            v = _head_rms(p[:, OFF_AQ + 256 * half:OFF_AQ + 256 * (half + 1)], bo, aqn)
            for s in range(2):
                blk = rope_a(v[:, LANE * s:LANE * (s + 1)]) * (HEAD_DIM ** -0.5 * LOG2_E)
                lo = 256 * half + LANE * s
                aq_o[lo:lo + LANE, rs] = blk.T.astype(BF16)
        v = _head_rms(p[:, OFF_AK:OFF_AK + 256], bo, akn_ref[...])
        for s in range(2):
            ak_o[rs, LANE * s:LANE * (s + 1)] = rope_a(v[:, LANE * s:LANE * (s + 1)]).astype(BF16)
        kv_w = avt_o.shape[2]
        cs = slice(r0 % kv_w, r0 % kv_w + sub)
        avt_o[r0 // kv_w, :, cs] = p[:, OFF_AV:OFF_AV + LANE].T.astype(BF16)

        bqd = p[:, OFF_BQD:OFF_BQD + 256]
        ms = jnp.sum(bqd * bqd, axis=-1, keepdims=True) * (1.0 / B_Q_RANK)
        qn = (bqd * lax.rsqrt(ms + EPS) * bqn_ref[...]).astype(BF16)
        bq = _dot(qn, wq_ref[...])
        bkvd = p[:, OFF_BKVD:OFF_BKVD + LANE]
        ms = jnp.mean(bkvd * bkvd, axis=-1, keepdims=True)
        kvn = (bkvd * lax.rsqrt(ms + EPS) * bkvn_ref[...]).astype(BF16)
        bkn = _dot(kvn, wkk_ref[...])
        bv = _dot(kvn, wkv_ref[...])
        kpe = rope_b(p[:, OFF_KPE:OFF_KPE + LANE])
        b_scale = (B_NOPE + B_ROPE) ** -0.5 * LOG2_E
        for hh in range(B_HEADS):
            sl = slice(LANE * hh, LANE * (hh + 1))
            bq_o[sl, rs] = (rope_b(bq[:, sl]) * b_scale).T.astype(BF16)
            bk_o[rs, sl] = (bkn[:, sl] + kpe).astype(BF16)
        bvt_o[r0 // kv_w, :, cs] = bv.T.astype(BF16)

        c_o[rs, C_W * C_Q:C_W * (C_Q + 1)] = _silu(p[:, OFF_CQ:OFF_CQ + C_W])
        for d, (off, ck) in enumerate(((OFF_CFF, C_KF), (OFF_CFB, C_KB))):
            lb = lb_ref[d:d + 1, :]
            sp, sn = _sigmoid_pair(p[:, off:off + C_W])
            f = lb + (1.0 - lb) * sp
            g2 = jnp.log2(jnp.maximum(f, F_TINY))
            hi = g2.astype(BF16)
            g_o[rs, C_W * 2 * d:C_W * (2 * d + 1)] = hi
            g_o[rs, C_W * (2 * d + 1):C_W * (2 * d + 2)] = (g2 - hi.astype(F32)).astype(BF16)
            c_o[rs, C_W * ck:C_W * (ck + 1)] = (1.0 - lb) * sn
        c_o[rs, C_W * C_V:C_W * (C_V + 1)] = p[:, OFF_CI:OFF_CI + C_W]
        c_o[rs, C_W * C_GATE:C_W * (C_GATE + 1)] = p[:, OFF_CG:OFF_CG + C_W]


def _features(xs, mod, w, layer, mod_row, rope):
    bsz, t, _ = xs.shape
    tm = min(FEAT_TM, t)
    nt = t // tm
    kv_w = min(KV_CHUNK, t)
    n_kv = tm // kv_w
    nb_rows = w["n_mod_rows"]

    def mod_spec(j):
        return pl.BlockSpec((None, 1, D_MODEL),
                            lambda b, i: ((layer * nb_rows + mod_row(b)) * 6 + j, 0, 0))

    in_specs = [
        pl.BlockSpec((None, tm, D_MODEL), lambda b, i: (b, i, 0)),
        mod_spec(0), mod_spec(1),
        _const_spec((None, 1, D_MODEL), (layer, 0, 0)),
        _const_spec((None, D_MODEL, N_COL), (layer, 0, 0)),
        _const_spec((None, 256, 512), (layer, 0, 0)),
        _const_spec((None, LANE, 512), (layer, 0, 0)),
        _const_spec((None, LANE, 256), (layer, 0, 0)),
        _const_spec((None, 1, 256), (layer, 0, 0)),
        _const_spec((None, 1, 256), (layer, 0, 0)),
        _const_spec((None, 1, 256), (layer, 0, 0)),
        _const_spec((None, 1, LANE), (layer, 0, 0)),
        _const_spec((256, 256), (0, 0)),
        _const_spec((None, 2, C_W), (layer, 0, 0)),
    ]
    args = [xs, mod, mod, w["g_pre_mix"], w["w_in"], w["w_q_up"], w["w_kv_k"], w["w_kv_v"],
            w["a_q_norm"], w["a_k_norm"], w["b_q_norm"], w["b_kv_norm"], w["block_ones"], w["lower"]]
    if rope is not None:
        in_specs += [pl.BlockSpec((tm, LANE), lambda b, i: (i, 0))] * 6
        args += list(rope)
    out_shape = [
        jax.ShapeDtypeStruct((bsz, 512, t), BF16),
        jax.ShapeDtypeStruct((bsz, t, 256), BF16),
        jax.ShapeDtypeStruct((bsz, t // kv_w, LANE, kv_w), BF16),
        jax.ShapeDtypeStruct((bsz, 512, t), BF16),
        jax.ShapeDtypeStruct((bsz, t, 512), BF16),
        jax.ShapeDtypeStruct((bsz, t // kv_w, 256, kv_w), BF16),
        jax.ShapeDtypeStruct((bsz, t, N_SLAB * C_W), F32),
        jax.ShapeDtypeStruct((bsz, t, 4 * C_W), BF16),
    ]
    out_specs = [
        pl.BlockSpec((None, 512, tm), lambda b, i: (b, 0, i)),
        pl.BlockSpec((None, tm, 256), lambda b, i: (b, i, 0)),
        pl.BlockSpec((None, n_kv, LANE, kv_w), lambda b, i: (b, i, 0, 0)),
        pl.BlockSpec((None, 512, tm), lambda b, i: (b, 0, i)),
        pl.BlockSpec((None, tm, 512), lambda b, i: (b, i, 0)),
        pl.BlockSpec((None, n_kv, 256, kv_w), lambda b, i: (b, i, 0, 0)),
        pl.BlockSpec((None, tm, N_SLAB * C_W), lambda b, i: (b, i, 0)),
        pl.BlockSpec((None, tm, 4 * C_W), lambda b, i: (b, i, 0)),
    ]
    outs = pl.pallas_call(
        functools.partial(_feat_kernel, use_rope=rope is not None),
        out_shape=out_shape, grid=(bsz, nt), in_specs=in_specs, out_specs=out_specs,
        compiler_params=_params(("parallel", "parallel")),
        name="feat_rope" if rope is not None else "feat_ctx",
    )(*args)
    return dict(zip(("aq", "ak", "avt", "bq", "bk", "bvt", "c", "g"), outs))


ACC_ROWS = B_V + 16
NEG_BIG = -1e30
LOG2_E = 1.4426950408889634
ATTN_TQ = 1024


def _attn_kernel(*refs, n_src, n_heads, shared_kv):
    q_ref = refs[0]
    src_refs = refs[1:1 + 2 * n_src]
    o_ref = refs[1 + 2 * n_src]
    rhs_scr, m_scr, acc_scr, st_scr, mx_scr = refs[2 + 2 * n_src:]
    tq = q_ref.shape[1]
    n_grp, k_dim, grp_w = rhs_scr.shape
    heads_per_grp = n_heads // n_grp

    if shared_kv:
        for h in range(n_heads):
            rhs_scr[0, :, tq * h:tq * (h + 1)] = q_ref[HEAD_DIM * h:HEAD_DIM * (h + 1), :]
    else:
        rhs_scr[...] = jnp.zeros(rhs_scr.shape, BF16)
        for h in range(n_heads):
            g, j = divmod(h, heads_per_grp)
            rhs_scr[g, LANE * j:LANE * (j + 1), tq * j:tq * (j + 1)] = q_ref[LANE * h:LANE * (h + 1), :]
    m_scr[...] = jnp.full(m_scr.shape, NEG_BIG, F32)
    acc_scr[...] = jnp.zeros(acc_scr.shape, F32)

    def stage(s, c, slot):
        k_ref, vt_ref = src_refs[2 * s], src_refs[2 * s + 1]
        tk = vt_ref.shape[2]
        rows = pl.ds(pl.multiple_of(c * tk, tk), tk)
        for g in range(n_grp):
            cols = slice(grp_w * g, grp_w * (g + 1))
            kc = k_ref[rows, 0:k_dim] if shared_kv else k_ref[rows, k_dim * g:k_dim * (g + 1)]
            st = _dot(kc, rhs_scr[g])
            st_scr[slot, 0:tk, cols] = st
            mx_scr[slot, :, cols] = jnp.max(st, axis=0, keepdims=True)

    def consume(s, c, slot):
        vt_ref = src_refs[2 * s + 1]
        tk = vt_ref.shape[2]
        ones = jnp.ones((ACC_ROWS - B_V, tk), BF16)
        for g in range(n_grp):
            cols = slice(grp_w * g, grp_w * (g + 1))
            m_old = m_scr[:, cols]
            m_new = jnp.maximum(m_old, mx_scr[slot, :, cols])
            pt = jnp.exp2(st_scr[slot, 0:tk, cols] - m_new).astype(BF16)
            alpha = jnp.exp2(m_old - m_new)
            m_scr[:, cols] = m_new
            if shared_kv:
                vt = jnp.concatenate([vt_ref[c], ones], axis=0)
                acc_scr[:, cols] = alpha * acc_scr[:, cols] + _dot(vt, pt)
            else:
                for j in range(heads_per_grp):
                    h = g * heads_per_grp + j
                    hc = slice(tq * h, tq * (h + 1))
                    lc = slice(tq * j, tq * (j + 1))
                    vt = jnp.concatenate([vt_ref[c, B_V * h:B_V * (h + 1), :], ones], axis=0)
                    acc_scr[:, hc] = alpha[:, lc] * acc_scr[:, hc] + _dot(vt, pt[:, lc])

    n0 = src_refs[1].shape[0]
    n_loop = (n0 - 2) // 2 if n0 >= 4 else 0
    stage(0, 0, 0)
    if n_loop:
        def pair(i, carry):
            stage(0, 2 * i + 1, 1)
            consume(0, 2 * i, 0)
            stage(0, 2 * i + 2, 0)
            consume(0, 2 * i + 1, 1)
            return carry
        lax.fori_loop(0, n_loop, pair, 0)
    tail = [(0, c) for c in range(2 * n_loop, n0)]
    tail += [(s, c) for s in range(1, n_src) for c in range(src_refs[2 * s + 1].shape[0])]
    for i, (s, c) in enumerate(tail):
        if i + 1 < len(tail):
            stage(*tail[i + 1], (i + 1) % 2)
        consume(s, c, i % 2)

    for h in range(n_heads):
        acc = acc_scr[:, tq * h:tq * (h + 1)]
        o_ref[B_V * h:B_V * (h + 1), :] = (acc[:B_V] * (1.0 / acc[B_V:B_V + 1])).astype(BF16)


def _attention(q, srcs, *, n_kv, n_heads, shared_kv, name):
    bsz, _, t = q.shape
    tq = min(ATTN_TQ, t)
    q_r = (HEAD_DIM if shared_kv else LANE) * n_heads
    k_w = LANE * (1 if shared_kv else n_heads)
    v_r = B_V * (1 if shared_kv else n_heads)
    rhs_shape = (1, HEAD_DIM, n_heads * tq) if shared_kv else (n_heads // 2, 2 * LANE, 2 * tq)
    in_specs = [pl.BlockSpec((None, q_r, tq), lambda b, g, i: (b, g, i))]
    args = [q]
    for k, vt in srcs:
        length = k.shape[1]
        n_chunk, tk = vt.shape[1], vt.shape[3]
        in_specs.append(pl.BlockSpec((None, length, k_w), lambda b, g, i: (b, 0, g)))
        in_specs.append(pl.BlockSpec((None, n_chunk, v_r, tk), lambda b, g, i: (b, 0, g, 0)))
        args += [k, vt]
    rows = B_V * n_heads
    return pl.pallas_call(
        functools.partial(_attn_kernel, n_src=len(srcs), n_heads=n_heads, shared_kv=shared_kv),
        out_shape=jax.ShapeDtypeStruct((bsz, rows * n_kv, t), BF16),
        grid=(bsz, n_kv, t // tq),
        in_specs=in_specs,
        out_specs=pl.BlockSpec((None, rows, tq), lambda b, g, i: (b, g, i)),
        scratch_shapes=[pltpu.VMEM(rhs_shape, BF16),
                        pltpu.VMEM((1, n_heads * tq), F32),
                        pltpu.VMEM((ACC_ROWS, n_heads * tq), F32),
                        pltpu.VMEM((2, max(vt.shape[3] for _, vt in srcs), n_heads * tq), F32),
                        pltpu.VMEM((2, 1, n_heads * tq), F32)],
        compiler_params=_params(("parallel", "parallel", "parallel")),
        name=name,
    )(*args)


def _scan_constants(reverse):
    c = SCAN_C
    t = np.arange(c)[:, None]
    s = np.arange(c)[None, :]
    cum = (s >= t) if reverse else (s <= t)
    msk = np.zeros((N_LEVELS + 1, c, c), np.float32)
    for lvl in range(N_LEVELS):
        w = (c // 2) >> lvl
        same = (t // (2 * w)) == (s // (2 * w))
        t_hi = (t % (2 * w)) >= w
        s_hi = (s % (2 * w)) >= w
        msk[lvl] = (same & ~t_hi & s_hi) if reverse else (same & t_hi & ~s_hi)
    msk[N_LEVELS] = (t == s)
    return jnp.asarray(cum, BF16), jnp.asarray(np.tile(msk, (1, 1, C_HEADS)), F32)


def _boundary_rows(b, lvl, reverse):
    w = (SCAN_C // 2) >> lvl
    width = b.shape[1]
    off = w if reverse else w - 1
    if w == 1:
        odd = lax.broadcasted_iota(jnp.int32, b.shape, 0) % 2 == 1
        if reverse:
            return jnp.where(odd, b, pltpu.roll(b, SCAN_C - 1, 0))
        return jnp.where(odd, pltpu.roll(b, 1, 0), b)
    if w == 2:
        low = lax.broadcasted_iota(jnp.int32, (8, width), 0) < 4
        pieces = []
        for r0 in range(0, SCAN_C, 8):
            first = jnp.broadcast_to(b[r0 + off:r0 + off + 1, :], (8, width))
            second = jnp.broadcast_to(b[r0 + 4 + off:r0 + 5 + off, :], (8, width))
            pieces.append(jnp.where(low, first, second))
        return jnp.concatenate(pieces, axis=0)
    pieces = [jnp.broadcast_to(b[r0 + off:r0 + off + 1, :], (2 * w, width))
              for r0 in range(0, SCAN_C, 2 * w)]
    return pieces[0] if len(pieces) == 1 else jnp.concatenate(pieces, axis=0)


def _hgrn_kernel(q_ref, k_ref, ghi_ref, glo_ref, v_ref, s0_ref, cum_ref, msk_ref, bm_ref,
                 o_ref, sfin_ref, st_ref, *, reverse, n_chunk):
    i = pl.program_id(1)

    @pl.when(i == 0)
    def _():
        st_ref[...] = s0_ref[...]

    bm = bm_ref[...]
    bm16 = bm.astype(BF16)
    cum = cum_ref[...]

    def stack_heads(a):
        a16 = a.astype(BF16)
        return jnp.concatenate([a16] * C_HEADS, axis=0) * bm16

    grp = min(n_chunk, SCAN_GROUP)
    n_grp = n_chunk // grp

    def group(jg, carry):
        jgg = (n_grp - 1 - jg) if reverse else jg
        base = jgg * (grp * SCAN_C)
        order = list(range(grp - 1, -1, -1)) if reverse else list(range(grp))
        rows = [pl.ds(pl.multiple_of(base + c * SCAN_C, SCAN_C), SCAN_C) for c in range(grp)]
        q = [q_ref[r, :] for r in rows]
        k = [k_ref[r, :] for r in rows]
        v = [v_ref[r, :] for r in rows]
        b = [_dot(cum, ghi_ref[r, :]) + _dot(cum, glo_ref[r, :]) for r in rows]
        b_all = [bc[0:1, :] if reverse else bc[SCAN_C - 1:SCAN_C, :] for bc in b]

        q16 = [qc.astype(BF16) for qc in q]
        kst = [stack_heads(kc) for kc in k]
        sc = [msk_ref[N_LEVELS] * _dot_nt(q16[c], kst[c]) for c in range(grp)]
        for lvl in range(N_LEVELS):
            for c in range(grp):
                d = b[c] - _boundary_rows(b[c], lvl, reverse)
                e16 = jnp.exp2(-jnp.abs(d)).astype(BF16)
                kl = kst[c] * jnp.concatenate([e16] * C_HEADS, axis=0)
                sc[c] = sc[c] + msk_ref[lvl] * _dot_nt(q16[c] * e16, kl)

        ds = [_dot_tn(v[c].astype(BF16), (k[c] * jnp.exp2(b_all[c] - b[c])).astype(BF16)) * bm
              for c in range(grp)]
        st = st_ref[...]
        st_in = [None] * grp
        for c in order:
            st_in[c] = st
            st = st * jnp.exp2(b_all[c]) + ds[c]
        st_ref[...] = st

        for c in order:
            qdec = q[c] * jnp.exp2(b[c])
            o_ref[rows[c], :] = (_dot(sc[c].astype(BF16), stack_heads(v[c]))
                                 + _dot_nt(qdec.astype(BF16), st_in[c].astype(BF16)))
        return carry

    if n_grp == 1:
        group(0, 0)
    else:
        lax.fori_loop(0, n_grp, group, 0)

    @pl.when(i == pl.num_programs(1) - 1)
    def _():
        sfin_ref[...] = st_ref[...]


def _hgrn(c_slab, g16, s0, consts, block_mask, reverse):
    bsz, t, _ = c_slab.shape
    tb = min(SCAN_BLOCK, t)
    nblk = t // tb
    cum, msk = consts

    def blk(i):
        return (nblk - 1 - i) if reverse else i

    def slab_spec(j):
        return pl.BlockSpec((None, tb, C_W), lambda b, i: (b, blk(i), j))

    g0 = 2 if reverse else 0

    return pl.pallas_call(
        functools.partial(_hgrn_kernel, reverse=reverse, n_chunk=tb // SCAN_C),
        out_shape=[jax.ShapeDtypeStruct((bsz, t, C_W), F32),
                   jax.ShapeDtypeStruct((bsz, C_W, C_W), F32)],
        grid=(bsz, nblk),
        in_specs=[
            slab_spec(C_Q), slab_spec(C_KB if reverse else C_KF), slab_spec(g0), slab_spec(g0 + 1),
            slab_spec(C_V),
            pl.BlockSpec((None, C_W, C_W), lambda b, i: (b, 0, 0)),
            _const_spec(cum.shape, (0, 0)),
            _const_spec(msk.shape, (0, 0, 0)),
            _const_spec(block_mask.shape, (0, 0)),
        ],
        out_specs=[pl.BlockSpec((None, tb, C_W), lambda b, i: (b, blk(i), 0)),
                   pl.BlockSpec((None, C_W, C_W), lambda b, i: (b, 0, 0))],
        scratch_shapes=[pltpu.VMEM((C_W, C_W), F32)],
        compiler_params=_params(("parallel", "arbitrary")),
        name="hgrn_bwd" if reverse else "hgrn_fwd",
    )(c_slab, c_slab, g16, g16, c_slab, s0, cum, msk, block_mask)


POST_TM = 1024
POST_SUB = 256


def _rms(v, gain):
    ms = jnp.mean(v * v, axis=-1, keepdims=True)
    return v * lax.rsqrt(ms + EPS) * gain


def _post_kernel(x_ref, ya_ref, yb_ref, of_ref, ob_ref, gate_ref,
                 gtm_ref, shf_ref, scf_ref, gtf_ref,
                 gpm_ref, gpf_ref, gqf_ref, con_ref, bo_ref,
                 woa_ref, wob_ref, woc_ref, w1_ref, w2_ref, o_ref):
    tm = x_ref.shape[0]
    sub = min(tm, POST_SUB)
    for r0 in range(0, tm, sub):
        rs = slice(r0, r0 + sub)
        o = of_ref[rs, :] + ob_ref[rs, :]
        yc = (_head_rms(o, bo_ref[...], con_ref[...]) * _silu(gate_ref[rs, :])).astype(BF16)
        mix = (_dot_tn(ya_ref[:, rs], woa_ref[...]) + _dot_tn(yb_ref[:, rs], wob_ref[...])
               + _dot(yc, woc_ref[...]))
        x1 = x_ref[rs, :] + gtm_ref[...] * _rms(mix, gpm_ref[...])
        h = _rms(x1, gpf_ref[...]) * (1.0 + scf_ref[...]) + shf_ref[...]
        u = jnp.maximum(_dot(h.astype(BF16), w1_ref[...]), 0.0)
        ff = _dot((u * u).astype(BF16), w2_ref[...])
        o_ref[rs, :] = x1 + gtf_ref[...] * _rms(ff, gqf_ref[...])


def _post(xs, yta, ytb, o_f, o_b, c_slab, mod, w, layer, mod_row):
    bsz, t, _ = xs.shape
    tm = min(POST_TM, t)
    nb_rows = w["n_mod_rows"]

    def mod_spec(j):
        return pl.BlockSpec((None, 1, D_MODEL),
                            lambda b, i: ((layer * nb_rows + mod_row(b)) * 6 + j, 0, 0))

    def vec_spec():
        return _const_spec((None, 1, D_MODEL), (layer, 0, 0))

    in_specs = [
        pl.BlockSpec((None, tm, D_MODEL), lambda b, i: (b, i, 0)),
        pl.BlockSpec((None, A_OUT, tm), lambda b, i: (b, 0, i)),
        pl.BlockSpec((None, B_OUT, tm), lambda b, i: (b, 0, i)),
        pl.BlockSpec((None, tm, C_W), lambda b, i: (b, i, 0)),
        pl.BlockSpec((None, tm, C_W), lambda b, i: (b, i, 0)),
        pl.BlockSpec((None, tm, C_W), lambda b, i: (b, i, C_GATE)),
        mod_spec(2), mod_spec(3), mod_spec(4), mod_spec(5),
        vec_spec(), vec_spec(), vec_spec(),
        _const_spec((None, 1, C_W), (layer, 0, 0)),
        _const_spec((256, 256), (0, 0)),
        _const_spec((None, A_OUT, D_MODEL), (layer, 0, 0)),
        _const_spec((None, B_OUT, D_MODEL), (layer, 0, 0)),
        _const_spec((None, C_OUT, D_MODEL), (layer, 0, 0)),
        _const_spec((None, D_MODEL, D_FF), (layer, 0, 0)),
        _const_spec((None, D_FF, D_MODEL), (layer, 0, 0)),
    ]
    return pl.pallas_call(
        _post_kernel,
        out_shape=jax.ShapeDtypeStruct((bsz, t, D_MODEL), F32),
        grid=(bsz, t // tm),
        in_specs=in_specs,
        out_specs=pl.BlockSpec((None, tm, D_MODEL), lambda b, i: (b, i, 0)),
        compiler_params=_params(("parallel", "parallel")),
        name="post",
    )(xs, yta, ytb, o_f, o_b, c_slab, mod, mod, mod, mod,
      w["g_post_mix"], w["g_pre_ffn"], w["g_post_ffn"], w["c_out_norm"], w["block_ones"],
      w["w_out_a"], w["w_out_b"], w["w_out_c"], w["w_ff1"], w["w_ff2"])


def _w_in_columns():
    src = np.full((N_COL,), -1, np.int64)
    a_k0 = A_HEADS * HEAD_DIM
    a_v0 = a_k0 + A_KV_HEADS * HEAD_DIM
    b_qd0 = a_v0 + A_KV_HEADS * HEAD_DIM
    b_kv0 = b_qd0 + B_Q_RANK
    b_kr0 = b_kv0 + B_KV_RANK
    c0 = b_kr0 + B_ROPE
    src[OFF_AQ:OFF_AQ + 512] = np.arange(512)
    for g in range(A_KV_HEADS):
        for rep in range(2):
            lo = OFF_AK + (2 * g + rep) * HEAD_DIM
            src[lo:lo + HEAD_DIM] = a_k0 + g * HEAD_DIM + np.arange(HEAD_DIM)
    src[OFF_AV:OFF_AV + 128] = a_v0 + np.arange(128)
    src[OFF_BQD:OFF_BQD + B_Q_RANK] = b_qd0 + np.arange(B_Q_RANK)
    src[OFF_BKVD:OFF_BKVD + B_KV_RANK] = b_kv0 + np.arange(B_KV_RANK)
    src[OFF_KPE + B_NOPE:OFF_KPE + B_NOPE + B_ROPE] = b_kr0 + np.arange(B_ROPE)
    src[OFF_CQ:OFF_CQ + 5 * C_W] = c0 + np.arange(5 * C_W)
    return src


def _gather_cols(w, src):
    keep = jnp.asarray(src >= 0)
    return jnp.where(keep, jnp.take(w, jnp.asarray(np.maximum(src, 0)), axis=-1), 0.0)


def _prepare_weights(p, n_mod_rows):
    depth = p["w_in"].shape[0]
    w = {"n_mod_rows": n_mod_rows}
    w["w_in"] = _gather_cols(p["w_in"], _w_in_columns()).astype(BF16)
    src = np.full((B_HEADS * LANE,), -1, np.int64)
    for hh in range(B_HEADS):
        src[hh * LANE:hh * LANE + B_NOPE + B_ROPE] = hh * (B_NOPE + B_ROPE) + np.arange(B_NOPE + B_ROPE)
    wq = _gather_cols(p["w_q_up"], src)
    w["w_q_up"] = jnp.pad(wq, ((0, 0), (0, 256 - B_Q_RANK), (0, 0))).astype(BF16)
    src = np.full((B_HEADS * LANE,), -1, np.int64)
    srcv = np.zeros((B_HEADS * B_V,), np.int64)
    for hh in range(B_HEADS):
        src[hh * LANE:hh * LANE + B_NOPE] = hh * (B_NOPE + B_V) + np.arange(B_NOPE)
        srcv[hh * B_V:(hh + 1) * B_V] = hh * (B_NOPE + B_V) + B_NOPE + np.arange(B_V)
    w["w_kv_k"] = _gather_cols(p["w_kv_up"], src).astype(BF16)
    w["w_kv_v"] = _gather_cols(p["w_kv_up"], srcv).astype(BF16)
    w["a_q_norm"] = jnp.tile(p["a_q_norm"], (1, 4))[:, None, :]
    w["a_k_norm"] = jnp.tile(p["a_k_norm"], (1, 4))[:, None, :]
    w["b_q_norm"] = jnp.pad(p["b_q_norm"], ((0, 0), (0, 256 - B_Q_RANK)))[:, None, :]
    w["b_kv_norm"] = p["b_kv_norm"][:, None, :]
    w["c_out_norm"] = jnp.tile(p["c_out_norm"], (1, C_HEADS))[:, None, :]
    for name in ("g_pre_mix", "g_post_mix", "g_pre_ffn", "g_post_ffn"):
        w[name] = p[name][:, None, :]
    head = np.arange(256) // HEAD_DIM
    same_head = head[:, None] == head[None, :]
    w["block_ones"] = jnp.asarray(same_head / float(HEAD_DIM), BF16)
    w["block_mask"] = jnp.asarray(same_head, F32)
    p_lb = jax.nn.softmax(p["c_lower_bounds"].astype(F32), axis=0)
    w["lower"] = jnp.cumsum(p_lb, axis=0) - p_lb[:1]
    w["w_out_a"] = p["w_out"][:, :A_OUT].astype(BF16)
    w["w_out_b"] = p["w_out"][:, A_OUT:A_OUT + B_OUT].astype(BF16)
    w["w_out_c"] = p["w_out"][:, A_OUT + B_OUT:].astype(BF16)
    w["w_ff1"] = p["w_ff1"].astype(BF16)
    w["w_ff2"] = p["w_ff2"].astype(BF16)
    del depth
    return w


def _rope_tables(n_tok):
    tok = np.arange(n_tok)
    row = (tok // GRID_W).astype(np.float32)[:, None]
    col = (tok % GRID_W).astype(np.float32)[:, None]

    def angles(rot_dim):
        n_freq = rot_dim // 4
        inv = jnp.asarray(ROPE_THETA, F32) ** (-jnp.arange(n_freq, dtype=F32) / n_freq)
        ang = jnp.concatenate([jnp.asarray(row) * inv, jnp.asarray(col) * inv], axis=-1)
        return jnp.cos(ang), jnp.sin(ang)

    zeros = lambda n: jnp.zeros((n_tok, n), F32)
    ones = lambda n: jnp.ones((n_tok, n), F32)
    cos, sin = angles(HEAD_DIM)
    ca = jnp.concatenate([cos, cos, cos, cos], axis=-1)
    s1a = jnp.concatenate([-sin, zeros(32), -sin, zeros(32)], axis=-1)
    s2a = jnp.concatenate([zeros(32), sin, zeros(32), sin], axis=-1)
    cos, sin = angles(B_ROPE)
    cb = jnp.concatenate([ones(64), cos, cos, ones(32)], axis=-1)
    s1b = jnp.concatenate([zeros(64), -sin, zeros(16), zeros(32)], axis=-1)
    s2b = jnp.concatenate([zeros(64), zeros(16), sin, zeros(32)], axis=-1)
    return ca, s1a, s2a, cb, s1b, s2b


def kernel(x, c, ctx, c_ctx, w_ada, b_ada, g_pre_mix, g_post_mix, g_pre_ffn, g_post_ffn, w_in, a_q_norm, a_k_norm, b_q_norm, w_q_up, b_kv_norm, w_kv_up, c_lower_bounds, c_out_norm, w_out, w_ff1, w_ff2):
    bsz, n_lat, _ = x.shape
    depth = w_in.shape[0]
    n_mod_rows = -(-(bsz + 1) // 8) * 8
    params = dict(w_in=w_in, a_q_norm=a_q_norm, a_k_norm=a_k_norm, b_q_norm=b_q_norm, w_q_up=w_q_up,
                  b_kv_norm=b_kv_norm, w_kv_up=w_kv_up, c_lower_bounds=c_lower_bounds,
                  c_out_norm=c_out_norm, w_out=w_out, w_ff1=w_ff1, w_ff2=w_ff2,
                  g_pre_mix=g_pre_mix, g_post_mix=g_post_mix, g_pre_ffn=g_pre_ffn, g_post_ffn=g_post_ffn)
    w = _prepare_weights(params, n_mod_rows)
    rope = _rope_tables(n_lat)
    scan_f = _scan_constants(False)
    scan_b = _scan_constants(True)

    cvec = jnp.concatenate([c, c_ctx[None, :], jnp.zeros((n_mod_rows - bsz - 1, D_MODEL), F32)], axis=0)
    mod = _ada(cvec, w_ada, b_ada).reshape(depth * n_mod_rows * 6, 1, D_MODEL)

    lat_row = lambda b: b
    ctx_row = lambda b: bsz
    zero_state = jnp.zeros((bsz, C_W, C_W), F32)
    attn_a = functools.partial(_attention, n_kv=A_KV_HEADS, n_heads=A_HEADS // A_KV_HEADS, shared_kv=True)
    attn_b = functools.partial(_attention, n_kv=1, n_heads=B_HEADS, shared_kv=False)

    xc = ctx
    for layer in range(depth):
        need_ctx = layer < depth - 1
        fl = _features(x, mod, w, layer, lat_row, rope)
        fc = _features(xc, mod, w, layer, ctx_row, None)
        yta = attn_a(fl["aq"], [(fl["ak"], fl["avt"]), (fc["ak"], fc["avt"])], name="attn_a")
        ytb = attn_b(fl["bq"], [(fl["bk"], fl["bvt"]), (fc["bk"], fc["bvt"])], name="attn_b")
        ocf, s_f = _hgrn(fc["c"], fc["g"], zero_state, scan_f, w["block_mask"], False)
        ocb, s_b = _hgrn(fc["c"], fc["g"], zero_state, scan_b, w["block_mask"], True)
        olf, _ = _hgrn(fl["c"], fl["g"], s_f, scan_f, w["block_mask"], False)
        olb, _ = _hgrn(fl["c"], fl["g"], s_b, scan_b, w["block_mask"], True)
        x_new = _post(x, yta, ytb, olf, olb, fl["c"], mod, w, layer, lat_row)
        if need_ctx:
            yta_c = attn_a(fc["aq"], [(fc["ak"], fc["avt"])], name="attn_a_ctx")
            ytb_c = attn_b(fc["bq"], [(fc["bk"], fc["bvt"])], name="attn_b_ctx")
            xc = _post(xc, yta_c, ytb_c, ocf, ocb, fc["c"], mod, w, layer, ctx_row)
        x = x_new
    return x
```

```python
import functools

import numpy as np
import jax
import jax.numpy as jnp
from jax import lax
from jax.experimental import pallas as pl
from jax.experimental.pallas import tpu as pltpu

F32 = jnp.float32
BF16 = jnp.bfloat16

D_MODEL = 1024
GRID_W = 64
HEAD_DIM = 64
A_HEADS = 8
A_KV_HEADS = 2
B_HEADS = 4
B_Q_RANK = 192
B_KV_RANK = 128
B_NOPE = 64
B_ROPE = 32
B_V = 64
C_HEADS = 4
C_DK = 64
C_DV = 64
D_FF = 4 * D_MODEL
A_OUT = A_HEADS * HEAD_DIM
B_OUT = B_HEADS * B_V
C_OUT = C_HEADS * C_DV
C_W = C_HEADS * C_DK
ROPE_THETA = 10000.0
EPS = 1e-6
F_TINY = 1e-30

LANE = 128
VMEM_LIMIT = 56 * 1024 * 1024

OFF_AQ = 0
OFF_AK = 512
OFF_AV = 768
OFF_BQD = 896
OFF_BKVD = 1152
OFF_KPE = 1280
OFF_CQ = 1408
OFF_CFF = 1664
OFF_CFB = 1920
OFF_CI = 2176
OFF_CG = 2432
N_COL = 2688

C_Q, C_KF, C_KB, C_V, C_GATE = range(5)
N_SLAB = 5

SCAN_C = 64
N_LEVELS = 6
SCAN_BLOCK = 2048
SCAN_GROUP = 8


def _dot(a, b):
    return jnp.dot(a, b, preferred_element_type=F32)


def _dot_nt(a, b):
    return lax.dot_general(a, b, (((1,), (1,)), ((), ())), preferred_element_type=F32)


def _dot_tn(a, b):
    return lax.dot_general(a, b, (((0,), (0,)), ((), ())), preferred_element_type=F32)


def _sigmoid_pair(z):
    e = jnp.exp(-jnp.abs(z))
    inv = 1.0 / (1.0 + e)
    small = e * inv
    pos = z >= 0
    return jnp.where(pos, inv, small), jnp.where(pos, small, inv)


def _silu(z):
    s, _ = _sigmoid_pair(z)
    return z * s


def _const_spec(shape, index):
    return pl.BlockSpec(shape, lambda *_: index, pipeline_mode=pl.Buffered(1))


def _params(sem):
    return pltpu.CompilerParams(dimension_semantics=sem, vmem_limit_bytes=VMEM_LIMIT)


def _ada_kernel(c_ref, w_ref, b_ref, o_ref):
    a = _silu(c_ref[...])
    w = w_ref[...]
    a_hi = a.astype(BF16)
    a_lo = (a - a_hi.astype(F32)).astype(BF16)
    w_hi = w.astype(BF16)
    w_lo = (w - w_hi.astype(F32)).astype(BF16)
    acc = _dot(a_hi, w_hi) + (_dot(a_hi, w_lo) + _dot(a_lo, w_hi))
    o_ref[...] = acc + b_ref[...]


def _ada(cvec, w_ada, b_ada):
    depth = w_ada.shape[0]
    rows = cvec.shape[0]
    n_blk = w_ada.shape[2] // D_MODEL
    return pl.pallas_call(
        _ada_kernel,
        out_shape=jax.ShapeDtypeStruct((depth, rows, n_blk * D_MODEL), F32),
        grid=(depth, n_blk),
        in_specs=[
            pl.BlockSpec((rows, D_MODEL), lambda l, j: (0, 0)),
            pl.BlockSpec((None, D_MODEL, D_MODEL), lambda l, j: (l, 0, j)),
            pl.BlockSpec((None, None, 1, D_MODEL), lambda l, j: (l, j, 0, 0)),
        ],
        out_specs=pl.BlockSpec((None, rows, D_MODEL), lambda l, j: (l, 0, j)),
        compiler_params=_params(("arbitrary", "arbitrary")),
        name="ada",
    )(cvec, w_ada, b_ada.reshape(depth, n_blk, 1, D_MODEL))


def _rope128(v, cos, s1, s2, half):
    up = pltpu.roll(v, LANE - half, 1)
    dn = pltpu.roll(v, half, 1)
    return v * cos + up * s1 + dn * s2


def _head_rms(v, bo, gain):
    sq = v * v
    hi = sq.astype(BF16)
    lo = (sq - hi.astype(F32)).astype(BF16)
    ms = _dot(hi, bo) + _dot(lo, bo)
    return v * lax.rsqrt(ms + EPS) * gain


FEAT_TM = 1024
FEAT_SUB = 256
KV_CHUNK = 512


def _feat_kernel(*refs, use_rope):
    (x_ref, sh_ref, sc_ref, gpre_ref, win_ref, wq_ref, wkk_ref, wkv_ref,
     aqn_ref, akn_ref, bqn_ref, bkvn_ref, bo_ref, lb_ref) = refs[:14]
    rest = refs[14:]
    if use_rope:
        ca_ref, s1a_ref, s2a_ref, cb_ref, s1b_ref, s2b_ref = rest[:6]
        rest = rest[6:]
    aq_o, ak_o, avt_o, bq_o, bk_o, bvt_o, c_o, g_o = rest
    bo = bo_ref[...]
    tm = x_ref.shape[0]
    sub = min(tm, FEAT_SUB)

    for r0 in range(0, tm, sub):
        rs = slice(r0, r0 + sub)

        def rope_a(v, rs=rs):
            if not use_rope:
                return v
            return _rope128(v, ca_ref[rs, :], s1a_ref[rs, :], s2a_ref[rs, :], HEAD_DIM // 2)

        def rope_b(v, rs=rs):
            if not use_rope:
                return v
            return _rope128(v, cb_ref[rs, :], s1b_ref[rs, :], s2b_ref[rs, :], B_ROPE // 2)

        x = x_ref[rs, :]
        ms = jnp.mean(x * x, axis=-1, keepdims=True)
        h = x * lax.rsqrt(ms + EPS) * gpre_ref[...]
        h = h * (1.0 + sc_ref[...]) + sh_ref[...]
        p = _dot(h.astype(BF16), win_ref[...])

        aqn = aqn_ref[...]
        for half in range(2):
            v = _head_rms(p[:, OFF_AQ + 256 * half:OFF_AQ + 256 * (half + 1)], bo, aqn)
            for s in range(2):
                blk = rope_a(v[:, LANE * s:LANE * (s + 1)]) * (HEAD_DIM ** -0.5 * LOG2_E)
                lo = 256 * half + LANE * s
                aq_o[lo:lo + LANE, rs] = blk.T.astype(BF16)
        v = _head_rms(p[:, OFF_AK:OFF_AK + 256], bo, akn_ref[...])
        for s in range(2):
            ak_o[rs, LANE * s:LANE * (s + 1)] = rope_a(v[:, LANE * s:LANE * (s + 1)]).astype(BF16)
        kv_w = avt_o.shape[2]
        cs = slice(r0 % kv_w, r0 % kv_w + sub)
        avt_o[r0 // kv_w, :, cs] = p[:, OFF_AV:OFF_AV + LANE].T.astype(BF16)

        bqd = p[:, OFF_BQD:OFF_BQD + 256]
        ms = jnp.sum(bqd * bqd, axis=-1, keepdims=True) * (1.0 / B_Q_RANK)
        qn = (bqd * lax.rsqrt(ms + EPS) * bqn_ref[...]).astype(BF16)
        bq = _dot(qn, wq_ref[...])
        bkvd = p[:, OFF_BKVD:OFF_BKVD + LANE]
        ms = jnp.mean(bkvd * bkvd, axis=-1, keepdims=True)
        kvn = (bkvd * lax.rsqrt(ms + EPS) * bkvn_ref[...]).astype(BF16)
        bkn = _dot(kvn, wkk_ref[...])
        bv = _dot(kvn, wkv_ref[...])
        kpe = rope_b(p[:, OFF_KPE:OFF_KPE + LANE])
        b_scale = (B_NOPE + B_ROPE) ** -0.5 * LOG2_E
        for hh in range(B_HEADS):
            sl = slice(LANE * hh, LANE * (hh + 1))
            bq_o[sl, rs] = (rope_b(bq[:, sl]) * b_scale).T.astype(BF16)
            bk_o[rs, sl] = (bkn[:, sl] + kpe).astype(BF16)
        bvt_o[r0 // kv_w, :, cs] = bv.T.astype(BF16)

        c_o[rs, C_W * C_Q:C_W * (C_Q + 1)] = _silu(p[:, OFF_CQ:OFF_CQ + C_W])
        for d, (off, ck) in enumerate(((OFF_CFF, C_KF), (OFF_CFB, C_KB))):
            lb = lb_ref[d:d + 1, :]
            sp, sn = _sigmoid_pair(p[:, off:off + C_W])
            f = lb + (1.0 - lb) * sp
            g2 = jnp.log2(jnp.maximum(f, F_TINY))
            hi = g2.astype(BF16)
            g_o[rs, C_W * 2 * d:C_W * (2 * d + 1)] = hi
            g_o[rs, C_W * (2 * d + 1):C_W * (2 * d + 2)] = (g2 - hi.astype(F32)).astype(BF16)
            c_o[rs, C_W * ck:C_W * (ck + 1)] = (1.0 - lb) * sn
        c_o[rs, C_W * C_V:C_W * (C_V + 1)] = p[:, OFF_CI:OFF_CI + C_W]
        c_o[rs, C_W * C_GATE:C_W * (C_GATE + 1)] = p[:, OFF_CG:OFF_CG + C_W]


def _features(xs, mod, w, layer, mod_row, rope):
    bsz, t, _ = xs.shape
    tm = min(FEAT_TM, t)
    nt = t // tm
    kv_w = min(KV_CHUNK, t)
    n_kv = tm // kv_w
    nb_rows = w["n_mod_rows"]

    def mod_spec(j):
        return pl.BlockSpec((None, 1, D_MODEL),
                            lambda b, i: ((layer * nb_rows + mod_row(b)) * 6 + j, 0, 0))

    in_specs = [
        pl.BlockSpec((None, tm, D_MODEL), lambda b, i: (b, i, 0)),
        mod_spec(0), mod_spec(1),
        _const_spec((None, 1, D_MODEL), (layer, 0, 0)),
        _const_spec((None, D_MODEL, N_COL), (layer, 0, 0)),
        _const_spec((None, 256, 512), (layer, 0, 0)),
        _const_spec((None, LANE, 512), (layer, 0, 0)),
        _const_spec((None, LANE, 256), (layer, 0, 0)),
        _const_spec((None, 1, 256), (layer, 0, 0)),
        _const_spec((None, 1, 256), (layer, 0, 0)),
        _const_spec((None, 1, 256), (layer, 0, 0)),
        _const_spec((None, 1, LANE), (layer, 0, 0)),
        _const_spec((256, 256), (0, 0)),
        _const_spec((None, 2, C_W), (layer, 0, 0)),
    ]
    args = [xs, mod, mod, w["g_pre_mix"], w["w_in"], w["w_q_up"], w["w_kv_k"], w["w_kv_v"],
            w["a_q_norm"], w["a_k_norm"], w["b_q_norm"], w["b_kv_norm"], w["block_ones"], w["lower"]]
    if rope is not None:
        in_specs += [pl.BlockSpec((tm, LANE), lambda b, i: (i, 0))] * 6
        args += list(rope)
    out_shape = [
        jax.ShapeDtypeStruct((bsz, 512, t), BF16),
        jax.ShapeDtypeStruct((bsz, t, 256), BF16),
        jax.ShapeDtypeStruct((bsz, t // kv_w, LANE, kv_w), BF16),
        jax.ShapeDtypeStruct((bsz, 512, t), BF16),
        jax.ShapeDtypeStruct((bsz, t, 512), BF16),
        jax.ShapeDtypeStruct((bsz, t // kv_w, 256, kv_w), BF16),
        jax.ShapeDtypeStruct((bsz, t, N_SLAB * C_W), F32),
        jax.ShapeDtypeStruct((bsz, t, 4 * C_W), BF16),
    ]
    out_specs = [
        pl.BlockSpec((None, 512, tm), lambda b, i: (b, 0, i)),
        pl.BlockSpec((None, tm, 256), lambda b, i: (b, i, 0)),
        pl.BlockSpec((None, n_kv, LANE, kv_w), lambda b, i: (b, i, 0, 0)),
        pl.BlockSpec((None, 512, tm), lambda b, i: (b, 0, i)),
        pl.BlockSpec((None, tm, 512), lambda b, i: (b, i, 0)),
        pl.BlockSpec((None, n_kv, 256, kv_w), lambda b, i: (b, i, 0, 0)),
        pl.BlockSpec((None, tm, N_SLAB * C_W), lambda b, i: (b, i, 0)),
        pl.BlockSpec((None, tm, 4 * C_W), lambda b, i: (b, i, 0)),
    ]
    outs = pl.pallas_call(
        functools.partial(_feat_kernel, use_rope=rope is not None),
        out_shape=out_shape, grid=(bsz, nt), in_specs=in_specs, out_specs=out_specs,
        compiler_params=_params(("parallel", "parallel")),
        name="feat_rope" if rope is not None else "feat_ctx",
    )(*args)
    return dict(zip(("aq", "ak", "avt", "bq", "bk", "bvt", "c", "g"), outs))


ACC_ROWS = B_V + 16
NEG_BIG = -1e30
LOG2_E = 1.4426950408889634
ATTN_TQ = 1024


def _attn_kernel(*refs, n_src, n_heads, shared_kv):
    q_ref = refs[0]
    src_refs = refs[1:1 + 2 * n_src]
    o_ref = refs[1 + 2 * n_src]
    rhs_scr, m_scr, acc_scr, st_scr, mx_scr = refs[2 + 2 * n_src:]
    tq = q_ref.shape[1]
    n_grp, k_dim, grp_w = rhs_scr.shape
    heads_per_grp = n_heads // n_grp

    if shared_kv:
        for h in range(n_heads):
            rhs_scr[0, :, tq * h:tq * (h + 1)] = q_ref[HEAD_DIM * h:HEAD_DIM * (h + 1), :]
    else:
        rhs_scr[...] = jnp.zeros(rhs_scr.shape, BF16)
        for h in range(n_heads):
            g, j = divmod(h, heads_per_grp)
            rhs_scr[g, LANE * j:LANE * (j + 1), tq * j:tq * (j + 1)] = q_ref[LANE * h:LANE * (h + 1), :]
    m_scr[...] = jnp.full(m_scr.shape, NEG_BIG, F32)
    acc_scr[...] = jnp.zeros(acc_scr.shape, F32)

    def stage(s, c, slot):
        k_ref, vt_ref = src_refs[2 * s], src_refs[2 * s + 1]
        tk = vt_ref.shape[2]
        rows = pl.ds(pl.multiple_of(c * tk, tk), tk)
        for g in range(n_grp):
            cols = slice(grp_w * g, grp_w * (g + 1))
            kc = k_ref[rows, 0:k_dim] if shared_kv else k_ref[rows, k_dim * g:k_dim * (g + 1)]
            st = _dot(kc, rhs_scr[g])
            st_scr[slot, 0:tk, cols] = st
            mx_scr[slot, :, cols] = jnp.max(st, axis=0, keepdims=True)

    def consume(s, c, slot):
        vt_ref = src_refs[2 * s + 1]
        tk = vt_ref.shape[2]
        ones = jnp.ones((ACC_ROWS - B_V, tk), BF16)
        for g in range(n_grp):
            cols = slice(grp_w * g, grp_w * (g + 1))
            m_old = m_scr[:, cols]
            m_new = jnp.maximum(m_old, mx_scr[slot, :, cols])
            pt = jnp.exp2(st_scr[slot, 0:tk, cols] - m_new).astype(BF16)
            alpha = jnp.exp2(m_old - m_new)
            m_scr[:, cols] = m_new
            if shared_kv:
                vt = jnp.concatenate([vt_ref[c], ones], axis=0)
                acc_scr[:, cols] = alpha * acc_scr[:, cols] + _dot(vt, pt)
            else:
                for j in range(heads_per_grp):
                    h = g * heads_per_grp + j
                    hc = slice(tq * h, tq * (h + 1))
                    lc = slice(tq * j, tq * (j + 1))
                    vt = jnp.concatenate([vt_ref[c, B_V * h:B_V * (h + 1), :], ones], axis=0)
                    acc_scr[:, hc] = alpha[:, lc] * acc_scr[:, hc] + _dot(vt, pt[:, lc])

    n0 = src_refs[1].shape[0]
    n_loop = (n0 - 2) // 2 if n0 >= 4 else 0
    stage(0, 0, 0)
    if n_loop:
        def pair(i, carry):
            stage(0, 2 * i + 1, 1)
            consume(0, 2 * i, 0)
            stage(0, 2 * i + 2, 0)
            consume(0, 2 * i + 1, 1)
            return carry
        lax.fori_loop(0, n_loop, pair, 0)
    tail = [(0, c) for c in range(2 * n_loop, n0)]
    tail += [(s, c) for s in range(1, n_src) for c in range(src_refs[2 * s + 1].shape[0])]
    for i, (s, c) in enumerate(tail):
        if i + 1 < len(tail):
            stage(*tail[i + 1], (i + 1) % 2)
        consume(s, c, i % 2)

    for h in range(n_heads):
        acc = acc_scr[:, tq * h:tq * (h + 1)]
        o_ref[B_V * h:B_V * (h + 1), :] = (acc[:B_V] * (1.0 / acc[B_V:B_V + 1])).astype(BF16)


def _attention(q, srcs, *, n_kv, n_heads, shared_kv, name):
    bsz, _, t = q.shape
    tq = min(ATTN_TQ * (2 if shared_kv else 1), t)
    q_r = (HEAD_DIM if shared_kv else LANE) * n_heads
    k_w = LANE * (1 if shared_kv else n_heads)
    v_r = B_V * (1 if shared_kv else n_heads)
    rhs_shape = (1, HEAD_DIM, n_heads * tq) if shared_kv else (n_heads // 2, 2 * LANE, 2 * tq)
    in_specs = [pl.BlockSpec((None, q_r, tq), lambda b, g, i: (b, g, i))]
    args = [q]
    for k, vt in srcs:
        length = k.shape[1]
        n_chunk, tk = vt.shape[1], vt.shape[3]
        in_specs.append(pl.BlockSpec((None, length, k_w), lambda b, g, i: (b, 0, g)))
        in_specs.append(pl.BlockSpec((None, n_chunk, v_r, tk), lambda b, g, i: (b, 0, g, 0)))
        args += [k, vt]
    rows = B_V * n_heads
    return pl.pallas_call(
        functools.partial(_attn_kernel, n_src=len(srcs), n_heads=n_heads, shared_kv=shared_kv),
        out_shape=jax.ShapeDtypeStruct((bsz, rows * n_kv, t), BF16),
        grid=(bsz, n_kv, t // tq),
        in_specs=in_specs,
        out_specs=pl.BlockSpec((None, rows, tq), lambda b, g, i: (b, g, i)),
        scratch_shapes=[pltpu.VMEM(rhs_shape, BF16),
                        pltpu.VMEM((1, n_heads * tq), F32),
                        pltpu.VMEM((ACC_ROWS, n_heads * tq), F32),
                        pltpu.VMEM((2, max(vt.shape[3] for _, vt in srcs), n_heads * tq), F32),
                        pltpu.VMEM((2, 1, n_heads * tq), F32)],
        compiler_params=_params(("parallel", "parallel", "parallel")),
        name=name,
    )(*args)


def _scan_constants(reverse):
    c = SCAN_C
    t = np.arange(c)[:, None]
    s = np.arange(c)[None, :]
    cum = (s >= t) if reverse else (s <= t)
    msk = np.zeros((N_LEVELS + 1, c, c), np.float32)
    for lvl in range(N_LEVELS):
        w = (c // 2) >> lvl
        same = (t // (2 * w)) == (s // (2 * w))
        t_hi = (t % (2 * w)) >= w
        s_hi = (s % (2 * w)) >= w
        msk[lvl] = (same & ~t_hi & s_hi) if reverse else (same & t_hi & ~s_hi)
    msk[N_LEVELS] = (t == s)
    return jnp.asarray(cum, BF16), jnp.asarray(np.tile(msk, (1, 1, C_HEADS)), F32)


def _boundary_rows(b, lvl, reverse):
    w = (SCAN_C // 2) >> lvl
    width = b.shape[1]
    off = w if reverse else w - 1
    if w == 1:
        odd = lax.broadcasted_iota(jnp.int32, b.shape, 0) % 2 == 1
        if reverse:
            return jnp.where(odd, b, pltpu.roll(b, SCAN_C - 1, 0))
        return jnp.where(odd, pltpu.roll(b, 1, 0), b)
    if w == 2:
        low = lax.broadcasted_iota(jnp.int32, (8, width), 0) < 4
        pieces = []
        for r0 in range(0, SCAN_C, 8):
            first = jnp.broadcast_to(b[r0 + off:r0 + off + 1, :], (8, width))
            second = jnp.broadcast_to(b[r0 + 4 + off:r0 + 5 + off, :], (8, width))
            pieces.append(jnp.where(low, first, second))
        return jnp.concatenate(pieces, axis=0)
    pieces = [jnp.broadcast_to(b[r0 + off:r0 + off + 1, :], (2 * w, width))
              for r0 in range(0, SCAN_C, 2 * w)]
    return pieces[0] if len(pieces) == 1 else jnp.concatenate(pieces, axis=0)


def _hgrn_kernel(q_ref, k_ref, ghi_ref, glo_ref, v_ref, s0_ref, cum_ref, msk_ref, bm_ref,
                 o_ref, sfin_ref, st_ref, kt_ref, *, reverse, n_chunk):
    i = pl.program_id(1)

    @pl.when(i == 0)
    def _():
        st_ref[...] = s0_ref[...]

    bm = bm_ref[...]
    bm16 = bm.astype(BF16)
    cum = cum_ref[...]

    def stack_heads(a):
        a16 = a.astype(BF16)
        return jnp.concatenate([a16] * C_HEADS, axis=0) * bm16

    grp = min(n_chunk, SCAN_GROUP)
    n_grp = n_chunk // grp

    def group(jg, carry):
        jgg = (n_grp - 1 - jg) if reverse else jg
        base = jgg * (grp * SCAN_C)
        order = list(range(grp - 1, -1, -1)) if reverse else list(range(grp))
        rows = [pl.ds(pl.multiple_of(base + c * SCAN_C, SCAN_C), SCAN_C) for c in range(grp)]
        q = [q_ref[r, :] for r in rows]
        k = [k_ref[r, :] for r in rows]
        v = [v_ref[r, :] for r in rows]
        b = [_dot(cum, ghi_ref[r, :]) + _dot(cum, glo_ref[r, :]) for r in rows]
        b_all = [bc[0:1, :] if reverse else bc[SCAN_C - 1:SCAN_C, :] for bc in b]

        q16 = [qc.astype(BF16) for qc in q]
        kst = [stack_heads(kc) for kc in k]
        sc = [msk_ref[N_LEVELS] * _dot_nt(q16[c], kst[c]) for c in range(grp)]
        for lvl in range(N_LEVELS):
            for c in range(grp):
                d = b[c] - _boundary_rows(b[c], lvl, reverse)
                e16 = jnp.exp2(-jnp.abs(d)).astype(BF16)
                kl = kst[c] * jnp.concatenate([e16] * C_HEADS, axis=0)
                kt_ref[c] = kl.T
                sc[c] = sc[c] + msk_ref[lvl] * _dot(q16[c] * e16, kt_ref[c])

        ds = [_dot_tn(v[c].astype(BF16), (k[c] * jnp.exp2(b_all[c] - b[c])).astype(BF16)) * bm
              for c in range(grp)]
        st = st_ref[...]
        st_in = [None] * grp
        for c in order:
            st_in[c] = st
            st = st * jnp.exp2(b_all[c]) + ds[c]
        st_ref[...] = st

        for c in order:
            qdec = q[c] * jnp.exp2(b[c])
            o_ref[rows[c], :] = (_dot(sc[c].astype(BF16), stack_heads(v[c]))
                                 + _dot_nt(qdec.astype(BF16), st_in[c].astype(BF16)))
        return carry

    if n_grp == 1:
        group(0, 0)
    else:
        lax.fori_loop(0, n_grp, group, 0)

    @pl.when(i == pl.num_programs(1) - 1)
    def _():
        sfin_ref[...] = st_ref[...]


def _hgrn(c_slab, g16, s0, consts, block_mask, reverse):
    bsz, t, _ = c_slab.shape
    tb = min(SCAN_BLOCK, t)
    nblk = t // tb
    cum, msk = consts

    def blk(i):
        return (nblk - 1 - i) if reverse else i

    def slab_spec(j):
        return pl.BlockSpec((None, tb, C_W), lambda b, i: (b, blk(i), j))

    g0 = 2 if reverse else 0

    return pl.pallas_call(
        functools.partial(_hgrn_kernel, reverse=reverse, n_chunk=tb // SCAN_C),
        out_shape=[jax.ShapeDtypeStruct((bsz, t, C_W), F32),
                   jax.ShapeDtypeStruct((bsz, C_W, C_W), F32)],
        grid=(bsz, nblk),
        in_specs=[
            slab_spec(C_Q), slab_spec(C_KB if reverse else C_KF), slab_spec(g0), slab_spec(g0 + 1),
            slab_spec(C_V),
            pl.BlockSpec((None, C_W, C_W), lambda b, i: (b, 0, 0)),
            _const_spec(cum.shape, (0, 0)),
            _const_spec(msk.shape, (0, 0, 0)),
            _const_spec(block_mask.shape, (0, 0)),
        ],
        out_specs=[pl.BlockSpec((None, tb, C_W), lambda b, i: (b, blk(i), 0)),
                   pl.BlockSpec((None, C_W, C_W), lambda b, i: (b, 0, 0))],
        scratch_shapes=[pltpu.VMEM((C_W, C_W), F32),
                        pltpu.VMEM((min(tb // SCAN_C, SCAN_GROUP), C_W, C_HEADS * SCAN_C), BF16)],
        compiler_params=_params(("parallel", "arbitrary")),
        name="hgrn_bwd" if reverse else "hgrn_fwd",
    )(c_slab, c_slab, g16, g16, c_slab, s0, cum, msk, block_mask)


POST_TM = 1024
POST_SUB = 256


def _rms(v, gain):
    ms = jnp.mean(v * v, axis=-1, keepdims=True)
    return v * lax.rsqrt(ms + EPS) * gain


def _post_kernel(x_ref, ya_ref, yb_ref, of_ref, ob_ref, gate_ref,
                 gtm_ref, shf_ref, scf_ref, gtf_ref,
                 gpm_ref, gpf_ref, gqf_ref, con_ref, bo_ref,
                 woa_ref, wob_ref, woc_ref, w1_ref, w2_ref, o_ref):
    tm = x_ref.shape[0]
    sub = min(tm, POST_SUB)
    for r0 in range(0, tm, sub):
        rs = slice(r0, r0 + sub)
        o = of_ref[rs, :] + ob_ref[rs, :]
        yc = (_head_rms(o, bo_ref[...], con_ref[...]) * _silu(gate_ref[rs, :])).astype(BF16)
        mix = (_dot_tn(ya_ref[:, rs], woa_ref[...]) + _dot_tn(yb_ref[:, rs], wob_ref[...])
               + _dot(yc, woc_ref[...]))
        x1 = x_ref[rs, :] + gtm_ref[...] * _rms(mix, gpm_ref[...])
        h = _rms(x1, gpf_ref[...]) * (1.0 + scf_ref[...]) + shf_ref[...]
        u = jnp.maximum(_dot(h.astype(BF16), w1_ref[...]), 0.0)
        ff = _dot((u * u).astype(BF16), w2_ref[...])
        o_ref[rs, :] = x1 + gtf_ref[...] * _rms(ff, gqf_ref[...])


def _post(xs, yta, ytb, o_f, o_b, c_slab, mod, w, layer, mod_row):
    bsz, t, _ = xs.shape
    tm = min(POST_TM, t)
    nb_rows = w["n_mod_rows"]

    def mod_spec(j):
        return pl.BlockSpec((None, 1, D_MODEL),
                            lambda b, i: ((layer * nb_rows + mod_row(b)) * 6 + j, 0, 0))

    def vec_spec():
        return _const_spec((None, 1, D_MODEL), (layer, 0, 0))

    in_specs = [
        pl.BlockSpec((None, tm, D_MODEL), lambda b, i: (b, i, 0)),
        pl.BlockSpec((None, A_OUT, tm), lambda b, i: (b, 0, i)),
        pl.BlockSpec((None, B_OUT, tm), lambda b, i: (b, 0, i)),
        pl.BlockSpec((None, tm, C_W), lambda b, i: (b, i, 0)),
        pl.BlockSpec((None, tm, C_W), lambda b, i: (b, i, 0)),
        pl.BlockSpec((None, tm, C_W), lambda b, i: (b, i, C_GATE)),
        mod_spec(2), mod_spec(3), mod_spec(4), mod_spec(5),
        vec_spec(), vec_spec(), vec_spec(),
        _const_spec((None, 1, C_W), (layer, 0, 0)),
        _const_spec((256, 256), (0, 0)),
        _const_spec((None, A_OUT, D_MODEL), (layer, 0, 0)),
        _const_spec((None, B_OUT, D_MODEL), (layer, 0, 0)),
        _const_spec((None, C_OUT, D_MODEL), (layer, 0, 0)),
        _const_spec((None, D_MODEL, D_FF), (layer, 0, 0)),
        _const_spec((None, D_FF, D_MODEL), (layer, 0, 0)),
    ]
    return pl.pallas_call(
        _post_kernel,
        out_shape=jax.ShapeDtypeStruct((bsz, t, D_MODEL), F32),
        grid=(bsz, t // tm),
        in_specs=in_specs,
        out_specs=pl.BlockSpec((None, tm, D_MODEL), lambda b, i: (b, i, 0)),
        compiler_params=_params(("parallel", "parallel")),
        name="post",
    )(xs, yta, ytb, o_f, o_b, c_slab, mod, mod, mod, mod,
      w["g_post_mix"], w["g_pre_ffn"], w["g_post_ffn"], w["c_out_norm"], w["block_ones"],
      w["w_out_a"], w["w_out_b"], w["w_out_c"], w["w_ff1"], w["w_ff2"])


def _w_in_columns():
    src = np.full((N_COL,), -1, np.int64)
    a_k0 = A_HEADS * HEAD_DIM
    a_v0 = a_k0 + A_KV_HEADS * HEAD_DIM
    b_qd0 = a_v0 + A_KV_HEADS * HEAD_DIM
    b_kv0 = b_qd0 + B_Q_RANK
    b_kr0 = b_kv0 + B_KV_RANK
    c0 = b_kr0 + B_ROPE
    src[OFF_AQ:OFF_AQ + 512] = np.arange(512)
    for g in range(A_KV_HEADS):
        for rep in range(2):
            lo = OFF_AK + (2 * g + rep) * HEAD_DIM
            src[lo:lo + HEAD_DIM] = a_k0 + g * HEAD_DIM + np.arange(HEAD_DIM)
    src[OFF_AV:OFF_AV + 128] = a_v0 + np.arange(128)
    src[OFF_BQD:OFF_BQD + B_Q_RANK] = b_qd0 + np.arange(B_Q_RANK)
    src[OFF_BKVD:OFF_BKVD + B_KV_RANK] = b_kv0 + np.arange(B_KV_RANK)
    src[OFF_KPE + B_NOPE:OFF_KPE + B_NOPE + B_ROPE] = b_kr0 + np.arange(B_ROPE)
    src[OFF_CQ:OFF_CQ + 5 * C_W] = c0 + np.arange(5 * C_W)
    return src


def _gather_cols(w, src):
    pieces, i, n = [], 0, len(src)
    while i < n:
        j = i + 1
        if src[i] < 0:
            while j < n and src[j] < 0:
                j += 1
            pieces.append(jnp.zeros(w.shape[:-1] + (j - i,), w.dtype))
        else:
            while j < n and src[j] == src[j - 1] + 1:
                j += 1
            pieces.append(w[..., int(src[i]):int(src[i]) + (j - i)])
        i = j
    return jnp.concatenate(pieces, axis=-1)


def _prepare_weights(p, n_mod_rows):
    depth = p["w_in"].shape[0]
    w = {"n_mod_rows": n_mod_rows}
    w["w_in"] = _gather_cols(p["w_in"], _w_in_columns()).astype(BF16)
    src = np.full((B_HEADS * LANE,), -1, np.int64)
    for hh in range(B_HEADS):
        src[hh * LANE:hh * LANE + B_NOPE + B_ROPE] = hh * (B_NOPE + B_ROPE) + np.arange(B_NOPE + B_ROPE)
    wq = _gather_cols(p["w_q_up"], src)
    w["w_q_up"] = jnp.pad(wq, ((0, 0), (0, 256 - B_Q_RANK), (0, 0))).astype(BF16)
    src = np.full((B_HEADS * LANE,), -1, np.int64)
    srcv = np.zeros((B_HEADS * B_V,), np.int64)
    for hh in range(B_HEADS):
        src[hh * LANE:hh * LANE + B_NOPE] = hh * (B_NOPE + B_V) + np.arange(B_NOPE)
        srcv[hh * B_V:(hh + 1) * B_V] = hh * (B_NOPE + B_V) + B_NOPE + np.arange(B_V)
    w["w_kv_k"] = _gather_cols(p["w_kv_up"], src).astype(BF16)
    w["w_kv_v"] = _gather_cols(p["w_kv_up"], srcv).astype(BF16)
    w["a_q_norm"] = jnp.tile(p["a_q_norm"], (1, 4))[:, None, :]
    w["a_k_norm"] = jnp.tile(p["a_k_norm"], (1, 4))[:, None, :]
    w["b_q_norm"] = jnp.pad(p["b_q_norm"], ((0, 0), (0, 256 - B_Q_RANK)))[:, None, :]
    w["b_kv_norm"] = p["b_kv_norm"][:, None, :]
    w["c_out_norm"] = jnp.tile(p["c_out_norm"], (1, C_HEADS))[:, None, :]
    for name in ("g_pre_mix", "g_post_mix", "g_pre_ffn", "g_post_ffn"):
        w[name] = p[name][:, None, :]
    head = np.arange(256) // HEAD_DIM
    same_head = head[:, None] == head[None, :]
    w["block_ones"] = jnp.asarray(same_head / float(HEAD_DIM), BF16)
    w["block_mask"] = jnp.asarray(same_head, F32)
    p_lb = jax.nn.softmax(p["c_lower_bounds"].astype(F32), axis=0)
    w["lower"] = jnp.cumsum(p_lb, axis=0) - p_lb[:1]
    w["w_out_a"] = p["w_out"][:, :A_OUT].astype(BF16)
    w["w_out_b"] = p["w_out"][:, A_OUT:A_OUT + B_OUT].astype(BF16)
    w["w_out_c"] = p["w_out"][:, A_OUT + B_OUT:].astype(BF16)
    w["w_ff1"] = p["w_ff1"].astype(BF16)
    w["w_ff2"] = p["w_ff2"].astype(BF16)
    del depth
    return w


def _rope_tables(n_tok):
    tok = np.arange(n_tok)
    row = (tok // GRID_W).astype(np.float32)[:, None]
    col = (tok % GRID_W).astype(np.float32)[:, None]

    def angles(rot_dim):
        n_freq = rot_dim // 4
        inv = jnp.asarray(ROPE_THETA, F32) ** (-jnp.arange(n_freq, dtype=F32) / n_freq)
        ang = jnp.concatenate([jnp.asarray(row) * inv, jnp.asarray(col) * inv], axis=-1)
        return jnp.cos(ang), jnp.sin(ang)

    zeros = lambda n: jnp.zeros((n_tok, n), F32)
    ones = lambda n: jnp.ones((n_tok, n), F32)
    cos, sin = angles(HEAD_DIM)
    ca = jnp.concatenate([cos, cos, cos, cos], axis=-1)
    s1a = jnp.concatenate([-sin, zeros(32), -sin, zeros(32)], axis=-1)
    s2a = jnp.concatenate([zeros(32), sin, zeros(32), sin], axis=-1)
    cos, sin = angles(B_ROPE)
    cb = jnp.concatenate([ones(64), cos, cos, ones(32)], axis=-1)
    s1b = jnp.concatenate([zeros(64), -sin, zeros(16), zeros(32)], axis=-1)
    s2b = jnp.concatenate([zeros(64), zeros(16), sin, zeros(32)], axis=-1)
    return ca, s1a, s2a, cb, s1b, s2b


def kernel(x, c, ctx, c_ctx, w_ada, b_ada, g_pre_mix, g_post_mix, g_pre_ffn, g_post_ffn, w_in, a_q_norm, a_k_norm, b_q_norm, w_q_up, b_kv_norm, w_kv_up, c_lower_bounds, c_out_norm, w_out, w_ff1, w_ff2):
    bsz, n_lat, _ = x.shape
    depth = w_in.shape[0]
    n_mod_rows = -(-(bsz + 1) // 8) * 8
    params = dict(w_in=w_in, a_q_norm=a_q_norm, a_k_norm=a_k_norm, b_q_norm=b_q_norm, w_q_up=w_q_up,
                  b_kv_norm=b_kv_norm, w_kv_up=w_kv_up, c_lower_bounds=c_lower_bounds,
                  c_out_norm=c_out_norm, w_out=w_out, w_ff1=w_ff1, w_ff2=w_ff2,
                  g_pre_mix=g_pre_mix, g_post_mix=g_post_mix, g_pre_ffn=g_pre_ffn, g_post_ffn=g_post_ffn)
    w = _prepare_weights(params, n_mod_rows)
    rope = _rope_tables(n_lat)
    scan_f = _scan_constants(False)
    scan_b = _scan_constants(True)

    cvec = jnp.concatenate([c, c_ctx[None, :], jnp.zeros((n_mod_rows - bsz - 1, D_MODEL), F32)], axis=0)
    mod = _ada(cvec, w_ada, b_ada).reshape(depth * n_mod_rows * 6, 1, D_MODEL)

    lat_row = lambda b: b
    ctx_row = lambda b: bsz
    zero_state = jnp.zeros((bsz, C_W, C_W), F32)
    attn_a = functools.partial(_attention, n_kv=A_KV_HEADS, n_heads=A_HEADS // A_KV_HEADS, shared_kv=True)
    attn_b = functools.partial(_attention, n_kv=1, n_heads=B_HEADS, shared_kv=False)

    xc = ctx
    for layer in range(depth):
        need_ctx = layer < depth - 1
        fl = _features(x, mod, w, layer, lat_row, rope)
        fc = _features(xc, mod, w, layer, ctx_row, None)
        yta = attn_a(fl["aq"], [(fl["ak"], fl["avt"]), (fc["ak"], fc["avt"])], name="attn_a")
        ytb = attn_b(fl["bq"], [(fl["bk"], fl["bvt"]), (fc["bk"], fc["bvt"])], name="attn_b")
        ocf, s_f = _hgrn(fc["c"], fc["g"], zero_state, scan_f, w["block_mask"], False)
        ocb, s_b = _hgrn(fc["c"], fc["g"], zero_state, scan_b, w["block_mask"], True)
        olf, _ = _hgrn(fl["c"], fl["g"], s_f, scan_f, w["block_mask"], False)
        olb, _ = _hgrn(fl["c"], fl["g"], s_b, scan_b, w["block_mask"], True)
        x_new = _post(x, yta, ytb, olf, olb, fl["c"], mod, w, layer, lat_row)
        if need_ctx:
            yta_c = attn_a(fc["aq"], [(fc["ak"], fc["avt"])], name="attn_a_ctx")
            ytb_c = attn_b(fc["bq"], [(fc["bk"], fc["bvt"])], name="attn_b_ctx")
            xc = _post(xc, yta_c, ytb_c, ocf, ocb, fc["c"], mod, w, layer, ctx_row)
        x = x_new
    return x
```

```python
import functools

import numpy as np
import jax
import jax.numpy as jnp
from jax import lax
from jax.experimental import pallas as pl
from jax.experimental.pallas import tpu as pltpu

F32 = jnp.float32
BF16 = jnp.bfloat16

D_MODEL = 1024
GRID_W = 64
HEAD_DIM = 64
A_HEADS = 8
A_KV_HEADS = 2
B_HEADS = 4
B_Q_RANK = 192
B_KV_RANK = 128
B_NOPE = 64
B_ROPE = 32
B_V = 64
C_HEADS = 4
C_DK = 64
C_DV = 64
D_FF = 4 * D_MODEL
A_OUT = A_HEADS * HEAD_DIM
B_OUT = B_HEADS * B_V
C_OUT = C_HEADS * C_DV
C_W = C_HEADS * C_DK
ROPE_THETA = 10000.0
EPS = 1e-6
F_TINY = 1e-30

LANE = 128
VMEM_LIMIT = 56 * 1024 * 1024

OFF_AQ = 0
OFF_AK = 512
OFF_AV = 768
OFF_BQD = 896
OFF_BKVD = 1152
OFF_KPE = 1280
OFF_CQ = 1408
OFF_CFF = 1664
OFF_CFB = 1920
OFF_CI = 2176
OFF_CG = 2432
N_COL = 2688

C_Q, C_KF, C_KB, C_V, C_GATE = range(5)
N_SLAB = 5

SCAN_C = 64
N_LEVELS = 6
SCAN_BLOCK = 2048
SCAN_GROUP = 8


def _dot(a, b):
    return jnp.dot(a, b, preferred_element_type=F32)


def _dot_nt(a, b):
    return lax.dot_general(a, b, (((1,), (1,)), ((), ())), preferred_element_type=F32)


def _dot_tn(a, b):
    return lax.dot_general(a, b, (((0,), (0,)), ((), ())), preferred_element_type=F32)


def _sigmoid_pair(z):
    e = jnp.exp(-jnp.abs(z))
    inv = 1.0 / (1.0 + e)
    small = e * inv
    pos = z >= 0
    return jnp.where(pos, inv, small), jnp.where(pos, small, inv)


def _silu(z):
    s, _ = _sigmoid_pair(z)
    return z * s


def _const_spec(shape, index):
    return pl.BlockSpec(shape, lambda *_: index, pipeline_mode=pl.Buffered(1))


def _params(sem):
    return pltpu.CompilerParams(dimension_semantics=sem, vmem_limit_bytes=VMEM_LIMIT)


def _ada_kernel(c_ref, w_ref, b_ref, o_ref):
    a = _silu(c_ref[...])
    w = w_ref[...]
    a_hi = a.astype(BF16)
    a_lo = (a - a_hi.astype(F32)).astype(BF16)
    w_hi = w.astype(BF16)
    w_lo = (w - w_hi.astype(F32)).astype(BF16)
    acc = _dot(a_hi, w_hi) + (_dot(a_hi, w_lo) + _dot(a_lo, w_hi))
    o_ref[...] = acc + b_ref[...]


def _ada(cvec, w_ada, b_ada):
    depth = w_ada.shape[0]
    rows = cvec.shape[0]
    n_blk = w_ada.shape[2] // D_MODEL
    return pl.pallas_call(
        _ada_kernel,
        out_shape=jax.ShapeDtypeStruct((depth, rows, n_blk * D_MODEL), F32),
        grid=(depth, n_blk),
        in_specs=[
            pl.BlockSpec((rows, D_MODEL), lambda l, j: (0, 0)),
            pl.BlockSpec((None, D_MODEL, D_MODEL), lambda l, j: (l, 0, j)),
            pl.BlockSpec((None, None, 1, D_MODEL), lambda l, j: (l, j, 0, 0)),
        ],
        out_specs=pl.BlockSpec((None, rows, D_MODEL), lambda l, j: (l, 0, j)),
        compiler_params=_params(("arbitrary", "arbitrary")),
        name="ada",
    )(cvec, w_ada, b_ada.reshape(depth, n_blk, 1, D_MODEL))


def _rope128(v, cos, s1, s2, half):
    up = pltpu.roll(v, LANE - half, 1)
    dn = pltpu.roll(v, half, 1)
    return v * cos + up * s1 + dn * s2


def _head_rms(v, bo, gain):
    sq = v * v
    hi = sq.astype(BF16)
    lo = (sq - hi.astype(F32)).astype(BF16)
    ms = _dot(hi, bo) + _dot(lo, bo)
    return v * lax.rsqrt(ms + EPS) * gain


FEAT_TM = 1024
FEAT_SUB = 256
KV_CHUNK = 512


def _feat_kernel(*refs, use_rope):
    (x_ref, sh_ref, sc_ref, gpre_ref, win_ref, wq_ref, wkk_ref, wkv_ref,
     aqn_ref, akn_ref, bqn_ref, bkvn_ref, bo_ref, lb_ref) = refs[:14]
    rest = refs[14:]
    if use_rope:
        ca_ref, s1a_ref, s2a_ref, cb_ref, s1b_ref, s2b_ref = rest[:6]
        rest = rest[6:]
    aq_o, ak_o, avt_o, bq_o, bk_o, bvt_o, c_o, g_o = rest
    bo = bo_ref[...]
    tm = x_ref.shape[0]
    sub = min(tm, FEAT_SUB)

    for r0 in range(0, tm, sub):
        rs = slice(r0, r0 + sub)

        def rope_a(v, rs=rs):
            if not use_rope:
                return v
            return _rope128(v, ca_ref[rs, :], s1a_ref[rs, :], s2a_ref[rs, :], HEAD_DIM // 2)

        def rope_b(v, rs=rs):
            if not use_rope:
                return v
            return _rope128(v, cb_ref[rs, :], s1b_ref[rs, :], s2b_ref[rs, :], B_ROPE // 2)

        x = x_ref[rs, :]
        ms = jnp.mean(x * x, axis=-1, keepdims=True)
        h = x * lax.rsqrt(ms + EPS) * gpre_ref[...]
        h = h * (1.0 + sc_ref[...]) + sh_ref[...]
        p = _dot(h.astype(BF16), win_ref[...])

        aqn = aqn_ref[...]
        for half in range(2):
            v = _head_rms(p[:, OFF_AQ + 256 * half:OFF_AQ + 256 * (half + 1)], bo, aqn)
            for s in range(2):
                blk = rope_a(v[:, LANE * s:LANE * (s + 1)]) * (HEAD_DIM ** -0.5 * LOG2_E)
                lo = 256 * half + LANE * s
                aq_o[lo:lo + LANE, rs] = blk.T.astype(BF16)
        v = _head_rms(p[:, OFF_AK:OFF_AK + 256], bo, akn_ref[...])
        for s in range(2):
            ak_o[rs, LANE * s:LANE * (s + 1)] = rope_a(v[:, LANE * s:LANE * (s + 1)]).astype(BF16)
        kv_w = avt_o.shape[2]
        cs = slice(r0 % kv_w, r0 % kv_w + sub)
        avt_o[r0 // kv_w, :, cs] = p[:, OFF_AV:OFF_AV + LANE].T.astype(BF16)

        bqd = p[:, OFF_BQD:OFF_BQD + 256]
        ms = jnp.sum(bqd * bqd, axis=-1, keepdims=True) * (1.0 / B_Q_RANK)
        qn = (bqd * lax.rsqrt(ms + EPS) * bqn_ref[...]).astype(BF16)
        bq = _dot(qn, wq_ref[...])
        bkvd = p[:, OFF_BKVD:OFF_BKVD + LANE]
        ms = jnp.mean(bkvd * bkvd, axis=-1, keepdims=True)
        kvn = (bkvd * lax.rsqrt(ms + EPS) * bkvn_ref[...]).astype(BF16)
        bkn = _dot(kvn, wkk_ref[...])
        bv = _dot(kvn, wkv_ref[...])
        kpe = rope_b(p[:, OFF_KPE:OFF_KPE + LANE])
        b_scale = (B_NOPE + B_ROPE) ** -0.5 * LOG2_E
        for hh in range(B_HEADS):
            sl = slice(LANE * hh, LANE * (hh + 1))
            bq_o[sl, rs] = (rope_b(bq[:, sl]) * b_scale).T.astype(BF16)
            bk_o[rs, sl] = (bkn[:, sl] + kpe).astype(BF16)
        bvt_o[r0 // kv_w, :, cs] = bv.T.astype(BF16)

        c_o[rs, C_W * C_Q:C_W * (C_Q + 1)] = _silu(p[:, OFF_CQ:OFF_CQ + C_W])
        for d, (off, ck) in enumerate(((OFF_CFF, C_KF), (OFF_CFB, C_KB))):
            lb = lb_ref[d:d + 1, :]
            sp, sn = _sigmoid_pair(p[:, off:off + C_W])
            f = lb + (1.0 - lb) * sp
            g2 = jnp.log2(jnp.maximum(f, F_TINY))
            hi = g2.astype(BF16)
            g_o[rs, C_W * 2 * d:C_W * (2 * d + 1)] = hi
            g_o[rs, C_W * (2 * d + 1):C_W * (2 * d + 2)] = (g2 - hi.astype(F32)).astype(BF16)
            c_o[rs, C_W * ck:C_W * (ck + 1)] = (1.0 - lb) * sn
        c_o[rs, C_W * C_V:C_W * (C_V + 1)] = p[:, OFF_CI:OFF_CI + C_W]
        c_o[rs, C_W * C_GATE:C_W * (C_GATE + 1)] = p[:, OFF_CG:OFF_CG + C_W]


def _features(xs, mod, w, layer, mod_row, rope):
    bsz, t, _ = xs.shape
    tm = min(FEAT_TM, t)
    nt = t // tm
    kv_w = min(KV_CHUNK, t)
    n_kv = tm // kv_w
    nb_rows = w["n_mod_rows"]

    def mod_spec(j):
        return pl.BlockSpec((None, 1, D_MODEL),
                            lambda b, i: ((layer * nb_rows + mod_row(b)) * 6 + j, 0, 0))

    in_specs = [
        pl.BlockSpec((None, tm, D_MODEL), lambda b, i: (b, i, 0)),
        mod_spec(0), mod_spec(1),
        _const_spec((None, 1, D_MODEL), (layer, 0, 0)),
        _const_spec((None, D_MODEL, N_COL), (layer, 0, 0)),
        _const_spec((None, 256, 512), (layer, 0, 0)),
        _const_spec((None, LANE, 512), (layer, 0, 0)),
        _const_spec((None, LANE, 256), (layer, 0, 0)),
        _const_spec((None, 1, 256), (layer, 0, 0)),
        _const_spec((None, 1, 256), (layer, 0, 0)),
        _const_spec((None, 1, 256), (layer, 0, 0)),
        _const_spec((None, 1, LANE), (layer, 0, 0)),
        _const_spec((256, 256), (0, 0)),
        _const_spec((None, 2, C_W), (layer, 0, 0)),
    ]
    args = [xs, mod, mod, w["g_pre_mix"], w["w_in"], w["w_q_up"], w["w_kv_k"], w["w_kv_v"],
            w["a_q_norm"], w["a_k_norm"], w["b_q_norm"], w["b_kv_norm"], w["block_ones"], w["lower"]]
    if rope is not None:
        in_specs += [pl.BlockSpec((tm, LANE), lambda b, i: (i, 0))] * 6
        args += list(rope)
    out_shape = [
        jax.ShapeDtypeStruct((bsz, 512, t), BF16),
        jax.ShapeDtypeStruct((bsz, t, 256), BF16),
        jax.ShapeDtypeStruct((bsz, t // kv_w, LANE, kv_w), BF16),
        jax.ShapeDtypeStruct((bsz, 512, t), BF16),
        jax.ShapeDtypeStruct((bsz, t, 512), BF16),
        jax.ShapeDtypeStruct((bsz, t // kv_w, 256, kv_w), BF16),
        jax.ShapeDtypeStruct((bsz, t, N_SLAB * C_W), F32),
        jax.ShapeDtypeStruct((bsz, t, 4 * C_W), BF16),
    ]
    out_specs = [
        pl.BlockSpec((None, 512, tm), lambda b, i: (b, 0, i)),
        pl.BlockSpec((None, tm, 256), lambda b, i: (b, i, 0)),
        pl.BlockSpec((None, n_kv, LANE, kv_w), lambda b, i: (b, i, 0, 0)),
        pl.BlockSpec((None, 512, tm), lambda b, i: (b, 0, i)),
        pl.BlockSpec((None, tm, 512), lambda b, i: (b, i, 0)),
        pl.BlockSpec((None, n_kv, 256, kv_w), lambda b, i: (b, i, 0, 0)),
        pl.BlockSpec((None, tm, N_SLAB * C_W), lambda b, i: (b, i, 0)),
        pl.BlockSpec((None, tm, 4 * C_W), lambda b, i: (b, i, 0)),
    ]
    outs = pl.pallas_call(
        functools.partial(_feat_kernel, use_rope=rope is not None),
        out_shape=out_shape, grid=(bsz, nt), in_specs=in_specs, out_specs=out_specs,
        compiler_params=_params(("parallel", "parallel")),
        name="feat_rope" if rope is not None else "feat_ctx",
    )(*args)
    return dict(zip(("aq", "ak", "avt", "bq", "bk", "bvt", "c", "g"), outs))


ACC_ROWS = B_V + 16
NEG_BIG = -1e30
LOG2_E = 1.4426950408889634
ATTN_TQ = 1024
SHIFT_LIMIT = 50.0


def _attn_kernel(*refs, n_src, n_heads, shared_kv, fixed_shift):
    if fixed_shift:
        shift = refs[0][0]
        refs = refs[1:]
    q_ref = refs[0]
    src_refs = refs[1:1 + 2 * n_src]
    o_ref = refs[1 + 2 * n_src]
    if fixed_shift:
        rhs_scr, acc_scr, st_scr = refs[2 + 2 * n_src:]
    else:
        rhs_scr, acc_scr, st_scr, m_scr, mx_scr = refs[2 + 2 * n_src:]
    tq = q_ref.shape[1]
    n_grp, k_dim, grp_w = rhs_scr.shape
    heads_per_grp = n_heads // n_grp

    if shared_kv:
        for h in range(n_heads):
            rhs_scr[0, :, tq * h:tq * (h + 1)] = q_ref[HEAD_DIM * h:HEAD_DIM * (h + 1), :]
    else:
        rhs_scr[...] = jnp.zeros(rhs_scr.shape, BF16)
        for h in range(n_heads):
            g, j = divmod(h, heads_per_grp)
            rhs_scr[g, LANE * j:LANE * (j + 1), tq * j:tq * (j + 1)] = q_ref[LANE * h:LANE * (h + 1), :]
    if not fixed_shift:
        m_scr[...] = jnp.full(m_scr.shape, NEG_BIG, F32)
    acc_scr[...] = jnp.zeros(acc_scr.shape, F32)

    def stage(s, c, slot):
        k_ref, vt_ref = src_refs[2 * s], src_refs[2 * s + 1]
        tk = vt_ref.shape[2]
        rows = pl.ds(pl.multiple_of(c * tk, tk), tk)
        for g in range(n_grp):
            cols = slice(grp_w * g, grp_w * (g + 1))
            kc = k_ref[rows, 0:k_dim] if shared_kv else k_ref[rows, k_dim * g:k_dim * (g + 1)]
            st = _dot(kc, rhs_scr[g])
            if fixed_shift:
                st_scr[slot, 0:tk, cols] = jnp.exp2(st - shift).astype(BF16)
            else:
                st_scr[slot, 0:tk, cols] = st
                mx_scr[slot, :, cols] = jnp.max(st, axis=0, keepdims=True)

    def consume(s, c, slot):
        vt_ref = src_refs[2 * s + 1]
        tk = vt_ref.shape[2]
        ones = jnp.ones((ACC_ROWS - B_V, tk), BF16)
        for g in range(n_grp):
            cols = slice(grp_w * g, grp_w * (g + 1))
            if fixed_shift:
                pt = st_scr[slot, 0:tk, cols]
                alpha = None
            else:
                m_old = m_scr[:, cols]
                m_new = jnp.maximum(m_old, mx_scr[slot, :, cols])
                pt = jnp.exp2(st_scr[slot, 0:tk, cols] - m_new).astype(BF16)
                alpha = jnp.exp2(m_old - m_new)
                m_scr[:, cols] = m_new
            if shared_kv and fixed_shift:
                vt = jnp.concatenate([vt_ref[c], ones], axis=0)
                acc_scr[:, cols] = acc_scr[:, cols] + _dot(vt, pt)
            elif shared_kv:
                vt = jnp.concatenate([vt_ref[c], ones], axis=0)
                acc_scr[:, cols] = alpha * acc_scr[:, cols] + _dot(vt, pt)
            else:
                for j in range(heads_per_grp):
                    h = g * heads_per_grp + j
                    hc = slice(tq * h, tq * (h + 1))
                    lc = slice(tq * j, tq * (j + 1))
                    vt = jnp.concatenate([vt_ref[c, B_V * h:B_V * (h + 1), :], ones], axis=0)
                    acc_scr[:, hc] = alpha[:, lc] * acc_scr[:, hc] + _dot(vt, pt[:, lc])

    n0 = src_refs[1].shape[0]
    n_loop = (n0 - 2) // 2 if n0 >= 4 else 0
    stage(0, 0, 0)
    if n_loop:
        def pair(i, carry):
            stage(0, 2 * i + 1, 1)
            consume(0, 2 * i, 0)
            stage(0, 2 * i + 2, 0)
            consume(0, 2 * i + 1, 1)
            return carry
        lax.fori_loop(0, n_loop, pair, 0)
    tail = [(0, c) for c in range(2 * n_loop, n0)]
    tail += [(s, c) for s in range(1, n_src) for c in range(src_refs[2 * s + 1].shape[0])]
    for i, (s, c) in enumerate(tail):
        if i + 1 < len(tail):
            stage(*tail[i + 1], (i + 1) % 2)
        consume(s, c, i % 2)

    for h in range(n_heads):
        acc = acc_scr[:, tq * h:tq * (h + 1)]
        o_ref[B_V * h:B_V * (h + 1), :] = (acc[:B_V] * (1.0 / acc[B_V:B_V + 1])).astype(BF16)


def _attention(q, srcs, *, n_kv, n_heads, shared_kv, name, shift=None):
    fixed_shift = shift is not None
    assert shared_kv or not fixed_shift
    bsz, _, t = q.shape
    tq = min(ATTN_TQ * (2 if shared_kv else 1), t)
    q_r = (HEAD_DIM if shared_kv else LANE) * n_heads
    k_w = LANE * (1 if shared_kv else n_heads)
    v_r = B_V * (1 if shared_kv else n_heads)
    rhs_shape = (1, HEAD_DIM, n_heads * tq) if shared_kv else (n_heads // 2, 2 * LANE, 2 * tq)
    in_specs = [pl.BlockSpec((None, q_r, tq), lambda b, g, i: (b, g, i))]
    args = [q]
    for k, vt in srcs:
        length = k.shape[1]
        n_chunk, tk = vt.shape[1], vt.shape[3]
        in_specs.append(pl.BlockSpec((None, length, k_w), lambda b, g, i: (b, 0, g)))
        in_specs.append(pl.BlockSpec((None, n_chunk, v_r, tk), lambda b, g, i: (b, 0, g, 0)))
        args += [k, vt]
    rows = B_V * n_heads
    width = n_heads * tq
    tk_max = max(vt.shape[3] for _, vt in srcs)
    scratch = [pltpu.VMEM(rhs_shape, BF16), pltpu.VMEM((ACC_ROWS, width), F32)]
    if fixed_shift:
        in_specs = [pl.BlockSpec(memory_space=pltpu.SMEM)] + in_specs
        args = [shift] + args
        scratch += [pltpu.VMEM((2, tk_max, width), BF16)]
    else:
        scratch += [pltpu.VMEM((2, tk_max, width), F32), pltpu.VMEM((1, width), F32),
                    pltpu.VMEM((2, 1, width), F32)]
    return pl.pallas_call(
        functools.partial(_attn_kernel, n_src=len(srcs), n_heads=n_heads, shared_kv=shared_kv,
                          fixed_shift=fixed_shift),
        out_shape=jax.ShapeDtypeStruct((bsz, rows * n_kv, t), BF16),
        grid=(bsz, n_kv, t // tq),
        in_specs=in_specs,
        out_specs=pl.BlockSpec((None, rows, tq), lambda b, g, i: (b, g, i)),
        scratch_shapes=scratch,
        compiler_params=_params(("parallel", "parallel", "parallel")),
        name=name,
    )(*args)


def _scan_constants(reverse):
    c = SCAN_C
    t = np.arange(c)[:, None]
    s = np.arange(c)[None, :]
    cum = (s >= t) if reverse else (s <= t)
    msk = np.zeros((N_LEVELS + 1, c, c), np.float32)
    for lvl in range(N_LEVELS):
        w = (c // 2) >> lvl
        same = (t // (2 * w)) == (s // (2 * w))
        t_hi = (t % (2 * w)) >= w
        s_hi = (s % (2 * w)) >= w
        msk[lvl] = (same & ~t_hi & s_hi) if reverse else (same & t_hi & ~s_hi)
    msk[N_LEVELS] = (t == s)
    return jnp.asarray(cum, BF16), jnp.asarray(np.tile(msk, (1, 1, C_HEADS)), F32)


def _boundary_rows(b, lvl, reverse):
    w = (SCAN_C // 2) >> lvl
    width = b.shape[1]
    off = w if reverse else w - 1
    if w == 1:
        odd = lax.broadcasted_iota(jnp.int32, b.shape, 0) % 2 == 1
        if reverse:
            return jnp.where(odd, b, pltpu.roll(b, SCAN_C - 1, 0))
        return jnp.where(odd, pltpu.roll(b, 1, 0), b)
    if w == 2:
        low = lax.broadcasted_iota(jnp.int32, (8, width), 0) < 4
        pieces = []
        for r0 in range(0, SCAN_C, 8):
            first = jnp.broadcast_to(b[r0 + off:r0 + off + 1, :], (8, width))
            second = jnp.broadcast_to(b[r0 + 4 + off:r0 + 5 + off, :], (8, width))
            pieces.append(jnp.where(low, first, second))
        return jnp.concatenate(pieces, axis=0)
    pieces = [jnp.broadcast_to(b[r0 + off:r0 + off + 1, :], (2 * w, width))
              for r0 in range(0, SCAN_C, 2 * w)]
    return pieces[0] if len(pieces) == 1 else jnp.concatenate(pieces, axis=0)


def _hgrn_kernel(q_ref, k_ref, ghi_ref, glo_ref, v_ref, s0_ref, cum_ref, msk_ref, bm_ref,
                 o_ref, sfin_ref, st_ref, kt_ref, *, reverse, n_chunk):
    i = pl.program_id(1)

    @pl.when(i == 0)
    def _():
        st_ref[...] = s0_ref[...]

    bm = bm_ref[...]
    bm16 = bm.astype(BF16)
    cum = cum_ref[...]

    def stack_heads(a):
        a16 = a.astype(BF16)
        return jnp.concatenate([a16] * C_HEADS, axis=0) * bm16

    grp = min(n_chunk, SCAN_GROUP)
    n_grp = n_chunk // grp

    def group(jg, carry):
        jgg = (n_grp - 1 - jg) if reverse else jg
        base = jgg * (grp * SCAN_C)
        order = list(range(grp - 1, -1, -1)) if reverse else list(range(grp))
        rows = [pl.ds(pl.multiple_of(base + c * SCAN_C, SCAN_C), SCAN_C) for c in range(grp)]
        q = [q_ref[r, :] for r in rows]
        k = [k_ref[r, :] for r in rows]
        v = [v_ref[r, :] for r in rows]
        b = [_dot(cum, ghi_ref[r, :]) + _dot(cum, glo_ref[r, :]) for r in rows]
        b_all = [bc[0:1, :] if reverse else bc[SCAN_C - 1:SCAN_C, :] for bc in b]

        q16 = [qc.astype(BF16) for qc in q]
        kst = [stack_heads(kc) for kc in k]
        sc = [msk_ref[N_LEVELS] * _dot_nt(q16[c], kst[c]) for c in range(grp)]
        for lvl in range(N_LEVELS):
            for c in range(grp):
                d = b[c] - _boundary_rows(b[c], lvl, reverse)
                e16 = jnp.exp2(-jnp.abs(d)).astype(BF16)
                kl = kst[c] * jnp.concatenate([e16] * C_HEADS, axis=0)
                kt_ref[c] = kl.T
                sc[c] = sc[c] + msk_ref[lvl] * _dot(q16[c] * e16, kt_ref[c])

        ds = [_dot_tn(v[c].astype(BF16), (k[c] * jnp.exp2(b_all[c] - b[c])).astype(BF16)) * bm
              for c in range(grp)]
        st = st_ref[...]
        st_in = [None] * grp
        for c in order:
            st_in[c] = st
            st = st * jnp.exp2(b_all[c]) + ds[c]
        st_ref[...] = st

        for c in order:
            qdec = q[c] * jnp.exp2(b[c])
            o_ref[rows[c], :] = (_dot(sc[c].astype(BF16), stack_heads(v[c]))
                                 + _dot_nt(qdec.astype(BF16), st_in[c].astype(BF16)))
        return carry

    if n_grp == 1:
        group(0, 0)
    else:
        lax.fori_loop(0, n_grp, group, 0)

    @pl.when(i == pl.num_programs(1) - 1)
    def _():
        sfin_ref[...] = st_ref[...]


def _hgrn(c_slab, g16, s0, consts, block_mask, reverse):
    bsz, t, _ = c_slab.shape
    tb = min(SCAN_BLOCK, t)
    nblk = t // tb
    cum, msk = consts

    def blk(i):
        return (nblk - 1 - i) if reverse else i

    def slab_spec(j):
        return pl.BlockSpec((None, tb, C_W), lambda b, i: (b, blk(i), j))

    g0 = 2 if reverse else 0

    return pl.pallas_call(
        functools.partial(_hgrn_kernel, reverse=reverse, n_chunk=tb // SCAN_C),
        out_shape=[jax.ShapeDtypeStruct((bsz, t, C_W), F32),
                   jax.ShapeDtypeStruct((bsz, C_W, C_W), F32)],
        grid=(bsz, nblk),
        in_specs=[
            slab_spec(C_Q), slab_spec(C_KB if reverse else C_KF), slab_spec(g0), slab_spec(g0 + 1),
            slab_spec(C_V),
            pl.BlockSpec((None, C_W, C_W), lambda b, i: (b, 0, 0)),
            _const_spec(cum.shape, (0, 0)),
            _const_spec(msk.shape, (0, 0, 0)),
            _const_spec(block_mask.shape, (0, 0)),
        ],
        out_specs=[pl.BlockSpec((None, tb, C_W), lambda b, i: (b, blk(i), 0)),
                   pl.BlockSpec((None, C_W, C_W), lambda b, i: (b, 0, 0))],
        scratch_shapes=[pltpu.VMEM((C_W, C_W), F32),
                        pltpu.VMEM((min(tb // SCAN_C, SCAN_GROUP), C_W, C_HEADS * SCAN_C), BF16)],
        compiler_params=_params(("parallel", "arbitrary")),
        name="hgrn_bwd" if reverse else "hgrn_fwd",
    )(c_slab, c_slab, g16, g16, c_slab, s0, cum, msk, block_mask)


POST_TM = 1024
POST_SUB = 256


def _rms(v, gain):
    ms = jnp.mean(v * v, axis=-1, keepdims=True)
    return v * lax.rsqrt(ms + EPS) * gain


def _post_kernel(x_ref, ya_ref, yb_ref, of_ref, ob_ref, gate_ref,
                 gtm_ref, shf_ref, scf_ref, gtf_ref,
                 gpm_ref, gpf_ref, gqf_ref, con_ref, bo_ref,
                 woa_ref, wob_ref, woc_ref, w1_ref, w2_ref, o_ref):
    tm = x_ref.shape[0]
    sub = min(tm, POST_SUB)
    for r0 in range(0, tm, sub):
        rs = slice(r0, r0 + sub)
        o = of_ref[rs, :] + ob_ref[rs, :]
        yc = (_head_rms(o, bo_ref[...], con_ref[...]) * _silu(gate_ref[rs, :])).astype(BF16)
        mix = (_dot_tn(ya_ref[:, rs], woa_ref[...]) + _dot_tn(yb_ref[:, rs], wob_ref[...])
               + _dot(yc, woc_ref[...]))
        x1 = x_ref[rs, :] + gtm_ref[...] * _rms(mix, gpm_ref[...])
        h = _rms(x1, gpf_ref[...]) * (1.0 + scf_ref[...]) + shf_ref[...]
        u = jnp.maximum(_dot(h.astype(BF16), w1_ref[...]), 0.0)
        ff = _dot((u * u).astype(BF16), w2_ref[...])
        o_ref[rs, :] = x1 + gtf_ref[...] * _rms(ff, gqf_ref[...])


def _post(xs, yta, ytb, o_f, o_b, c_slab, mod, w, layer, mod_row):
    bsz, t, _ = xs.shape
    tm = min(POST_TM, t)
    nb_rows = w["n_mod_rows"]

    def mod_spec(j):
        return pl.BlockSpec((None, 1, D_MODEL),
                            lambda b, i: ((layer * nb_rows + mod_row(b)) * 6 + j, 0, 0))

    def vec_spec():
        return _const_spec((None, 1, D_MODEL), (layer, 0, 0))

    in_specs = [
        pl.BlockSpec((None, tm, D_MODEL), lambda b, i: (b, i, 0)),
        pl.BlockSpec((None, A_OUT, tm), lambda b, i: (b, 0, i)),
        pl.BlockSpec((None, B_OUT, tm), lambda b, i: (b, 0, i)),
        pl.BlockSpec((None, tm, C_W), lambda b, i: (b, i, 0)),
        pl.BlockSpec((None, tm, C_W), lambda b, i: (b, i, 0)),
        pl.BlockSpec((None, tm, C_W), lambda b, i: (b, i, C_GATE)),
        mod_spec(2), mod_spec(3), mod_spec(4), mod_spec(5),
        vec_spec(), vec_spec(), vec_spec(),
        _const_spec((None, 1, C_W), (layer, 0, 0)),
        _const_spec((256, 256), (0, 0)),
        _const_spec((None, A_OUT, D_MODEL), (layer, 0, 0)),
        _const_spec((None, B_OUT, D_MODEL), (layer, 0, 0)),
        _const_spec((None, C_OUT, D_MODEL), (layer, 0, 0)),
        _const_spec((None, D_MODEL, D_FF), (layer, 0, 0)),
        _const_spec((None, D_FF, D_MODEL), (layer, 0, 0)),
    ]
    return pl.pallas_call(
        _post_kernel,
        out_shape=jax.ShapeDtypeStruct((bsz, t, D_MODEL), F32),
        grid=(bsz, t // tm),
        in_specs=in_specs,
        out_specs=pl.BlockSpec((None, tm, D_MODEL), lambda b, i: (b, i, 0)),
        compiler_params=_params(("parallel", "parallel")),
        name="post",
    )(xs, yta, ytb, o_f, o_b, c_slab, mod, mod, mod, mod,
      w["g_post_mix"], w["g_pre_ffn"], w["g_post_ffn"], w["c_out_norm"], w["block_ones"],
      w["w_out_a"], w["w_out_b"], w["w_out_c"], w["w_ff1"], w["w_ff2"])


def _w_in_columns():
    src = np.full((N_COL,), -1, np.int64)
    a_k0 = A_HEADS * HEAD_DIM
    a_v0 = a_k0 + A_KV_HEADS * HEAD_DIM
    b_qd0 = a_v0 + A_KV_HEADS * HEAD_DIM
    b_kv0 = b_qd0 + B_Q_RANK
    b_kr0 = b_kv0 + B_KV_RANK
    c0 = b_kr0 + B_ROPE
    src[OFF_AQ:OFF_AQ + 512] = np.arange(512)
    for g in range(A_KV_HEADS):
        for rep in range(2):
            lo = OFF_AK + (2 * g + rep) * HEAD_DIM
            src[lo:lo + HEAD_DIM] = a_k0 + g * HEAD_DIM + np.arange(HEAD_DIM)
    src[OFF_AV:OFF_AV + 128] = a_v0 + np.arange(128)
    src[OFF_BQD:OFF_BQD + B_Q_RANK] = b_qd0 + np.arange(B_Q_RANK)
    src[OFF_BKVD:OFF_BKVD + B_KV_RANK] = b_kv0 + np.arange(B_KV_RANK)
    src[OFF_KPE + B_NOPE:OFF_KPE + B_NOPE + B_ROPE] = b_kr0 + np.arange(B_ROPE)
    src[OFF_CQ:OFF_CQ + 5 * C_W] = c0 + np.arange(5 * C_W)
    return src


def _gather_cols(w, src):
    pieces, i, n = [], 0, len(src)
    while i < n:
        j = i + 1
        if src[i] < 0:
            while j < n and src[j] < 0:
                j += 1
            pieces.append(jnp.zeros(w.shape[:-1] + (j - i,), w.dtype))
        else:
            while j < n and src[j] == src[j - 1] + 1:
                j += 1
            pieces.append(w[..., int(src[i]):int(src[i]) + (j - i)])
        i = j
    return jnp.concatenate(pieces, axis=-1)


def _prepare_weights(p, n_mod_rows):
    depth = p["w_in"].shape[0]
    w = {"n_mod_rows": n_mod_rows}
    w["w_in"] = _gather_cols(p["w_in"], _w_in_columns()).astype(BF16)
    src = np.full((B_HEADS * LANE,), -1, np.int64)
    for hh in range(B_HEADS):
        src[hh * LANE:hh * LANE + B_NOPE + B_ROPE] = hh * (B_NOPE + B_ROPE) + np.arange(B_NOPE + B_ROPE)
    wq = _gather_cols(p["w_q_up"], src)
    w["w_q_up"] = jnp.pad(wq, ((0, 0), (0, 256 - B_Q_RANK), (0, 0))).astype(BF16)
    src = np.full((B_HEADS * LANE,), -1, np.int64)
    srcv = np.zeros((B_HEADS * B_V,), np.int64)
    for hh in range(B_HEADS):
        src[hh * LANE:hh * LANE + B_NOPE] = hh * (B_NOPE + B_V) + np.arange(B_NOPE)
        srcv[hh * B_V:(hh + 1) * B_V] = hh * (B_NOPE + B_V) + B_NOPE + np.arange(B_V)
    w["w_kv_k"] = _gather_cols(p["w_kv_up"], src).astype(BF16)
    w["w_kv_v"] = _gather_cols(p["w_kv_up"], srcv).astype(BF16)
    w["a_q_norm"] = jnp.tile(p["a_q_norm"], (1, 4))[:, None, :]
    w["a_k_norm"] = jnp.tile(p["a_k_norm"], (1, 4))[:, None, :]
    w["b_q_norm"] = jnp.pad(p["b_q_norm"], ((0, 0), (0, 256 - B_Q_RANK)))[:, None, :]
    w["b_kv_norm"] = p["b_kv_norm"][:, None, :]
    w["c_out_norm"] = jnp.tile(p["c_out_norm"], (1, C_HEADS))[:, None, :]
    for name in ("g_pre_mix", "g_post_mix", "g_pre_ffn", "g_post_ffn"):
        w[name] = p[name][:, None, :]
    head = np.arange(256) // HEAD_DIM
    same_head = head[:, None] == head[None, :]
    w["block_ones"] = jnp.asarray(same_head / float(HEAD_DIM), BF16)
    w["block_mask"] = jnp.asarray(same_head, F32)
    p_lb = jax.nn.softmax(p["c_lower_bounds"].astype(F32), axis=0)
    w["lower"] = jnp.cumsum(p_lb, axis=0) - p_lb[:1]
    w["w_out_a"] = p["w_out"][:, :A_OUT].astype(BF16)
    w["w_out_b"] = p["w_out"][:, A_OUT:A_OUT + B_OUT].astype(BF16)
    w["w_out_c"] = p["w_out"][:, A_OUT + B_OUT:].astype(BF16)
    w["w_ff1"] = p["w_ff1"].astype(BF16)
    w["w_ff2"] = p["w_ff2"].astype(BF16)
    del depth
    return w


def _rope_tables(n_tok):
    tok = np.arange(n_tok)
    row = (tok // GRID_W).astype(np.float32)[:, None]
    col = (tok % GRID_W).astype(np.float32)[:, None]

    def angles(rot_dim):
        n_freq = rot_dim // 4
        inv = jnp.asarray(ROPE_THETA, F32) ** (-jnp.arange(n_freq, dtype=F32) / n_freq)
        ang = jnp.concatenate([jnp.asarray(row) * inv, jnp.asarray(col) * inv], axis=-1)
        return jnp.cos(ang), jnp.sin(ang)

    zeros = lambda n: jnp.zeros((n_tok, n), F32)
    ones = lambda n: jnp.ones((n_tok, n), F32)
    cos, sin = angles(HEAD_DIM)
    ca = jnp.concatenate([cos, cos, cos, cos], axis=-1)
    s1a = jnp.concatenate([-sin, zeros(32), -sin, zeros(32)], axis=-1)
    s2a = jnp.concatenate([zeros(32), sin, zeros(32), sin], axis=-1)
    cos, sin = angles(B_ROPE)
    cb = jnp.concatenate([ones(64), cos, cos, ones(32)], axis=-1)
    s1b = jnp.concatenate([zeros(64), -sin, zeros(16), zeros(32)], axis=-1)
    s2b = jnp.concatenate([zeros(64), zeros(16), sin, zeros(32)], axis=-1)
    return ca, s1a, s2a, cb, s1b, s2b


def kernel(x, c, ctx, c_ctx, w_ada, b_ada, g_pre_mix, g_post_mix, g_pre_ffn, g_post_ffn, w_in, a_q_norm, a_k_norm, b_q_norm, w_q_up, b_kv_norm, w_kv_up, c_lower_bounds, c_out_norm, w_out, w_ff1, w_ff2):
    bsz, n_lat, _ = x.shape
    depth = w_in.shape[0]
    n_mod_rows = -(-(bsz + 1) // 8) * 8
    params = dict(w_in=w_in, a_q_norm=a_q_norm, a_k_norm=a_k_norm, b_q_norm=b_q_norm, w_q_up=w_q_up,
                  b_kv_norm=b_kv_norm, w_kv_up=w_kv_up, c_lower_bounds=c_lower_bounds,
                  c_out_norm=c_out_norm, w_out=w_out, w_ff1=w_ff1, w_ff2=w_ff2,
                  g_pre_mix=g_pre_mix, g_post_mix=g_post_mix, g_pre_ffn=g_pre_ffn, g_post_ffn=g_post_ffn)
    w = _prepare_weights(params, n_mod_rows)
    rope = _rope_tables(n_lat)
    scan_f = _scan_constants(False)
    scan_b = _scan_constants(True)

    cvec = jnp.concatenate([c, c_ctx[None, :], jnp.zeros((n_mod_rows - bsz - 1, D_MODEL), F32)], axis=0)
    mod = _ada(cvec, w_ada, b_ada).reshape(depth * n_mod_rows * 6, 1, D_MODEL)

    lat_row = lambda b: b
    ctx_row = lambda b: bsz
    zero_state = jnp.zeros((bsz, C_W, C_W), F32)
    attn_a = functools.partial(_attention, n_kv=A_KV_HEADS, n_heads=A_HEADS // A_KV_HEADS, shared_kv=True)
    attn_b = functools.partial(_attention, n_kv=1, n_heads=B_HEADS, shared_kv=False)

    xc = ctx
    for layer in range(depth):
        need_ctx = layer < depth - 1
        fl = _features(x, mod, w, layer, lat_row, rope)
        fc = _features(xc, mod, w, layer, ctx_row, None)
        bound = (HEAD_DIM ** 0.5 * LOG2_E) * jnp.max(jnp.abs(a_q_norm[layer])) * jnp.max(jnp.abs(a_k_norm[layer]))
        a_srcs = [(fl["ak"], fl["avt"]), (fc["ak"], fc["avt"])]
        yta = lax.cond(
            bound <= SHIFT_LIMIT,
            lambda q, s, bnd: attn_a(q, s, name="attn_a_fixed", shift=bnd.reshape(1)),
            lambda q, s, bnd: attn_a(q, s, name="attn_a"),
            fl["aq"], a_srcs, bound)
        ytb = attn_b(fl["bq"], [(fl["bk"], fl["bvt"]), (fc["bk"], fc["bvt"])], name="attn_b")
        ocf, s_f = _hgrn(fc["c"], fc["g"], zero_state, scan_f, w["block_mask"], False)
        ocb, s_b = _hgrn(fc["c"], fc["g"], zero_state, scan_b, w["block_mask"], True)
        olf, _ = _hgrn(fl["c"], fl["g"], s_f, scan_f, w["block_mask"], False)
        olb, _ = _hgrn(fl["c"], fl["g"], s_b, scan_b, w["block_mask"], True)
        x_new = _post(x, yta, ytb, olf, olb, fl["c"], mod, w, layer, lat_row)
        if need_ctx:
            yta_c = attn_a(fc["aq"], [(fc["ak"], fc["avt"])], name="attn_a_ctx")
            ytb_c = attn_b(fc["bq"], [(fc["bk"], fc["bvt"])], name="attn_b_ctx")
            xc = _post(xc, yta_c, ytb_c, ocf, ocb, fc["c"], mod, w, layer, ctx_row)
        x = x_new
    return x
```

```python
import functools

import numpy as np
import jax
import jax.numpy as jnp
from jax import lax
from jax.experimental import pallas as pl
from jax.experimental.pallas import tpu as pltpu

F32 = jnp.float32
BF16 = jnp.bfloat16

D_MODEL = 1024
GRID_W = 64
HEAD_DIM = 64
A_HEADS = 8
A_KV_HEADS = 2
B_HEADS = 4
B_Q_RANK = 192
B_KV_RANK = 128
B_NOPE = 64
B_ROPE = 32
B_V = 64
C_HEADS = 4
C_DK = 64
C_DV = 64
D_FF = 4 * D_MODEL
A_OUT = A_HEADS * HEAD_DIM
B_OUT = B_HEADS * B_V
C_OUT = C_HEADS * C_DV
C_W = C_HEADS * C_DK
ROPE_THETA = 10000.0
EPS = 1e-6
F_TINY = 1e-30

LANE = 128
VMEM_LIMIT = 56 * 1024 * 1024

OFF_AQ = 0
OFF_AK = 512
OFF_AV = 768
OFF_BQD = 896
OFF_BKVD = 1152
OFF_KPE = 1280
OFF_CQ = 1408
OFF_CFF = 1664
OFF_CFB = 1920
OFF_CI = 2176
OFF_CG = 2432
N_COL = 2688

C_Q, C_KF, C_KB, C_V, C_GATE = range(5)
N_SLAB = 5

SCAN_C = 64
N_LEVELS = 6
SCAN_BLOCK = 2048
SCAN_GROUP = 8


def _dot(a, b):
    return jnp.dot(a, b, preferred_element_type=F32)


def _dot_nt(a, b):
    return lax.dot_general(a, b, (((1,), (1,)), ((), ())), preferred_element_type=F32)


def _dot_tn(a, b):
    return lax.dot_general(a, b, (((0,), (0,)), ((), ())), preferred_element_type=F32)


def _sigmoid_pair(z):
    e = jnp.exp(-jnp.abs(z))
    inv = 1.0 / (1.0 + e)
    small = e * inv
    pos = z >= 0
    return jnp.where(pos, inv, small), jnp.where(pos, small, inv)


def _silu(z):
    s, _ = _sigmoid_pair(z)
    return z * s


def _const_spec(shape, index):
    return pl.BlockSpec(shape, lambda *_: index, pipeline_mode=pl.Buffered(1))


def _params(sem):
    return pltpu.CompilerParams(dimension_semantics=sem, vmem_limit_bytes=VMEM_LIMIT)


def _ada_kernel(c_ref, w_ref, b_ref, o_ref):
    a = _silu(c_ref[...])
    w = w_ref[...]
    a_hi = a.astype(BF16)
    a_lo = (a - a_hi.astype(F32)).astype(BF16)
    w_hi = w.astype(BF16)
    w_lo = (w - w_hi.astype(F32)).astype(BF16)
    acc = _dot(a_hi, w_hi) + (_dot(a_hi, w_lo) + _dot(a_lo, w_hi))
    o_ref[...] = acc + b_ref[...]


def _ada(cvec, w_ada, b_ada):
    depth = w_ada.shape[0]
    rows = cvec.shape[0]
    n_blk = w_ada.shape[2] // D_MODEL
    return pl.pallas_call(
        _ada_kernel,
        out_shape=jax.ShapeDtypeStruct((depth, rows, n_blk * D_MODEL), F32),
        grid=(depth, n_blk),
        in_specs=[
            pl.BlockSpec((rows, D_MODEL), lambda l, j: (0, 0)),
            pl.BlockSpec((None, D_MODEL, D_MODEL), lambda l, j: (l, 0, j)),
            pl.BlockSpec((None, None, 1, D_MODEL), lambda l, j: (l, j, 0, 0)),
        ],
        out_specs=pl.BlockSpec((None, rows, D_MODEL), lambda l, j: (l, 0, j)),
        compiler_params=_params(("arbitrary", "arbitrary")),
        name="ada",
    )(cvec, w_ada, b_ada.reshape(depth, n_blk, 1, D_MODEL))


def _rope128(v, cos, s1, s2, half):
    up = pltpu.roll(v, LANE - half, 1)
    dn = pltpu.roll(v, half, 1)
    return v * cos + up * s1 + dn * s2


def _head_rms(v, bo, gain):
    sq = v * v
    hi = sq.astype(BF16)
    lo = (sq - hi.astype(F32)).astype(BF16)
    ms = _dot(hi, bo) + _dot(lo, bo)
    return v * lax.rsqrt(ms + EPS) * gain


FEAT_TM = 1024
FEAT_SUB = 256
KV_CHUNK = 512


def _feat_kernel(*refs, use_rope):
    (x_ref, sh_ref, sc_ref, gpre_ref, win_ref, wq_ref, wkk_ref, wkv_ref,
     aqn_ref, akn_ref, bqn_ref, bkvn_ref, bo_ref, lb_ref) = refs[:14]
    rest = refs[14:]
    if use_rope:
        ca_ref, s1a_ref, s2a_ref, cb_ref, s1b_ref, s2b_ref = rest[:6]
        rest = rest[6:]
    aq_o, ak_o, avt_o, bq_o, bk_o, bvt_o, c_o, g_o, bqn_o, bkn_o = rest
    bo = bo_ref[...]
    tm = x_ref.shape[0]
    sub = min(tm, FEAT_SUB)
    b_k2 = [None] * B_HEADS
    bqn_o[B_HEADS:, :] = jnp.zeros((bqn_o.shape[0] - B_HEADS, tm), F32)

    for r0 in range(0, tm, sub):
        rs = slice(r0, r0 + sub)

        def rope_a(v, rs=rs):
            if not use_rope:
                return v
            return _rope128(v, ca_ref[rs, :], s1a_ref[rs, :], s2a_ref[rs, :], HEAD_DIM // 2)

        def rope_b(v, rs=rs):
            if not use_rope:
                return v
            return _rope128(v, cb_ref[rs, :], s1b_ref[rs, :], s2b_ref[rs, :], B_ROPE // 2)

        x = x_ref[rs, :]
        ms = jnp.mean(x * x, axis=-1, keepdims=True)
        h = x * lax.rsqrt(ms + EPS) * gpre_ref[...]
        h = h * (1.0 + sc_ref[...]) + sh_ref[...]
        p = _dot(h.astype(BF16), win_ref[...])

        aqn = aqn_ref[...]
        for half in range(2):
            v = _head_rms(p[:, OFF_AQ + 256 * half:OFF_AQ + 256 * (half + 1)], bo, aqn)
            for s in range(2):
                blk = rope_a(v[:, LANE * s:LANE * (s + 1)]) * (HEAD_DIM ** -0.5 * LOG2_E)
                lo = 256 * half + LANE * s
                aq_o[lo:lo + LANE, rs] = blk.T.astype(BF16)
        v = _head_rms(p[:, OFF_AK:OFF_AK + 256], bo, akn_ref[...])
        for s in range(2):
            ak_o[rs, LANE * s:LANE * (s + 1)] = rope_a(v[:, LANE * s:LANE * (s + 1)]).astype(BF16)
        kv_w = avt_o.shape[2]
        cs = slice(r0 % kv_w, r0 % kv_w + sub)
        avt_o[r0 // kv_w, :, cs] = p[:, OFF_AV:OFF_AV + LANE].T.astype(BF16)

        bqd = p[:, OFF_BQD:OFF_BQD + 256]
        ms = jnp.sum(bqd * bqd, axis=-1, keepdims=True) * (1.0 / B_Q_RANK)
        qn = (bqd * lax.rsqrt(ms + EPS) * bqn_ref[...]).astype(BF16)
        bq = _dot(qn, wq_ref[...])
        bkvd = p[:, OFF_BKVD:OFF_BKVD + LANE]
        ms = jnp.mean(bkvd * bkvd, axis=-1, keepdims=True)
        kvn = (bkvd * lax.rsqrt(ms + EPS) * bkvn_ref[...]).astype(BF16)
        bkn = _dot(kvn, wkk_ref[...])
        bv = _dot(kvn, wkv_ref[...])
        kpe = rope_b(p[:, OFF_KPE:OFF_KPE + LANE])
        b_scale = (B_NOPE + B_ROPE) ** -0.5 * LOG2_E
        for hh in range(B_HEADS):
            sl = slice(LANE * hh, LANE * (hh + 1))
            qt = (rope_b(bq[:, sl]) * b_scale).T
            kh = bkn[:, sl] + kpe
            bq_o[sl, rs] = qt.astype(BF16)
            bk_o[rs, sl] = kh.astype(BF16)
            bqn_o[hh:hh + 1, rs] = jnp.sum(qt * qt, axis=0, keepdims=True)
            k2 = jnp.max(jnp.sum(kh * kh, axis=-1, keepdims=True), axis=0, keepdims=True)
            b_k2[hh] = k2 if b_k2[hh] is None else jnp.maximum(b_k2[hh], k2)
        bvt_o[r0 // kv_w, :, cs] = bv.T.astype(BF16)

        c_o[rs, C_W * C_Q:C_W * (C_Q + 1)] = _silu(p[:, OFF_CQ:OFF_CQ + C_W])
        for d, (off, ck) in enumerate(((OFF_CFF, C_KF), (OFF_CFB, C_KB))):
            lb = lb_ref[d:d + 1, :]
            sp, sn = _sigmoid_pair(p[:, off:off + C_W])
            f = lb + (1.0 - lb) * sp
            g2 = jnp.log2(jnp.maximum(f, F_TINY))
            hi = g2.astype(BF16)
            g_o[rs, C_W * 2 * d:C_W * (2 * d + 1)] = hi
            g_o[rs, C_W * (2 * d + 1):C_W * (2 * d + 2)] = (g2 - hi.astype(F32)).astype(BF16)
            c_o[rs, C_W * ck:C_W * (ck + 1)] = (1.0 - lb) * sn
        c_o[rs, C_W * C_V:C_W * (C_V + 1)] = p[:, OFF_CI:OFF_CI + C_W]
        c_o[rs, C_W * C_GATE:C_W * (C_GATE + 1)] = p[:, OFF_CG:OFF_CG + C_W]

    lane = lax.broadcasted_iota(jnp.int32, bkn_o.shape, 1)
    row = jnp.zeros(bkn_o.shape, F32)
    for hh in range(B_HEADS):
        row = jnp.where(lane == hh, jnp.broadcast_to(b_k2[hh], bkn_o.shape), row)
    bkn_o[...] = row


def _features(xs, mod, w, layer, mod_row, rope):
    bsz, t, _ = xs.shape
    tm = min(FEAT_TM, t)
    nt = t // tm
    kv_w = min(KV_CHUNK, t)
    n_kv = tm // kv_w
    nb_rows = w["n_mod_rows"]

    def mod_spec(j):
        return pl.BlockSpec((None, 1, D_MODEL),
                            lambda b, i: ((layer * nb_rows + mod_row(b)) * 6 + j, 0, 0))

    in_specs = [
        pl.BlockSpec((None, tm, D_MODEL), lambda b, i: (b, i, 0)),
        mod_spec(0), mod_spec(1),
        _const_spec((None, 1, D_MODEL), (layer, 0, 0)),
        _const_spec((None, D_MODEL, N_COL), (layer, 0, 0)),
        _const_spec((None, 256, 512), (layer, 0, 0)),
        _const_spec((None, LANE, 512), (layer, 0, 0)),
        _const_spec((None, LANE, 256), (layer, 0, 0)),
        _const_spec((None, 1, 256), (layer, 0, 0)),
        _const_spec((None, 1, 256), (layer, 0, 0)),
        _const_spec((None, 1, 256), (layer, 0, 0)),
        _const_spec((None, 1, LANE), (layer, 0, 0)),
        _const_spec((256, 256), (0, 0)),
        _const_spec((None, 2, C_W), (layer, 0, 0)),
    ]
    args = [xs, mod, mod, w["g_pre_mix"], w["w_in"], w["w_q_up"], w["w_kv_k"], w["w_kv_v"],
            w["a_q_norm"], w["a_k_norm"], w["b_q_norm"], w["b_kv_norm"], w["block_ones"], w["lower"]]
    if rope is not None:
        in_specs += [pl.BlockSpec((tm, LANE), lambda b, i: (i, 0))] * 6
        args += list(rope)
    out_shape = [
        jax.ShapeDtypeStruct((bsz, 512, t), BF16),
        jax.ShapeDtypeStruct((bsz, t, 256), BF16),
        jax.ShapeDtypeStruct((bsz, t // kv_w, LANE, kv_w), BF16),
        jax.ShapeDtypeStruct((bsz, 512, t), BF16),
        jax.ShapeDtypeStruct((bsz, t, 512), BF16),
        jax.ShapeDtypeStruct((bsz, t // kv_w, 256, kv_w), BF16),
        jax.ShapeDtypeStruct((bsz, t, N_SLAB * C_W), F32),
        jax.ShapeDtypeStruct((bsz, t, 4 * C_W), BF16),
        jax.ShapeDtypeStruct((bsz, 8, t), F32),
        jax.ShapeDtypeStruct((bsz, nt, 8, LANE), F32),
    ]
    out_specs = [
        pl.BlockSpec((None, 512, tm), lambda b, i: (b, 0, i)),
        pl.BlockSpec((None, tm, 256), lambda b, i: (b, i, 0)),
        pl.BlockSpec((None, n_kv, LANE, kv_w), lambda b, i: (b, i, 0, 0)),
        pl.BlockSpec((None, 512, tm), lambda b, i: (b, 0, i)),
        pl.BlockSpec((None, tm, 512), lambda b, i: (b, i, 0)),
        pl.BlockSpec((None, n_kv, 256, kv_w), lambda b, i: (b, i, 0, 0)),
        pl.BlockSpec((None, tm, N_SLAB * C_W), lambda b, i: (b, i, 0)),
        pl.BlockSpec((None, tm, 4 * C_W), lambda b, i: (b, i, 0)),
        pl.BlockSpec((None, 8, tm), lambda b, i: (b, 0, i)),
        pl.BlockSpec((None, None, 8, LANE), lambda b, i: (b, i, 0, 0)),
    ]
    outs = pl.pallas_call(
        functools.partial(_feat_kernel, use_rope=rope is not None),
        out_shape=out_shape, grid=(bsz, nt), in_specs=in_specs, out_specs=out_specs,
        compiler_params=_params(("parallel", "parallel")),
        name="feat_rope" if rope is not None else "feat_ctx",
    )(*args)
    return dict(zip(("aq", "ak", "avt", "bq", "bk", "bvt", "c", "g", "bqn", "bkn"), outs))


ACC_ROWS = B_V + 16
NEG_BIG = -1e30
LOG2_E = 1.4426950408889634
ATTN_TQ = 1024
SHIFT_LIMIT = 50.0
NORM_SLACK = 1.02


def _attn_kernel(*refs, n_src, n_heads, shared_kv, fixed_shift):
    if fixed_shift:
        shift_ref = refs[0]
        shift = jnp.concatenate([shift_ref[h:h + 1, :] for h in range(n_heads)], axis=1)
        refs = refs[1:]
    q_ref = refs[0]
    src_refs = refs[1:1 + 2 * n_src]
    o_ref = refs[1 + 2 * n_src]
    if fixed_shift:
        rhs_scr, acc_scr, st_scr = refs[2 + 2 * n_src:]
    else:
        rhs_scr, acc_scr, st_scr, m_scr, mx_scr = refs[2 + 2 * n_src:]
    tq = q_ref.shape[1]
    n_grp, k_dim, grp_w = rhs_scr.shape
    heads_per_grp = n_heads // n_grp

    if shared_kv:
        for h in range(n_heads):
            rhs_scr[0, :, tq * h:tq * (h + 1)] = q_ref[HEAD_DIM * h:HEAD_DIM * (h + 1), :]
    else:
        rhs_scr[...] = jnp.zeros(rhs_scr.shape, BF16)
        for h in range(n_heads):
            g, j = divmod(h, heads_per_grp)
            rhs_scr[g, LANE * j:LANE * (j + 1), tq * j:tq * (j + 1)] = q_ref[LANE * h:LANE * (h + 1), :]
    if not fixed_shift:
        m_scr[...] = jnp.full(m_scr.shape, NEG_BIG, F32)
    acc_scr[...] = jnp.zeros(acc_scr.shape, F32)

    def stage(s, c, slot):
        k_ref, vt_ref = src_refs[2 * s], src_refs[2 * s + 1]
        tk = vt_ref.shape[2]
        rows = pl.ds(pl.multiple_of(c * tk, tk), tk)
        for g in range(n_grp):
            cols = slice(grp_w * g, grp_w * (g + 1))
            kc = k_ref[rows, 0:k_dim] if shared_kv else k_ref[rows, k_dim * g:k_dim * (g + 1)]
            st = _dot(kc, rhs_scr[g])
            if fixed_shift:
                st_scr[slot, 0:tk, cols] = jnp.exp2(st - shift[:, cols]).astype(BF16)
            else:
                st_scr[slot, 0:tk, cols] = st
                mx_scr[slot, :, cols] = jnp.max(st, axis=0, keepdims=True)

    def consume(s, c, slot):
        vt_ref = src_refs[2 * s + 1]
        tk = vt_ref.shape[2]
        ones = jnp.ones((ACC_ROWS - B_V, tk), BF16)
        for g in range(n_grp):
            cols = slice(grp_w * g, grp_w * (g + 1))
            if fixed_shift:
                pt = st_scr[slot, 0:tk, cols]
                alpha = None
            else:
                m_old = m_scr[:, cols]
                m_new = jnp.maximum(m_old, mx_scr[slot, :, cols])
                pt = jnp.exp2(st_scr[slot, 0:tk, cols] - m_new).astype(BF16)
                alpha = jnp.exp2(m_old - m_new)
                m_scr[:, cols] = m_new
            if shared_kv:
                vt = jnp.concatenate([vt_ref[c], ones], axis=0)
                prev = acc_scr[:, cols] if fixed_shift else alpha * acc_scr[:, cols]
                acc_scr[:, cols] = prev + _dot(vt, pt)
            else:
                for j in range(heads_per_grp):
                    h = g * heads_per_grp + j
                    hc = slice(tq * h, tq * (h + 1))
                    lc = slice(tq * j, tq * (j + 1))
                    vt = jnp.concatenate([vt_ref[c, B_V * h:B_V * (h + 1), :], ones], axis=0)
                    prev = acc_scr[:, hc] if fixed_shift else alpha[:, lc] * acc_scr[:, hc]
                    acc_scr[:, hc] = prev + _dot(vt, pt[:, lc])

    n0 = src_refs[1].shape[0]
    n_loop = (n0 - 2) // 2 if n0 >= 4 else 0
    stage(0, 0, 0)
    if n_loop:
        def pair(i, carry):
            stage(0, 2 * i + 1, 1)
            consume(0, 2 * i, 0)
            stage(0, 2 * i + 2, 0)
            consume(0, 2 * i + 1, 1)
            return carry
        lax.fori_loop(0, n_loop, pair, 0)
    tail = [(0, c) for c in range(2 * n_loop, n0)]
    tail += [(s, c) for s in range(1, n_src) for c in range(src_refs[2 * s + 1].shape[0])]
    for i, (s, c) in enumerate(tail):
        if i + 1 < len(tail):
            stage(*tail[i + 1], (i + 1) % 2)
        consume(s, c, i % 2)

    for h in range(n_heads):
        acc = acc_scr[:, tq * h:tq * (h + 1)]
        o_ref[B_V * h:B_V * (h + 1), :] = (acc[:B_V] * (1.0 / acc[B_V:B_V + 1])).astype(BF16)


def _attention(q, srcs, *, n_kv, n_heads, shared_kv, name, shift=None):
    fixed_shift = shift is not None
    bsz, _, t = q.shape
    tq = min(ATTN_TQ * (2 if shared_kv else 1), t)
    q_r = (HEAD_DIM if shared_kv else LANE) * n_heads
    k_w = LANE * (1 if shared_kv else n_heads)
    v_r = B_V * (1 if shared_kv else n_heads)
    rhs_shape = (1, HEAD_DIM, n_heads * tq) if shared_kv else (n_heads // 2, 2 * LANE, 2 * tq)
    in_specs = [pl.BlockSpec((None, q_r, tq), lambda b, g, i: (b, g, i))]
    args = [q]
    for k, vt in srcs:
        length = k.shape[1]
        n_chunk, tk = vt.shape[1], vt.shape[3]
        in_specs.append(pl.BlockSpec((None, length, k_w), lambda b, g, i: (b, 0, g)))
        in_specs.append(pl.BlockSpec((None, n_chunk, v_r, tk), lambda b, g, i: (b, 0, g, 0)))
        args += [k, vt]
    rows = B_V * n_heads
    width = n_heads * tq
    tk_max = max(vt.shape[3] for _, vt in srcs)
    scratch = [pltpu.VMEM(rhs_shape, BF16), pltpu.VMEM((ACC_ROWS, width), F32)]
    if fixed_shift:
        in_specs = [pl.BlockSpec((None, None, n_heads, tq), lambda b, g, i: (b, g, 0, i))] + in_specs
        args = [shift] + args
        scratch += [pltpu.VMEM((2, tk_max, width), BF16)]
    else:
        scratch += [pltpu.VMEM((2, tk_max, width), F32), pltpu.VMEM((1, width), F32),
                    pltpu.VMEM((2, 1, width), F32)]
    return pl.pallas_call(
        functools.partial(_attn_kernel, n_src=len(srcs), n_heads=n_heads, shared_kv=shared_kv,
                          fixed_shift=fixed_shift),
        out_shape=jax.ShapeDtypeStruct((bsz, rows * n_kv, t), BF16),
        grid=(bsz, n_kv, t // tq),
        in_specs=in_specs,
        out_specs=pl.BlockSpec((None, rows, tq), lambda b, g, i: (b, g, i)),
        scratch_shapes=scratch,
        compiler_params=_params(("parallel", "parallel", "parallel")),
        name=name,
    )(*args)


def _scan_constants(reverse):
    c = SCAN_C
    t = np.arange(c)[:, None]
    s = np.arange(c)[None, :]
    cum = (s >= t) if reverse else (s <= t)
    msk = np.zeros((N_LEVELS + 1, c, c), np.float32)
    for lvl in range(N_LEVELS):
        w = (c // 2) >> lvl
        same = (t // (2 * w)) == (s // (2 * w))
        t_hi = (t % (2 * w)) >= w
        s_hi = (s % (2 * w)) >= w
        msk[lvl] = (same & ~t_hi & s_hi) if reverse else (same & t_hi & ~s_hi)
    msk[N_LEVELS] = (t == s)
    return jnp.asarray(cum, BF16), jnp.asarray(np.tile(msk, (1, 1, C_HEADS)), F32)


def _boundary_rows(b, lvl, reverse):
    w = (SCAN_C // 2) >> lvl
    width = b.shape[1]
    off = w if reverse else w - 1
    if w == 1:
        odd = lax.broadcasted_iota(jnp.int32, b.shape, 0) % 2 == 1
        if reverse:
            return jnp.where(odd, b, pltpu.roll(b, SCAN_C - 1, 0))
        return jnp.where(odd, pltpu.roll(b, 1, 0), b)
    if w == 2:
        low = lax.broadcasted_iota(jnp.int32, (8, width), 0) < 4
        pieces = []
        for r0 in range(0, SCAN_C, 8):
            first = jnp.broadcast_to(b[r0 + off:r0 + off + 1, :], (8, width))
            second = jnp.broadcast_to(b[r0 + 4 + off:r0 + 5 + off, :], (8, width))
            pieces.append(jnp.where(low, first, second))
        return jnp.concatenate(pieces, axis=0)
    pieces = [jnp.broadcast_to(b[r0 + off:r0 + off + 1, :], (2 * w, width))
              for r0 in range(0, SCAN_C, 2 * w)]
    return pieces[0] if len(pieces) == 1 else jnp.concatenate(pieces, axis=0)


def _hgrn_kernel(q_ref, k_ref, ghi_ref, glo_ref, v_ref, s0_ref, cum_ref, msk_ref, bm_ref,
                 o_ref, sfin_ref, st_ref, kt_ref, *, reverse, n_chunk):
    i = pl.program_id(1)

    @pl.when(i == 0)
    def _():
        st_ref[...] = s0_ref[...]

    bm = bm_ref[...]
    bm16 = bm.astype(BF16)
    cum = cum_ref[...]

    def stack_heads(a):
        a16 = a.astype(BF16)
        return jnp.concatenate([a16] * C_HEADS, axis=0) * bm16

    grp = min(n_chunk, SCAN_GROUP)
    n_grp = n_chunk // grp

    def group(jg, carry):
        jgg = (n_grp - 1 - jg) if reverse else jg
        base = jgg * (grp * SCAN_C)
        order = list(range(grp - 1, -1, -1)) if reverse else list(range(grp))
        rows = [pl.ds(pl.multiple_of(base + c * SCAN_C, SCAN_C), SCAN_C) for c in range(grp)]
        q = [q_ref[r, :] for r in rows]
        k = [k_ref[r, :] for r in rows]
        v = [v_ref[r, :] for r in rows]
        b = [_dot(cum, ghi_ref[r, :]) + _dot(cum, glo_ref[r, :]) for r in rows]
        b_all = [bc[0:1, :] if reverse else bc[SCAN_C - 1:SCAN_C, :] for bc in b]

        q16 = [qc.astype(BF16) for qc in q]
        kst = [stack_heads(kc) for kc in k]
        sc = [msk_ref[N_LEVELS] * _dot_nt(q16[c], kst[c]) for c in range(grp)]
        for lvl in range(N_LEVELS):
            for c in range(grp):
                d = b[c] - _boundary_rows(b[c], lvl, reverse)
                e16 = jnp.exp2(-jnp.abs(d)).astype(BF16)
                kl = kst[c] * jnp.concatenate([e16] * C_HEADS, axis=0)
                kt_ref[c] = kl.T
                sc[c] = sc[c] + msk_ref[lvl] * _dot(q16[c] * e16, kt_ref[c])

        ds = [_dot_tn(v[c].astype(BF16), (k[c] * jnp.exp2(b_all[c] - b[c])).astype(BF16)) * bm
              for c in range(grp)]
        st = st_ref[...]
        st_in = [None] * grp
        for c in order:
            st_in[c] = st
            st = st * jnp.exp2(b_all[c]) + ds[c]
        st_ref[...] = st

        for c in order:
            qdec = q[c] * jnp.exp2(b[c])
            o_ref[rows[c], :] = (_dot(sc[c].astype(BF16), stack_heads(v[c]))
                                 + _dot_nt(qdec.astype(BF16), st_in[c].astype(BF16)))
        return carry

    if n_grp == 1:
        group(0, 0)
    else:
        lax.fori_loop(0, n_grp, group, 0)

    @pl.when(i == pl.num_programs(1) - 1)
    def _():
        sfin_ref[...] = st_ref[...]


def _hgrn(c_slab, g16, s0, consts, block_mask, reverse):
    bsz, t, _ = c_slab.shape
    tb = min(SCAN_BLOCK, t)
    nblk = t // tb
    cum, msk = consts

    def blk(i):
        return (nblk - 1 - i) if reverse else i

    def slab_spec(j):
        return pl.BlockSpec((None, tb, C_W), lambda b, i: (b, blk(i), j))

    g0 = 2 if reverse else 0

    return pl.pallas_call(
        functools.partial(_hgrn_kernel, reverse=reverse, n_chunk=tb // SCAN_C),
        out_shape=[jax.ShapeDtypeStruct((bsz, t, C_W), F32),
                   jax.ShapeDtypeStruct((bsz, C_W, C_W), F32)],
        grid=(bsz, nblk),
        in_specs=[
            slab_spec(C_Q), slab_spec(C_KB if reverse else C_KF), slab_spec(g0), slab_spec(g0 + 1),
            slab_spec(C_V),
            pl.BlockSpec((None, C_W, C_W), lambda b, i: (b, 0, 0)),
            _const_spec(cum.shape, (0, 0)),
            _const_spec(msk.shape, (0, 0, 0)),
            _const_spec(block_mask.shape, (0, 0)),
        ],
        out_specs=[pl.BlockSpec((None, tb, C_W), lambda b, i: (b, blk(i), 0)),
                   pl.BlockSpec((None, C_W, C_W), lambda b, i: (b, 0, 0))],
        scratch_shapes=[pltpu.VMEM((C_W, C_W), F32),
                        pltpu.VMEM((min(tb // SCAN_C, SCAN_GROUP), C_W, C_HEADS * SCAN_C), BF16)],
        compiler_params=_params(("parallel", "arbitrary")),
        name="hgrn_bwd" if reverse else "hgrn_fwd",
    )(c_slab, c_slab, g16, g16, c_slab, s0, cum, msk, block_mask)


POST_TM = 1024
POST_SUB = 256


def _rms(v, gain):
    ms = jnp.mean(v * v, axis=-1, keepdims=True)
    return v * lax.rsqrt(ms + EPS) * gain


def _post_kernel(x_ref, ya_ref, yb_ref, of_ref, ob_ref, gate_ref,
                 gtm_ref, shf_ref, scf_ref, gtf_ref,
                 gpm_ref, gpf_ref, gqf_ref, con_ref, bo_ref,
                 woa_ref, wob_ref, woc_ref, w1_ref, w2_ref, o_ref):
    tm = x_ref.shape[0]
    sub = min(tm, POST_SUB)
    for r0 in range(0, tm, sub):
        rs = slice(r0, r0 + sub)
        o = of_ref[rs, :] + ob_ref[rs, :]
        yc = (_head_rms(o, bo_ref[...], con_ref[...]) * _silu(gate_ref[rs, :])).astype(BF16)
        mix = (_dot_tn(ya_ref[:, rs], woa_ref[...]) + _dot_tn(yb_ref[:, rs], wob_ref[...])
               + _dot(yc, woc_ref[...]))
        x1 = x_ref[rs, :] + gtm_ref[...] * _rms(mix, gpm_ref[...])
        h = _rms(x1, gpf_ref[...]) * (1.0 + scf_ref[...]) + shf_ref[...]
        u = jnp.maximum(_dot(h.astype(BF16), w1_ref[...]), 0.0)
        ff = _dot((u * u).astype(BF16), w2_ref[...])
        o_ref[rs, :] = x1 + gtf_ref[...] * _rms(ff, gqf_ref[...])


def _post(xs, yta, ytb, o_f, o_b, c_slab, mod, w, layer, mod_row):
    bsz, t, _ = xs.shape
    tm = min(POST_TM, t)
    nb_rows = w["n_mod_rows"]

    def mod_spec(j):
        return pl.BlockSpec((None, 1, D_MODEL),
                            lambda b, i: ((layer * nb_rows + mod_row(b)) * 6 + j, 0, 0))

    def vec_spec():
        return _const_spec((None, 1, D_MODEL), (layer, 0, 0))

    in_specs = [
        pl.BlockSpec((None, tm, D_MODEL), lambda b, i: (b, i, 0)),
        pl.BlockSpec((None, A_OUT, tm), lambda b, i: (b, 0, i)),
        pl.BlockSpec((None, B_OUT, tm), lambda b, i: (b, 0, i)),
        pl.BlockSpec((None, tm, C_W), lambda b, i: (b, i, 0)),
        pl.BlockSpec((None, tm, C_W), lambda b, i: (b, i, 0)),
        pl.BlockSpec((None, tm, C_W), lambda b, i: (b, i, C_GATE)),
        mod_spec(2), mod_spec(3), mod_spec(4), mod_spec(5),
        vec_spec(), vec_spec(), vec_spec(),
        _const_spec((None, 1, C_W), (layer, 0, 0)),
        _const_spec((256, 256), (0, 0)),
        _const_spec((None, A_OUT, D_MODEL), (layer, 0, 0)),
        _const_spec((None, B_OUT, D_MODEL), (layer, 0, 0)),
        _const_spec((None, C_OUT, D_MODEL), (layer, 0, 0)),
        _const_spec((None, D_MODEL, D_FF), (layer, 0, 0)),
        _const_spec((None, D_FF, D_MODEL), (layer, 0, 0)),
    ]
    return pl.pallas_call(
        _post_kernel,
        out_shape=jax.ShapeDtypeStruct((bsz, t, D_MODEL), F32),
        grid=(bsz, t // tm),
        in_specs=in_specs,
        out_specs=pl.BlockSpec((None, tm, D_MODEL), lambda b, i: (b, i, 0)),
        compiler_params=_params(("parallel", "parallel")),
        name="post",
    )(xs, yta, ytb, o_f, o_b, c_slab, mod, mod, mod, mod,
      w["g_post_mix"], w["g_pre_ffn"], w["g_post_ffn"], w["c_out_norm"], w["block_ones"],
      w["w_out_a"], w["w_out_b"], w["w_out_c"], w["w_ff1"], w["w_ff2"])


def _w_in_columns():
    src = np.full((N_COL,), -1, np.int64)
    a_k0 = A_HEADS * HEAD_DIM
    a_v0 = a_k0 + A_KV_HEADS * HEAD_DIM
    b_qd0 = a_v0 + A_KV_HEADS * HEAD_DIM
    b_kv0 = b_qd0 + B_Q_RANK
    b_kr0 = b_kv0 + B_KV_RANK
    c0 = b_kr0 + B_ROPE
    src[OFF_AQ:OFF_AQ + 512] = np.arange(512)
    for g in range(A_KV_HEADS):
        for rep in range(2):
            lo = OFF_AK + (2 * g + rep) * HEAD_DIM
            src[lo:lo + HEAD_DIM] = a_k0 + g * HEAD_DIM + np.arange(HEAD_DIM)
    src[OFF_AV:OFF_AV + 128] = a_v0 + np.arange(128)
    src[OFF_BQD:OFF_BQD + B_Q_RANK] = b_qd0 + np.arange(B_Q_RANK)
    src[OFF_BKVD:OFF_BKVD + B_KV_RANK] = b_kv0 + np.arange(B_KV_RANK)
    src[OFF_KPE + B_NOPE:OFF_KPE + B_NOPE + B_ROPE] = b_kr0 + np.arange(B_ROPE)
    src[OFF_CQ:OFF_CQ + 5 * C_W] = c0 + np.arange(5 * C_W)
    return src


def _gather_cols(w, src):
    pieces, i, n = [], 0, len(src)
    while i < n:
        j = i + 1
        if src[i] < 0:
            while j < n and src[j] < 0:
                j += 1
            pieces.append(jnp.zeros(w.shape[:-1] + (j - i,), w.dtype))
        else:
            while j < n and src[j] == src[j - 1] + 1:
                j += 1
            pieces.append(w[..., int(src[i]):int(src[i]) + (j - i)])
        i = j
    return jnp.concatenate(pieces, axis=-1)


def _prepare_weights(p, n_mod_rows):
    depth = p["w_in"].shape[0]
    w = {"n_mod_rows": n_mod_rows}
    w["w_in"] = _gather_cols(p["w_in"], _w_in_columns()).astype(BF16)
    src = np.full((B_HEADS * LANE,), -1, np.int64)
    for hh in range(B_HEADS):
        src[hh * LANE:hh * LANE + B_NOPE + B_ROPE] = hh * (B_NOPE + B_ROPE) + np.arange(B_NOPE + B_ROPE)
    wq = _gather_cols(p["w_q_up"], src)
    w["w_q_up"] = jnp.pad(wq, ((0, 0), (0, 256 - B_Q_RANK), (0, 0))).astype(BF16)
    src = np.full((B_HEADS * LANE,), -1, np.int64)
    srcv = np.zeros((B_HEADS * B_V,), np.int64)
    for hh in range(B_HEADS):
        src[hh * LANE:hh * LANE + B_NOPE] = hh * (B_NOPE + B_V) + np.arange(B_NOPE)
        srcv[hh * B_V:(hh + 1) * B_V] = hh * (B_NOPE + B_V) + B_NOPE + np.arange(B_V)
    w["w_kv_k"] = _gather_cols(p["w_kv_up"], src).astype(BF16)
    w["w_kv_v"] = _gather_cols(p["w_kv_up"], srcv).astype(BF16)
    w["a_q_norm"] = jnp.tile(p["a_q_norm"], (1, 4))[:, None, :]
    w["a_k_norm"] = jnp.tile(p["a_k_norm"], (1, 4))[:, None, :]
    w["b_q_norm"] = jnp.pad(p["b_q_norm"], ((0, 0), (0, 256 - B_Q_RANK)))[:, None, :]
    w["b_kv_norm"] = p["b_kv_norm"][:, None, :]
    w["c_out_norm"] = jnp.tile(p["c_out_norm"], (1, C_HEADS))[:, None, :]
    for name in ("g_pre_mix", "g_post_mix", "g_pre_ffn", "g_post_ffn"):
        w[name] = p[name][:, None, :]
    head = np.arange(256) // HEAD_DIM
    same_head = head[:, None] == head[None, :]
    w["block_ones"] = jnp.asarray(same_head / float(HEAD_DIM), BF16)
    w["block_mask"] = jnp.asarray(same_head, F32)
    p_lb = jax.nn.softmax(p["c_lower_bounds"].astype(F32), axis=0)
    w["lower"] = jnp.cumsum(p_lb, axis=0) - p_lb[:1]
    w["w_out_a"] = p["w_out"][:, :A_OUT].astype(BF16)
    w["w_out_b"] = p["w_out"][:, A_OUT:A_OUT + B_OUT].astype(BF16)
    w["w_out_c"] = p["w_out"][:, A_OUT + B_OUT:].astype(BF16)
    w["w_ff1"] = p["w_ff1"].astype(BF16)
    w["w_ff2"] = p["w_ff2"].astype(BF16)
    del depth
    return w


def _rope_tables(n_tok):
    tok = np.arange(n_tok)
    row = (tok // GRID_W).astype(np.float32)[:, None]
    col = (tok % GRID_W).astype(np.float32)[:, None]

    def angles(rot_dim):
        n_freq = rot_dim // 4
        inv = jnp.asarray(ROPE_THETA, F32) ** (-jnp.arange(n_freq, dtype=F32) / n_freq)
        ang = jnp.concatenate([jnp.asarray(row) * inv, jnp.asarray(col) * inv], axis=-1)
        return jnp.cos(ang), jnp.sin(ang)

    zeros = lambda n: jnp.zeros((n_tok, n), F32)
    ones = lambda n: jnp.ones((n_tok, n), F32)
    cos, sin = angles(HEAD_DIM)
    ca = jnp.concatenate([cos, cos, cos, cos], axis=-1)
    s1a = jnp.concatenate([-sin, zeros(32), -sin, zeros(32)], axis=-1)
    s2a = jnp.concatenate([zeros(32), sin, zeros(32), sin], axis=-1)
    cos, sin = angles(B_ROPE)
    cb = jnp.concatenate([ones(64), cos, cos, ones(32)], axis=-1)
    s1b = jnp.concatenate([zeros(64), -sin, zeros(16), zeros(32)], axis=-1)
    s2b = jnp.concatenate([zeros(64), zeros(16), sin, zeros(32)], axis=-1)
    return ca, s1a, s2a, cb, s1b, s2b


def kernel(x, c, ctx, c_ctx, w_ada, b_ada, g_pre_mix, g_post_mix, g_pre_ffn, g_post_ffn, w_in, a_q_norm, a_k_norm, b_q_norm, w_q_up, b_kv_norm, w_kv_up, c_lower_bounds, c_out_norm, w_out, w_ff1, w_ff2):
    bsz, n_lat, _ = x.shape
    depth = w_in.shape[0]
    n_mod_rows = -(-(bsz + 1) // 8) * 8
    params = dict(w_in=w_in, a_q_norm=a_q_norm, a_k_norm=a_k_norm, b_q_norm=b_q_norm, w_q_up=w_q_up,
                  b_kv_norm=b_kv_norm, w_kv_up=w_kv_up, c_lower_bounds=c_lower_bounds,
                  c_out_norm=c_out_norm, w_out=w_out, w_ff1=w_ff1, w_ff2=w_ff2,
                  g_pre_mix=g_pre_mix, g_post_mix=g_post_mix, g_pre_ffn=g_pre_ffn, g_post_ffn=g_post_ffn)
    w = _prepare_weights(params, n_mod_rows)
    rope = _rope_tables(n_lat)
    scan_f = _scan_constants(False)
    scan_b = _scan_constants(True)

    cvec = jnp.concatenate([c, c_ctx[None, :], jnp.zeros((n_mod_rows - bsz - 1, D_MODEL), F32)], axis=0)
    mod = _ada(cvec, w_ada, b_ada).reshape(depth * n_mod_rows * 6, 1, D_MODEL)

    lat_row = lambda b: b
    ctx_row = lambda b: bsz
    zero_state = jnp.zeros((bsz, C_W, C_W), F32)
    attn_a = functools.partial(_attention, n_kv=A_KV_HEADS, n_heads=A_HEADS // A_KV_HEADS, shared_kv=True)
    attn_b = functools.partial(_attention, n_kv=1, n_heads=B_HEADS, shared_kv=False)

    xc = ctx
    for layer in range(depth):
        need_ctx = layer < depth - 1
        fl = _features(x, mod, w, layer, lat_row, rope)
        fc = _features(xc, mod, w, layer, ctx_row, None)
        bound_a = ((HEAD_DIM ** 0.5 * LOG2_E) * jnp.max(jnp.abs(a_q_norm[layer]))
                   * jnp.max(jnp.abs(a_k_norm[layer])))
        a_srcs = [(fl["ak"], fl["avt"]), (fc["ak"], fc["avt"])]
        yta = lax.cond(
            bound_a <= SHIFT_LIMIT,
            lambda q, s, bnd: attn_a(q, s, name="attn_a_fixed", shift=jnp.broadcast_to(
                bnd, (bsz, A_KV_HEADS, A_HEADS // A_KV_HEADS, n_lat))),
            lambda q, s, bnd: attn_a(q, s, name="attn_a"),
            fl["aq"], a_srcs, bound_a)
        k2_max = jnp.maximum(jnp.max(fl["bkn"][:, :, 0, :B_HEADS], axis=1),
                             jnp.max(fc["bkn"][:, :, 0, :B_HEADS], axis=1))
        bound_b = jnp.sqrt(fl["bqn"][:, :B_HEADS, :] * k2_max[:, :, None]) * NORM_SLACK
        b_srcs = [(fl["bk"], fl["bvt"]), (fc["bk"], fc["bvt"])]
        ytb = lax.cond(
            jnp.max(bound_b) <= SHIFT_LIMIT,
            lambda q, s, bnd: attn_b(q, s, name="attn_b_fixed", shift=bnd[:, None]),
            lambda q, s, bnd: attn_b(q, s, name="attn_b"),
            fl["bq"], b_srcs, bound_b)
        ocf, s_f = _hgrn(fc["c"], fc["g"], zero_state, scan_f, w["block_mask"], False)
        ocb, s_b = _hgrn(fc["c"], fc["g"], zero_state, scan_b, w["block_mask"], True)
        olf, _ = _hgrn(fl["c"], fl["g"], s_f, scan_f, w["block_mask"], False)
        olb, _ = _hgrn(fl["c"], fl["g"], s_b, scan_b, w["block_mask"], True)
        x_new = _post(x, yta, ytb, olf, olb, fl["c"], mod, w, layer, lat_row)
        if need_ctx:
            yta_c = attn_a(fc["aq"], [(fc["ak"], fc["avt"])], name="attn_a_ctx")
            ytb_c = attn_b(fc["bq"], [(fc["bk"], fc["bvt"])], name="attn_b_ctx")
            xc = _post(xc, yta_c, ytb_c, ocf, ocb, fc["c"], mod, w, layer, ctx_row)
        x = x_new
    return x
```

```python
import functools

import numpy as np
import jax
import jax.numpy as jnp
from jax import lax
from jax.experimental import pallas as pl
from jax.experimental.pallas import tpu as pltpu

F32 = jnp.float32
BF16 = jnp.bfloat16

D_MODEL = 1024
GRID_W = 64
HEAD_DIM = 64
A_HEADS = 8
A_KV_HEADS = 2
B_HEADS = 4
B_Q_RANK = 192
B_KV_RANK = 128
B_NOPE = 64
B_ROPE = 32
B_V = 64
C_HEADS = 4
C_DK = 64
C_DV = 64
D_FF = 4 * D_MODEL
A_OUT = A_HEADS * HEAD_DIM
B_OUT = B_HEADS * B_V
C_OUT = C_HEADS * C_DV
C_W = C_HEADS * C_DK
ROPE_THETA = 10000.0
EPS = 1e-6
F_TINY = 1e-30

LANE = 128
VMEM_LIMIT = 56 * 1024 * 1024

OFF_AQ = 0
OFF_AK = 512
OFF_AV = 768
OFF_BQD = 896
OFF_BKVD = 1152
OFF_KPE = 1280
OFF_CQ = 1408
OFF_CFF = 1664
OFF_CFB = 1920
OFF_CI = 2176
OFF_CG = 2432
N_COL = 2688

C_Q, C_KF, C_KB, C_V, C_GATE = range(5)
N_SLAB = 5

SCAN_C = 64
N_LEVELS = 6
SCAN_BLOCK = 2048
SCAN_GROUP = 8


def _dot(a, b):
    return jnp.dot(a, b, preferred_element_type=F32)


def _dot_nt(a, b):
    return lax.dot_general(a, b, (((1,), (1,)), ((), ())), preferred_element_type=F32)


def _dot_tn(a, b):
    return lax.dot_general(a, b, (((0,), (0,)), ((), ())), preferred_element_type=F32)


def _sigmoid_pair(z):
    e = jnp.exp(-jnp.abs(z))
    inv = 1.0 / (1.0 + e)
    small = e * inv
    pos = z >= 0
    return jnp.where(pos, inv, small), jnp.where(pos, small, inv)


def _silu(z):
    s, _ = _sigmoid_pair(z)
    return z * s


def _const_spec(shape, index):
    return pl.BlockSpec(shape, lambda *_: index, pipeline_mode=pl.Buffered(1))


def _params(sem):
    return pltpu.CompilerParams(dimension_semantics=sem, vmem_limit_bytes=VMEM_LIMIT)


def _ada_kernel(c_ref, w_ref, b_ref, o_ref):
    a = _silu(c_ref[...])
    w = w_ref[...]
    a_hi = a.astype(BF16)
    a_lo = (a - a_hi.astype(F32)).astype(BF16)
    w_hi = w.astype(BF16)
    w_lo = (w - w_hi.astype(F32)).astype(BF16)
    acc = _dot(a_hi, w_hi) + (_dot(a_hi, w_lo) + _dot(a_lo, w_hi))
    o_ref[...] = acc + b_ref[...]


def _ada(cvec, w_ada, b_ada):
    depth = w_ada.shape[0]
    rows = cvec.shape[0]
    n_blk = w_ada.shape[2] // D_MODEL
    return pl.pallas_call(
        _ada_kernel,
        out_shape=jax.ShapeDtypeStruct((depth, rows, n_blk * D_MODEL), F32),
        grid=(depth, n_blk),
        in_specs=[
            pl.BlockSpec((rows, D_MODEL), lambda l, j: (0, 0)),
            pl.BlockSpec((None, D_MODEL, D_MODEL), lambda l, j: (l, 0, j)),
            pl.BlockSpec((None, None, 1, D_MODEL), lambda l, j: (l, j, 0, 0)),
        ],
        out_specs=pl.BlockSpec((None, rows, D_MODEL), lambda l, j: (l, 0, j)),
        compiler_params=_params(("arbitrary", "arbitrary")),
        name="ada",
    )(cvec, w_ada, b_ada.reshape(depth, n_blk, 1, D_MODEL))


def _rope128(v, cos, s1, s2, half):
    up = pltpu.roll(v, LANE - half, 1)
    dn = pltpu.roll(v, half, 1)
    return v * cos + up * s1 + dn * s2


def _head_rms(v, bo, gain):
    sq = v * v
    hi = sq.astype(BF16)
    lo = (sq - hi.astype(F32)).astype(BF16)
    ms = _dot(hi, bo) + _dot(lo, bo)
    return v * lax.rsqrt(ms + EPS) * gain


FEAT_TM = 1024
FEAT_SUB = 256
KV_CHUNK = 512


def _feat_kernel(*refs, use_rope):
    (x_ref, sh_ref, sc_ref, gpre_ref, win_ref, wq_ref, wkk_ref, wkv_ref,
     aqn_ref, akn_ref, bqn_ref, bkvn_ref, bo_ref, lb_ref) = refs[:14]
    rest = refs[14:]
    if use_rope:
        ca_ref, s1a_ref, s2a_ref, cb_ref, s1b_ref, s2b_ref = rest[:6]
        rest = rest[6:]
    aq_o, ak_o, avt_o, bq_o, bk_o, bvt_o, c_o, g_o, bqn_o, bkn_o = rest
    bo = bo_ref[...]
    tm = x_ref.shape[0]
    sub = min(tm, FEAT_SUB)
    b_k2 = [None] * B_HEADS
    bqn_o[B_HEADS:, :] = jnp.zeros((bqn_o.shape[0] - B_HEADS, tm), F32)

    for r0 in range(0, tm, sub):
        rs = slice(r0, r0 + sub)

        def rope_a(v, rs=rs):
            if not use_rope:
                return v
            return _rope128(v, ca_ref[rs, :], s1a_ref[rs, :], s2a_ref[rs, :], HEAD_DIM // 2)

        def rope_b(v, rs=rs):
            if not use_rope:
                return v
            return _rope128(v, cb_ref[rs, :], s1b_ref[rs, :], s2b_ref[rs, :], B_ROPE // 2)

        x = x_ref[rs, :]
        ms = jnp.mean(x * x, axis=-1, keepdims=True)
        h = x * lax.rsqrt(ms + EPS) * gpre_ref[...]
        h = h * (1.0 + sc_ref[...]) + sh_ref[...]
        p = _dot(h.astype(BF16), win_ref[...])

        aqn = aqn_ref[...]
        for half in range(2):
            v = _head_rms(p[:, OFF_AQ + 256 * half:OFF_AQ + 256 * (half + 1)], bo, aqn)
            for s in range(2):
                blk = rope_a(v[:, LANE * s:LANE * (s + 1)]) * (HEAD_DIM ** -0.5 * LOG2_E)
                lo = 256 * half + LANE * s
                aq_o[lo:lo + LANE, rs] = blk.T.astype(BF16)
        v = _head_rms(p[:, OFF_AK:OFF_AK + 256], bo, akn_ref[...])
        for s in range(2):
            ak_o[rs, LANE * s:LANE * (s + 1)] = rope_a(v[:, LANE * s:LANE * (s + 1)]).astype(BF16)
        kv_w = avt_o.shape[2]
        cs = slice(r0 % kv_w, r0 % kv_w + sub)
        avt_o[r0 // kv_w, :, cs] = p[:, OFF_AV:OFF_AV + LANE].T.astype(BF16)

        bqd = p[:, OFF_BQD:OFF_BQD + 256]
        ms = jnp.sum(bqd * bqd, axis=-1, keepdims=True) * (1.0 / B_Q_RANK)
        qn = (bqd * lax.rsqrt(ms + EPS) * bqn_ref[...]).astype(BF16)
        bq = _dot(qn, wq_ref[...])
        bkvd = p[:, OFF_BKVD:OFF_BKVD + LANE]
        ms = jnp.mean(bkvd * bkvd, axis=-1, keepdims=True)
        kvn = (bkvd * lax.rsqrt(ms + EPS) * bkvn_ref[...]).astype(BF16)
        bkn = _dot(kvn, wkk_ref[...])
        bv = _dot(kvn, wkv_ref[...])
        kpe = rope_b(p[:, OFF_KPE:OFF_KPE + LANE])
        b_scale = (B_NOPE + B_ROPE) ** -0.5 * LOG2_E
        for hh in range(B_HEADS):
            sl = slice(LANE * hh, LANE * (hh + 1))
            qt = (rope_b(bq[:, sl]) * b_scale).T
            kh = bkn[:, sl] + kpe
            bq_o[sl, rs] = qt.astype(BF16)
            bk_o[rs, sl] = kh.astype(BF16)
            bqn_o[hh:hh + 1, rs] = jnp.sum(qt * qt, axis=0, keepdims=True)
            k2 = jnp.max(jnp.sum(kh * kh, axis=-1, keepdims=True), axis=0, keepdims=True)
            b_k2[hh] = k2 if b_k2[hh] is None else jnp.maximum(b_k2[hh], k2)
        bvt_o[r0 // kv_w, :, cs] = bv.T.astype(BF16)

        c_o[rs, C_W * C_Q:C_W * (C_Q + 1)] = _silu(p[:, OFF_CQ:OFF_CQ + C_W])
        for d, (off, ck) in enumerate(((OFF_CFF, C_KF), (OFF_CFB, C_KB))):
            lb = lb_ref[d:d + 1, :]
            sp, sn = _sigmoid_pair(p[:, off:off + C_W])
            f = lb + (1.0 - lb) * sp
            g2 = jnp.log2(jnp.maximum(f, F_TINY))
            hi = g2.astype(BF16)
            g_o[rs, C_W * 2 * d:C_W * (2 * d + 1)] = hi
            g_o[rs, C_W * (2 * d + 1):C_W * (2 * d + 2)] = (g2 - hi.astype(F32)).astype(BF16)
            c_o[rs, C_W * ck:C_W * (ck + 1)] = (1.0 - lb) * sn
        c_o[rs, C_W * C_V:C_W * (C_V + 1)] = p[:, OFF_CI:OFF_CI + C_W]
        c_o[rs, C_W * C_GATE:C_W * (C_GATE + 1)] = p[:, OFF_CG:OFF_CG + C_W]

    lane = lax.broadcasted_iota(jnp.int32, bkn_o.shape, 1)
    row = jnp.zeros(bkn_o.shape, F32)
    for hh in range(B_HEADS):
        row = jnp.where(lane == hh, jnp.broadcast_to(b_k2[hh], bkn_o.shape), row)
    bkn_o[...] = row


def _features(xs, mod, w, layer, mod_row, rope):
    bsz, t, _ = xs.shape
    tm = min(FEAT_TM, t)
    nt = t // tm
    kv_w = min(KV_CHUNK, t)
    n_kv = tm // kv_w
    nb_rows = w["n_mod_rows"]

    def mod_spec(j):
        return pl.BlockSpec((None, 1, D_MODEL),
                            lambda b, i: ((layer * nb_rows + mod_row(b)) * 6 + j, 0, 0))

    in_specs = [
        pl.BlockSpec((None, tm, D_MODEL), lambda b, i: (b, i, 0)),
        mod_spec(0), mod_spec(1),
        _const_spec((None, 1, D_MODEL), (layer, 0, 0)),
        _const_spec((None, D_MODEL, N_COL), (layer, 0, 0)),
        _const_spec((None, 256, 512), (layer, 0, 0)),
        _const_spec((None, LANE, 512), (layer, 0, 0)),
        _const_spec((None, LANE, 256), (layer, 0, 0)),
        _const_spec((None, 1, 256), (layer, 0, 0)),
        _const_spec((None, 1, 256), (layer, 0, 0)),
        _const_spec((None, 1, 256), (layer, 0, 0)),
        _const_spec((None, 1, LANE), (layer, 0, 0)),
        _const_spec((256, 256), (0, 0)),
        _const_spec((None, 2, C_W), (layer, 0, 0)),
    ]
    args = [xs, mod, mod, w["g_pre_mix"], w["w_in"], w["w_q_up"], w["w_kv_k"], w["w_kv_v"],
            w["a_q_norm"], w["a_k_norm"], w["b_q_norm"], w["b_kv_norm"], w["block_ones"], w["lower"]]
    if rope is not None:
        in_specs += [pl.BlockSpec((tm, LANE), lambda b, i: (i, 0))] * 6
        args += list(rope)
    out_shape = [
        jax.ShapeDtypeStruct((bsz, 512, t), BF16),
        jax.ShapeDtypeStruct((bsz, t, 256), BF16),
        jax.ShapeDtypeStruct((bsz, t // kv_w, LANE, kv_w), BF16),
        jax.ShapeDtypeStruct((bsz, 512, t), BF16),
        jax.ShapeDtypeStruct((bsz, t, 512), BF16),
        jax.ShapeDtypeStruct((bsz, t // kv_w, 256, kv_w), BF16),
        jax.ShapeDtypeStruct((bsz, t, N_SLAB * C_W), F32),
        jax.ShapeDtypeStruct((bsz, t, 4 * C_W), BF16),
        jax.ShapeDtypeStruct((bsz, 8, t), F32),
        jax.ShapeDtypeStruct((bsz, nt, 8, LANE), F32),
    ]
    out_specs = [
        pl.BlockSpec((None, 512, tm), lambda b, i: (b, 0, i)),
        pl.BlockSpec((None, tm, 256), lambda b, i: (b, i, 0)),
        pl.BlockSpec((None, n_kv, LANE, kv_w), lambda b, i: (b, i, 0, 0)),
        pl.BlockSpec((None, 512, tm), lambda b, i: (b, 0, i)),
        pl.BlockSpec((None, tm, 512), lambda b, i: (b, i, 0)),
        pl.BlockSpec((None, n_kv, 256, kv_w), lambda b, i: (b, i, 0, 0)),
        pl.BlockSpec((None, tm, N_SLAB * C_W), lambda b, i: (b, i, 0)),
        pl.BlockSpec((None, tm, 4 * C_W), lambda b, i: (b, i, 0)),
        pl.BlockSpec((None, 8, tm), lambda b, i: (b, 0, i)),
        pl.BlockSpec((None, None, 8, LANE), lambda b, i: (b, i, 0, 0)),
    ]
    outs = pl.pallas_call(
        functools.partial(_feat_kernel, use_rope=rope is not None),
        out_shape=out_shape, grid=(bsz, nt), in_specs=in_specs, out_specs=out_specs,
        compiler_params=_params(("parallel", "parallel")),
        name="feat_rope" if rope is not None else "feat_ctx",
    )(*args)
    return dict(zip(("aq", "ak", "avt", "bq", "bk", "bvt", "c", "g", "bqn", "bkn"), outs))


ACC_ROWS = B_V + 16
NEG_BIG = -1e30
LOG2_E = 1.4426950408889634
ATTN_TQ = 1024
SHIFT_LIMIT = 50.0
NORM_SLACK = 1.02


def _attn_kernel(*refs, n_src, n_heads, shared_kv, fixed_shift):
    if fixed_shift:
        shift_ref = refs[0]
        shift = jnp.concatenate([shift_ref[h:h + 1, :] for h in range(n_heads)], axis=1)
        refs = refs[1:]
    q_ref = refs[0]
    src_refs = refs[1:1 + 2 * n_src]
    o_ref = refs[1 + 2 * n_src]
    if fixed_shift:
        rhs_scr, acc_scr, st_scr = refs[2 + 2 * n_src:]
    else:
        rhs_scr, acc_scr, st_scr, m_scr, mx_scr = refs[2 + 2 * n_src:]
    tq = q_ref.shape[1]
    n_grp, k_dim, grp_w = rhs_scr.shape
    heads_per_grp = n_heads // n_grp

    if shared_kv:
        for h in range(n_heads):
            rhs_scr[0, :, tq * h:tq * (h + 1)] = q_ref[HEAD_DIM * h:HEAD_DIM * (h + 1), :]
    else:
        rhs_scr[...] = jnp.zeros(rhs_scr.shape, BF16)
        for h in range(n_heads):
            g, j = divmod(h, heads_per_grp)
            rhs_scr[g, LANE * j:LANE * (j + 1), tq * j:tq * (j + 1)] = q_ref[LANE * h:LANE * (h + 1), :]
    if not fixed_shift:
        m_scr[...] = jnp.full(m_scr.shape, NEG_BIG, F32)
    acc_scr[...] = jnp.zeros(acc_scr.shape, F32)

    def stage(s, c, slot):
        k_ref, vt_ref = src_refs[2 * s], src_refs[2 * s + 1]
        tk = vt_ref.shape[2]
        rows = pl.ds(pl.multiple_of(c * tk, tk), tk)
        for g in range(n_grp):
            cols = slice(grp_w * g, grp_w * (g + 1))
            kc = k_ref[rows, 0:k_dim] if shared_kv else k_ref[rows, k_dim * g:k_dim * (g + 1)]
            st = _dot(kc, rhs_scr[g])
            if fixed_shift:
                st_scr[slot, 0:tk, cols] = jnp.exp2(st - shift[:, cols]).astype(BF16)
            else:
                st_scr[slot, 0:tk, cols] = st
                mx_scr[slot, :, cols] = jnp.max(st, axis=0, keepdims=True)

    def consume(s, c, slot):
        vt_ref = src_refs[2 * s + 1]
        tk = vt_ref.shape[2]
        ones = jnp.ones((ACC_ROWS - B_V, tk), BF16)
        for g in range(n_grp):
            cols = slice(grp_w * g, grp_w * (g + 1))
            if fixed_shift:
                pt = st_scr[slot, 0:tk, cols]
                alpha = None
            else:
                m_old = m_scr[:, cols]
                m_new = jnp.maximum(m_old, mx_scr[slot, :, cols])
                pt = jnp.exp2(st_scr[slot, 0:tk, cols] - m_new).astype(BF16)
                alpha = jnp.exp2(m_old - m_new)
                m_scr[:, cols] = m_new
            if shared_kv:
                vt = jnp.concatenate([vt_ref[c], ones], axis=0)
                prev = acc_scr[:, cols] if fixed_shift else alpha * acc_scr[:, cols]
                acc_scr[:, cols] = prev + _dot(vt, pt)
            else:
                for j in range(heads_per_grp):
                    h = g * heads_per_grp + j
                    hc = slice(tq * h, tq * (h + 1))
                    lc = slice(tq * j, tq * (j + 1))
                    vt = jnp.concatenate([vt_ref[c, B_V * h:B_V * (h + 1), :], ones], axis=0)
                    prev = acc_scr[:, hc] if fixed_shift else alpha[:, lc] * acc_scr[:, hc]
                    acc_scr[:, hc] = prev + _dot(vt, pt[:, lc])

    n0 = src_refs[1].shape[0]
    n_loop = (n0 - 2) // 2 if n0 >= 4 else 0
    stage(0, 0, 0)
    if n_loop:
        def pair(i, carry):
            stage(0, 2 * i + 1, 1)
            consume(0, 2 * i, 0)
            stage(0, 2 * i + 2, 0)
            consume(0, 2 * i + 1, 1)
            return carry
        lax.fori_loop(0, n_loop, pair, 0)
    tail = [(0, c) for c in range(2 * n_loop, n0)]
    tail += [(s, c) for s in range(1, n_src) for c in range(src_refs[2 * s + 1].shape[0])]
    for i, (s, c) in enumerate(tail):
        if i + 1 < len(tail):
            stage(*tail[i + 1], (i + 1) % 2)
        consume(s, c, i % 2)

    for h in range(n_heads):
        acc = acc_scr[:, tq * h:tq * (h + 1)]
        o_ref[B_V * h:B_V * (h + 1), :] = (acc[:B_V] * (1.0 / acc[B_V:B_V + 1])).astype(BF16)


def _attention(q, srcs, *, n_kv, n_heads, shared_kv, name, shift=None):
    fixed_shift = shift is not None
    bsz, _, t = q.shape
    tq = min(ATTN_TQ * (2 if shared_kv else 1) * (2 if fixed_shift else 1), t)
    q_r = (HEAD_DIM if shared_kv else LANE) * n_heads
    k_w = LANE * (1 if shared_kv else n_heads)
    v_r = B_V * (1 if shared_kv else n_heads)
    rhs_shape = (1, HEAD_DIM, n_heads * tq) if shared_kv else (n_heads // 2, 2 * LANE, 2 * tq)
    in_specs = [pl.BlockSpec((None, q_r, tq), lambda b, g, i: (b, g, i))]
    args = [q]
    for k, vt in srcs:
        length = k.shape[1]
        n_chunk, tk = vt.shape[1], vt.shape[3]
        in_specs.append(pl.BlockSpec((None, length, k_w), lambda b, g, i: (b, 0, g)))
        in_specs.append(pl.BlockSpec((None, n_chunk, v_r, tk), lambda b, g, i: (b, 0, g, 0)))
        args += [k, vt]
    rows = B_V * n_heads
    width = n_heads * tq
    tk_max = max(vt.shape[3] for _, vt in srcs)
    scratch = [pltpu.VMEM(rhs_shape, BF16), pltpu.VMEM((ACC_ROWS, width), F32)]
    if fixed_shift:
        in_specs = [pl.BlockSpec((None, None, n_heads, tq), lambda b, g, i: (b, g, 0, i))] + in_specs
        args = [shift] + args
        scratch += [pltpu.VMEM((2, tk_max, width), BF16)]
    else:
        scratch += [pltpu.VMEM((2, tk_max, width), F32), pltpu.VMEM((1, width), F32),
                    pltpu.VMEM((2, 1, width), F32)]
    return pl.pallas_call(
        functools.partial(_attn_kernel, n_src=len(srcs), n_heads=n_heads, shared_kv=shared_kv,
                          fixed_shift=fixed_shift),
        out_shape=jax.ShapeDtypeStruct((bsz, rows * n_kv, t), BF16),
        grid=(bsz, n_kv, t // tq),
        in_specs=in_specs,
        out_specs=pl.BlockSpec((None, rows, tq), lambda b, g, i: (b, g, i)),
        scratch_shapes=scratch,
        compiler_params=_params(("parallel", "parallel", "parallel")),
        name=name,
    )(*args)


def _scan_constants(reverse):
    c = SCAN_C
    t = np.arange(c)[:, None]
    s = np.arange(c)[None, :]
    cum = (s >= t) if reverse else (s <= t)
    msk = np.zeros((N_LEVELS + 1, c, c), np.float32)
    for lvl in range(N_LEVELS):
        w = (c // 2) >> lvl
        same = (t // (2 * w)) == (s // (2 * w))
        t_hi = (t % (2 * w)) >= w
        s_hi = (s % (2 * w)) >= w
        msk[lvl] = (same & ~t_hi & s_hi) if reverse else (same & t_hi & ~s_hi)
    msk[N_LEVELS] = (t == s)
    return jnp.asarray(cum, BF16), jnp.asarray(np.tile(msk, (1, 1, C_HEADS)), F32)


def _boundary_rows(b, lvl, reverse):
    w = (SCAN_C // 2) >> lvl
    width = b.shape[1]
    off = w if reverse else w - 1
    if w == 1:
        odd = lax.broadcasted_iota(jnp.int32, b.shape, 0) % 2 == 1
        if reverse:
            return jnp.where(odd, b, pltpu.roll(b, SCAN_C - 1, 0))
        return jnp.where(odd, pltpu.roll(b, 1, 0), b)
    if w == 2:
        low = lax.broadcasted_iota(jnp.int32, (8, width), 0) < 4
        pieces = []
        for r0 in range(0, SCAN_C, 8):
            first = jnp.broadcast_to(b[r0 + off:r0 + off + 1, :], (8, width))
            second = jnp.broadcast_to(b[r0 + 4 + off:r0 + 5 + off, :], (8, width))
            pieces.append(jnp.where(low, first, second))
        return jnp.concatenate(pieces, axis=0)
    pieces = [jnp.broadcast_to(b[r0 + off:r0 + off + 1, :], (2 * w, width))
              for r0 in range(0, SCAN_C, 2 * w)]
    return pieces[0] if len(pieces) == 1 else jnp.concatenate(pieces, axis=0)


def _hgrn_kernel(q_ref, k_ref, ghi_ref, glo_ref, v_ref, s0_ref, cum_ref, msk_ref, bm_ref,
                 o_ref, sfin_ref, st_ref, kt_ref, *, reverse, n_chunk):
    i = pl.program_id(1)

    @pl.when(i == 0)
    def _():
        st_ref[...] = s0_ref[...]

    bm = bm_ref[...]
    bm16 = bm.astype(BF16)
    cum = cum_ref[...]

    def stack_heads(a):
        a16 = a.astype(BF16)
        return jnp.concatenate([a16] * C_HEADS, axis=0) * bm16

    grp = min(n_chunk, SCAN_GROUP)
    n_grp = n_chunk // grp

    def group(jg, carry):
        jgg = (n_grp - 1 - jg) if reverse else jg
        base = jgg * (grp * SCAN_C)
        order = list(range(grp - 1, -1, -1)) if reverse else list(range(grp))
        rows = [pl.ds(pl.multiple_of(base + c * SCAN_C, SCAN_C), SCAN_C) for c in range(grp)]
        q = [q_ref[r, :] for r in rows]
        k = [k_ref[r, :] for r in rows]
        v = [v_ref[r, :] for r in rows]
        b = [_dot(cum, ghi_ref[r, :]) + _dot(cum, glo_ref[r, :]) for r in rows]
        b_all = [bc[0:1, :] if reverse else bc[SCAN_C - 1:SCAN_C, :] for bc in b]

        q16 = [qc.astype(BF16) for qc in q]
        kst = [stack_heads(kc) for kc in k]
        sc = [msk_ref[N_LEVELS] * _dot_nt(q16[c], kst[c]) for c in range(grp)]
        for lvl in range(N_LEVELS):
            for c in range(grp):
                d = b[c] - _boundary_rows(b[c], lvl, reverse)
                e16 = jnp.exp2(-jnp.abs(d)).astype(BF16)
                kl = kst[c] * jnp.concatenate([e16] * C_HEADS, axis=0)
                kt_ref[c] = kl.T
                sc[c] = sc[c] + msk_ref[lvl] * _dot(q16[c] * e16, kt_ref[c])

        ds = [_dot_tn(v[c].astype(BF16), (k[c] * jnp.exp2(b_all[c] - b[c])).astype(BF16)) * bm
              for c in range(grp)]
        st = st_ref[...]
        st_in = [None] * grp
        for c in order:
            st_in[c] = st
            st = st * jnp.exp2(b_all[c]) + ds[c]
        st_ref[...] = st

        for c in order:
            qdec = q[c] * jnp.exp2(b[c])
            o_ref[rows[c], :] = (_dot(sc[c].astype(BF16), stack_heads(v[c]))
                                 + _dot_nt(qdec.astype(BF16), st_in[c].astype(BF16)))
        return carry

    if n_grp == 1:
        group(0, 0)
    else:
        lax.fori_loop(0, n_grp, group, 0)

    @pl.when(i == pl.num_programs(1) - 1)
    def _():
        sfin_ref[...] = st_ref[...]


def _hgrn(c_slab, g16, s0, consts, block_mask, reverse):
    bsz, t, _ = c_slab.shape
    tb = min(SCAN_BLOCK, t)
    nblk = t // tb
    cum, msk = consts

    def blk(i):
        return (nblk - 1 - i) if reverse else i

    def slab_spec(j):
        return pl.BlockSpec((None, tb, C_W), lambda b, i: (b, blk(i), j))

    g0 = 2 if reverse else 0

    return pl.pallas_call(
        functools.partial(_hgrn_kernel, reverse=reverse, n_chunk=tb // SCAN_C),
        out_shape=[jax.ShapeDtypeStruct((bsz, t, C_W), F32),
                   jax.ShapeDtypeStruct((bsz, C_W, C_W), F32)],
        grid=(bsz, nblk),
        in_specs=[
            slab_spec(C_Q), slab_spec(C_KB if reverse else C_KF), slab_spec(g0), slab_spec(g0 + 1),
            slab_spec(C_V),
            pl.BlockSpec((None, C_W, C_W), lambda b, i: (b, 0, 0)),
            _const_spec(cum.shape, (0, 0)),
            _const_spec(msk.shape, (0, 0, 0)),
            _const_spec(block_mask.shape, (0, 0)),
        ],
        out_specs=[pl.BlockSpec((None, tb, C_W), lambda b, i: (b, blk(i), 0)),
                   pl.BlockSpec((None, C_W, C_W), lambda b, i: (b, 0, 0))],
        scratch_shapes=[pltpu.VMEM((C_W, C_W), F32),
                        pltpu.VMEM((min(tb // SCAN_C, SCAN_GROUP), C_W, C_HEADS * SCAN_C), BF16)],
        compiler_params=_params(("parallel", "arbitrary")),
        name="hgrn_bwd" if reverse else "hgrn_fwd",
    )(c_slab, c_slab, g16, g16, c_slab, s0, cum, msk, block_mask)


POST_TM = 1024
POST_SUB = 256


def _rms(v, gain):
    ms = jnp.mean(v * v, axis=-1, keepdims=True)
    return v * lax.rsqrt(ms + EPS) * gain


def _post_kernel(x_ref, ya_ref, yb_ref, of_ref, ob_ref, gate_ref,
                 gtm_ref, shf_ref, scf_ref, gtf_ref,
                 gpm_ref, gpf_ref, gqf_ref, con_ref, bo_ref,
                 woa_ref, wob_ref, woc_ref, w1_ref, w2_ref, o_ref):
    tm = x_ref.shape[0]
    sub = min(tm, POST_SUB)
    for r0 in range(0, tm, sub):
        rs = slice(r0, r0 + sub)
        o = of_ref[rs, :] + ob_ref[rs, :]
        yc = (_head_rms(o, bo_ref[...], con_ref[...]) * _silu(gate_ref[rs, :])).astype(BF16)
        mix = (_dot_tn(ya_ref[:, rs], woa_ref[...]) + _dot_tn(yb_ref[:, rs], wob_ref[...])
               + _dot(yc, woc_ref[...]))
        x1 = x_ref[rs, :] + gtm_ref[...] * _rms(mix, gpm_ref[...])
        h = _rms(x1, gpf_ref[...]) * (1.0 + scf_ref[...]) + shf_ref[...]
        u = jnp.maximum(_dot(h.astype(BF16), w1_ref[...]), 0.0)
        ff = _dot((u * u).astype(BF16), w2_ref[...])
        o_ref[rs, :] = x1 + gtf_ref[...] * _rms(ff, gqf_ref[...])


def _post(xs, yta, ytb, o_f, o_b, c_slab, mod, w, layer, mod_row):
    bsz, t, _ = xs.shape
    tm = min(POST_TM, t)
    nb_rows = w["n_mod_rows"]

    def mod_spec(j):
        return pl.BlockSpec((None, 1, D_MODEL),
                            lambda b, i: ((layer * nb_rows + mod_row(b)) * 6 + j, 0, 0))

    def vec_spec():
        return _const_spec((None, 1, D_MODEL), (layer, 0, 0))

    in_specs = [
        pl.BlockSpec((None, tm, D_MODEL), lambda b, i: (b, i, 0)),
        pl.BlockSpec((None, A_OUT, tm), lambda b, i: (b, 0, i)),
        pl.BlockSpec((None, B_OUT, tm), lambda b, i: (b, 0, i)),
        pl.BlockSpec((None, tm, C_W), lambda b, i: (b, i, 0)),
        pl.BlockSpec((None, tm, C_W), lambda b, i: (b, i, 0)),
        pl.BlockSpec((None, tm, C_W), lambda b, i: (b, i, C_GATE)),
        mod_spec(2), mod_spec(3), mod_spec(4), mod_spec(5),
        vec_spec(), vec_spec(), vec_spec(),
        _const_spec((None, 1, C_W), (layer, 0, 0)),
        _const_spec((256, 256), (0, 0)),
        _const_spec((None, A_OUT, D_MODEL), (layer, 0, 0)),
        _const_spec((None, B_OUT, D_MODEL), (layer, 0, 0)),
        _const_spec((None, C_OUT, D_MODEL), (layer, 0, 0)),
        _const_spec((None, D_MODEL, D_FF), (layer, 0, 0)),
        _const_spec((None, D_FF, D_MODEL), (layer, 0, 0)),
    ]
    return pl.pallas_call(
        _post_kernel,
        out_shape=jax.ShapeDtypeStruct((bsz, t, D_MODEL), F32),
        grid=(bsz, t // tm),
        in_specs=in_specs,
        out_specs=pl.BlockSpec((None, tm, D_MODEL), lambda b, i: (b, i, 0)),
        compiler_params=_params(("parallel", "parallel")),
        name="post",
    )(xs, yta, ytb, o_f, o_b, c_slab, mod, mod, mod, mod,
      w["g_post_mix"], w["g_pre_ffn"], w["g_post_ffn"], w["c_out_norm"], w["block_ones"],
      w["w_out_a"], w["w_out_b"], w["w_out_c"], w["w_ff1"], w["w_ff2"])


def _w_in_columns():
    src = np.full((N_COL,), -1, np.int64)
    a_k0 = A_HEADS * HEAD_DIM
    a_v0 = a_k0 + A_KV_HEADS * HEAD_DIM
    b_qd0 = a_v0 + A_KV_HEADS * HEAD_DIM
    b_kv0 = b_qd0 + B_Q_RANK
    b_kr0 = b_kv0 + B_KV_RANK
    c0 = b_kr0 + B_ROPE
    src[OFF_AQ:OFF_AQ + 512] = np.arange(512)
    for g in range(A_KV_HEADS):
        for rep in range(2):
            lo = OFF_AK + (2 * g + rep) * HEAD_DIM
            src[lo:lo + HEAD_DIM] = a_k0 + g * HEAD_DIM + np.arange(HEAD_DIM)
    src[OFF_AV:OFF_AV + 128] = a_v0 + np.arange(128)
    src[OFF_BQD:OFF_BQD + B_Q_RANK] = b_qd0 + np.arange(B_Q_RANK)
    src[OFF_BKVD:OFF_BKVD + B_KV_RANK] = b_kv0 + np.arange(B_KV_RANK)
    src[OFF_KPE + B_NOPE:OFF_KPE + B_NOPE + B_ROPE] = b_kr0 + np.arange(B_ROPE)
    src[OFF_CQ:OFF_CQ + 5 * C_W] = c0 + np.arange(5 * C_W)
    return src


def _gather_cols(w, src):
    pieces, i, n = [], 0, len(src)
    while i < n:
        j = i + 1
        if src[i] < 0:
            while j < n and src[j] < 0:
                j += 1
            pieces.append(jnp.zeros(w.shape[:-1] + (j - i,), w.dtype))
        else:
            while j < n and src[j] == src[j - 1] + 1:
                j += 1
            pieces.append(w[..., int(src[i]):int(src[i]) + (j - i)])
        i = j
    return jnp.concatenate(pieces, axis=-1)


def _prepare_weights(p, n_mod_rows):
    w = {"n_mod_rows": n_mod_rows}
    w["w_in"] = _gather_cols(p["w_in"], _w_in_columns()).astype(BF16)
    src = np.full((B_HEADS * LANE,), -1, np.int64)
    for hh in range(B_HEADS):
        src[hh * LANE:hh * LANE + B_NOPE + B_ROPE] = hh * (B_NOPE + B_ROPE) + np.arange(B_NOPE + B_ROPE)
    wq = _gather_cols(p["w_q_up"], src)
    w["w_q_up"] = jnp.pad(wq, ((0, 0), (0, 256 - B_Q_RANK), (0, 0))).astype(BF16)
    src = np.full((B_HEADS * LANE,), -1, np.int64)
    srcv = np.zeros((B_HEADS * B_V,), np.int64)
    for hh in range(B_HEADS):
        src[hh * LANE:hh * LANE + B_NOPE] = hh * (B_NOPE + B_V) + np.arange(B_NOPE)
        srcv[hh * B_V:(hh + 1) * B_V] = hh * (B_NOPE + B_V) + B_NOPE + np.arange(B_V)
    w["w_kv_k"] = _gather_cols(p["w_kv_up"], src).astype(BF16)
    w["w_kv_v"] = _gather_cols(p["w_kv_up"], srcv).astype(BF16)
    w["a_q_norm"] = jnp.tile(p["a_q_norm"], (1, 4))[:, None, :]
    w["a_k_norm"] = jnp.tile(p["a_k_norm"], (1, 4))[:, None, :]
    w["b_q_norm"] = jnp.pad(p["b_q_norm"], ((0, 0), (0, 256 - B_Q_RANK)))[:, None, :]
    w["b_kv_norm"] = p["b_kv_norm"][:, None, :]
    w["c_out_norm"] = jnp.tile(p["c_out_norm"], (1, C_HEADS))[:, None, :]
    for name in ("g_pre_mix", "g_post_mix", "g_pre_ffn", "g_post_ffn"):
        w[name] = p[name][:, None, :]
    head = np.arange(256) // HEAD_DIM
    same_head = head[:, None] == head[None, :]
    w["block_ones"] = jnp.asarray(same_head / float(HEAD_DIM), BF16)
    w["block_mask"] = jnp.asarray(same_head, F32)
    p_lb = jax.nn.softmax(p["c_lower_bounds"].astype(F32), axis=0)
    w["lower"] = jnp.cumsum(p_lb, axis=0) - p_lb[:1]
    w["w_out_a"] = p["w_out"][:, :A_OUT].astype(BF16)
    w["w_out_b"] = p["w_out"][:, A_OUT:A_OUT + B_OUT].astype(BF16)
    w["w_out_c"] = p["w_out"][:, A_OUT + B_OUT:].astype(BF16)
    w["w_ff1"] = p["w_ff1"].astype(BF16)
    w["w_ff2"] = p["w_ff2"].astype(BF16)
    return w


def _rope_tables(n_tok):
    tok = np.arange(n_tok)
    row = (tok // GRID_W).astype(np.float32)[:, None]
    col = (tok % GRID_W).astype(np.float32)[:, None]

    def angles(rot_dim):
        n_freq = rot_dim // 4
        inv = jnp.asarray(ROPE_THETA, F32) ** (-jnp.arange(n_freq, dtype=F32) / n_freq)
        ang = jnp.concatenate([jnp.asarray(row) * inv, jnp.asarray(col) * inv], axis=-1)
        return jnp.cos(ang), jnp.sin(ang)

    zeros = lambda n: jnp.zeros((n_tok, n), F32)
    ones = lambda n: jnp.ones((n_tok, n), F32)
    cos, sin = angles(HEAD_DIM)
    ca = jnp.concatenate([cos, cos, cos, cos], axis=-1)
    s1a = jnp.concatenate([-sin, zeros(32), -sin, zeros(32)], axis=-1)
    s2a = jnp.concatenate([zeros(32), sin, zeros(32), sin], axis=-1)
    cos, sin = angles(B_ROPE)
    cb = jnp.concatenate([ones(64), cos, cos, ones(32)], axis=-1)
    s1b = jnp.concatenate([zeros(64), -sin, zeros(16), zeros(32)], axis=-1)
    s2b = jnp.concatenate([zeros(64), zeros(16), sin, zeros(32)], axis=-1)
    return ca, s1a, s2a, cb, s1b, s2b


def kernel(x, c, ctx, c_ctx, w_ada, b_ada, g_pre_mix, g_post_mix, g_pre_ffn, g_post_ffn, w_in, a_q_norm, a_k_norm, b_q_norm, w_q_up, b_kv_norm, w_kv_up, c_lower_bounds, c_out_norm, w_out, w_ff1, w_ff2):
    bsz, n_lat, _ = x.shape
    depth = w_in.shape[0]
    n_mod_rows = -(-(bsz + 1) // 8) * 8
    params = dict(w_in=w_in, a_q_norm=a_q_norm, a_k_norm=a_k_norm, b_q_norm=b_q_norm, w_q_up=w_q_up,
                  b_kv_norm=b_kv_norm, w_kv_up=w_kv_up, c_lower_bounds=c_lower_bounds,
                  c_out_norm=c_out_norm, w_out=w_out, w_ff1=w_ff1, w_ff2=w_ff2,
                  g_pre_mix=g_pre_mix, g_post_mix=g_post_mix, g_pre_ffn=g_pre_ffn, g_post_ffn=g_post_ffn)
    w = _prepare_weights(params, n_mod_rows)
    rope = _rope_tables(n_lat)
    scan_f = _scan_constants(False)
    scan_b = _scan_constants(True)

    cvec = jnp.concatenate([c, c_ctx[None, :], jnp.zeros((n_mod_rows - bsz - 1, D_MODEL), F32)], axis=0)
    mod = _ada(cvec, w_ada, b_ada).reshape(depth * n_mod_rows * 6, 1, D_MODEL)

    lat_row = lambda b: b
    ctx_row = lambda b: bsz
    zero_state = jnp.zeros((bsz, C_W, C_W), F32)
    attn_a = functools.partial(_attention, n_kv=A_KV_HEADS, n_heads=A_HEADS // A_KV_HEADS, shared_kv=True)
    attn_b = functools.partial(_attention, n_kv=1, n_heads=B_HEADS, shared_kv=False)

    xc = ctx
    for layer in range(depth):
        need_ctx = layer < depth - 1
        fl = _features(x, mod, w, layer, lat_row, rope)
        fc = _features(xc, mod, w, layer, ctx_row, None)
        bound_a = ((HEAD_DIM ** 0.5 * LOG2_E) * jnp.max(jnp.abs(a_q_norm[layer]))
                   * jnp.max(jnp.abs(a_k_norm[layer])))
        a_srcs = [(fl["ak"], fl["avt"]), (fc["ak"], fc["avt"])]
        yta = lax.cond(
            bound_a <= SHIFT_LIMIT,
            lambda q, s, bnd: attn_a(q, s, name="attn_a_fixed", shift=jnp.broadcast_to(
                bnd, (bsz, A_KV_HEADS, A_HEADS // A_KV_HEADS, n_lat))),
            lambda q, s, bnd: attn_a(q, s, name="attn_a"),
            fl["aq"], a_srcs, bound_a)
        k2_max = jnp.maximum(jnp.max(fl["bkn"][:, :, 0, :B_HEADS], axis=1),
                             jnp.max(fc["bkn"][:, :, 0, :B_HEADS], axis=1))
        bound_b = jnp.sqrt(fl["bqn"][:, :B_HEADS, :] * k2_max[:, :, None]) * NORM_SLACK
        b_srcs = [(fl["bk"], fl["bvt"]), (fc["bk"], fc["bvt"])]
        ytb = lax.cond(
            jnp.max(bound_b) <= SHIFT_LIMIT,
            lambda q, s, bnd: attn_b(q, s, name="attn_b_fixed", shift=bnd[:, None]),
            lambda q, s, bnd: attn_b(q, s, name="attn_b"),
            fl["bq"], b_srcs, bound_b)
        ocf, s_f = _hgrn(fc["c"], fc["g"], zero_state, scan_f, w["block_mask"], False)
        ocb, s_b = _hgrn(fc["c"], fc["g"], zero_state, scan_b, w["block_mask"], True)
        olf, _ = _hgrn(fl["c"], fl["g"], s_f, scan_f, w["block_mask"], False)
        olb, _ = _hgrn(fl["c"], fl["g"], s_b, scan_b, w["block_mask"], True)
        x_new = _post(x, yta, ytb, olf, olb, fl["c"], mod, w, layer, lat_row)
        if need_ctx:
            yta_c = attn_a(fc["aq"], [(fc["ak"], fc["avt"])], name="attn_a_ctx")
            ytb_c = attn_b(fc["bq"], [(fc["bk"], fc["bvt"])], name="attn_b_ctx")
            xc = _post(xc, yta_c, ytb_c, ocf, ocb, fc["c"], mod, w, layer, ctx_row)
        x = x_new
    return x
```

```python
import functools

import numpy as np
import jax
import jax.numpy as jnp
from jax import lax
from jax.experimental import pallas as pl
from jax.experimental.pallas import tpu as pltpu

F32 = jnp.float32
BF16 = jnp.bfloat16

D_MODEL = 1024
GRID_W = 64
HEAD_DIM = 64
A_HEADS = 8
A_KV_HEADS = 2
B_HEADS = 4
B_Q_RANK = 192
B_KV_RANK = 128
B_NOPE = 64
B_ROPE = 32
B_V = 64
C_HEADS = 4
C_DK = 64
C_DV = 64
D_FF = 4 * D_MODEL
A_OUT = A_HEADS * HEAD_DIM
B_OUT = B_HEADS * B_V
C_OUT = C_HEADS * C_DV
C_W = C_HEADS * C_DK
ROPE_THETA = 10000.0
EPS = 1e-6
F_TINY = 1e-30

LANE = 128
VMEM_LIMIT = 56 * 1024 * 1024

OFF_AQ = 0
OFF_AK = 512
OFF_AV = 768
OFF_BQD = 896
OFF_BKVD = 1152
OFF_KPE = 1280
OFF_CQ = 1408
OFF_CFF = 1664
OFF_CFB = 1920
OFF_CI = 2176
OFF_CG = 2432
N_COL = 2688

C_Q, C_KF, C_KB, C_V, C_GATE = range(5)
N_SLAB = 5

SCAN_C = 64
N_LEVELS = 6
SCAN_BLOCK = 2048
SCAN_GROUP = 8


def _dot(a, b):
    return jnp.dot(a, b, preferred_element_type=F32)


def _dot_nt(a, b):
    return lax.dot_general(a, b, (((1,), (1,)), ((), ())), preferred_element_type=F32)


def _dot_tn(a, b):
    return lax.dot_general(a, b, (((0,), (0,)), ((), ())), preferred_element_type=F32)


def _sigmoid_pair(z):
    e = jnp.exp(-jnp.abs(z))
    inv = 1.0 / (1.0 + e)
    small = e * inv
    pos = z >= 0
    return jnp.where(pos, inv, small), jnp.where(pos, small, inv)


def _silu(z):
    s, _ = _sigmoid_pair(z)
    return z * s


def _const_spec(shape, index):
    return pl.BlockSpec(shape, lambda *_: index, pipeline_mode=pl.Buffered(1))


def _params(sem):
    return pltpu.CompilerParams(dimension_semantics=sem, vmem_limit_bytes=VMEM_LIMIT)


def _ada_kernel(c_ref, w_ref, b_ref, o_ref):
    a = _silu(c_ref[...])
    w = w_ref[...]
    a_hi = a.astype(BF16)
    a_lo = (a - a_hi.astype(F32)).astype(BF16)
    w_hi = w.astype(BF16)
    w_lo = (w - w_hi.astype(F32)).astype(BF16)
    acc = _dot(a_hi, w_hi) + (_dot(a_hi, w_lo) + _dot(a_lo, w_hi))
    o_ref[...] = acc + b_ref[...]


def _ada(cvec, w_ada, b_ada):
    depth = w_ada.shape[0]
    rows = cvec.shape[0]
    n_blk = w_ada.shape[2] // D_MODEL
    return pl.pallas_call(
        _ada_kernel,
        out_shape=jax.ShapeDtypeStruct((depth, rows, n_blk * D_MODEL), F32),
        grid=(depth, n_blk),
        in_specs=[
            pl.BlockSpec((rows, D_MODEL), lambda l, j: (0, 0)),
            pl.BlockSpec((None, D_MODEL, D_MODEL), lambda l, j: (l, 0, j)),
            pl.BlockSpec((None, None, 1, D_MODEL), lambda l, j: (l, j, 0, 0)),
        ],
        out_specs=pl.BlockSpec((None, rows, D_MODEL), lambda l, j: (l, 0, j)),
        compiler_params=_params(("arbitrary", "arbitrary")),
        name="ada",
    )(cvec, w_ada, b_ada.reshape(depth, n_blk, 1, D_MODEL))


def _rope128(v, cos, s1, s2, half):
    up = pltpu.roll(v, LANE - half, 1)
    dn = pltpu.roll(v, half, 1)
    return v * cos + up * s1 + dn * s2


def _head_rms(v, bo, gain):
    sq = v * v
    hi = sq.astype(BF16)
    lo = (sq - hi.astype(F32)).astype(BF16)
    ms = _dot(hi, bo) + _dot(lo, bo)
    return v * lax.rsqrt(ms + EPS) * gain


FEAT_TM = 1024
FEAT_SUB = 256
KV_CHUNK = 512


def _feat_kernel(*refs, use_rope):
    (x_ref, sh_ref, sc_ref, gpre_ref, win_ref, wq_ref, wkk_ref, wkv_ref,
     aqn_ref, akn_ref, bqn_ref, bkvn_ref, bo_ref, lb_ref) = refs[:14]
    rest = refs[14:]
    if use_rope:
        ca_ref, s1a_ref, s2a_ref, cb_ref, s1b_ref, s2b_ref = rest[:6]
        rest = rest[6:]
    aq_o, ak_o, avt_o, bq_o, bk_o, bvt_o, c_o, g_o, bqn_o, bkn_o = rest
    bo = bo_ref[...]
    tm = x_ref.shape[0]
    sub = min(tm, FEAT_SUB)
    b_k2 = [None] * B_HEADS
    bqn_o[B_HEADS:, :] = jnp.zeros((bqn_o.shape[0] - B_HEADS, tm), F32)

    for r0 in range(0, tm, sub):
        rs = slice(r0, r0 + sub)

        def rope_a(v, rs=rs):
            if not use_rope:
                return v
            return _rope128(v, ca_ref[rs, :], s1a_ref[rs, :], s2a_ref[rs, :], HEAD_DIM // 2)

        def rope_b(v, rs=rs):
            if not use_rope:
                return v
            return _rope128(v, cb_ref[rs, :], s1b_ref[rs, :], s2b_ref[rs, :], B_ROPE // 2)

        x = x_ref[rs, :]
        ms = jnp.mean(x * x, axis=-1, keepdims=True)
        h = x * lax.rsqrt(ms + EPS) * gpre_ref[...]
        h = h * (1.0 + sc_ref[...]) + sh_ref[...]
        p = _dot(h.astype(BF16), win_ref[...])

        aqn = aqn_ref[...]
        for half in range(2):
            v = _head_rms(p[:, OFF_AQ + 256 * half:OFF_AQ + 256 * (half + 1)], bo, aqn)
            for s in range(2):
                blk = rope_a(v[:, LANE * s:LANE * (s + 1)]) * (HEAD_DIM ** -0.5 * LOG2_E)
                lo = 256 * half + LANE * s
                aq_o[lo:lo + LANE, rs] = blk.T.astype(BF16)
        v = _head_rms(p[:, OFF_AK:OFF_AK + 256], bo, akn_ref[...])
        for s in range(2):
            ak_o[rs, LANE * s:LANE * (s + 1)] = rope_a(v[:, LANE * s:LANE * (s + 1)]).astype(BF16)
        kv_w = avt_o.shape[2]
        cs = slice(r0 % kv_w, r0 % kv_w + sub)
        avt_o[r0 // kv_w, :, cs] = p[:, OFF_AV:OFF_AV + LANE].T.astype(BF16)

        bqd = p[:, OFF_BQD:OFF_BQD + 256]
        ms = jnp.sum(bqd * bqd, axis=-1, keepdims=True) * (1.0 / B_Q_RANK)
        qn = (bqd * lax.rsqrt(ms + EPS) * bqn_ref[...]).astype(BF16)
        bq = _dot(qn, wq_ref[...])
        bkvd = p[:, OFF_BKVD:OFF_BKVD + LANE]
        ms = jnp.mean(bkvd * bkvd, axis=-1, keepdims=True)
        kvn = (bkvd * lax.rsqrt(ms + EPS) * bkvn_ref[...]).astype(BF16)
        bkn = _dot(kvn, wkk_ref[...])
        bv = _dot(kvn, wkv_ref[...])
        kpe = rope_b(p[:, OFF_KPE:OFF_KPE + LANE])
        b_scale = (B_NOPE + B_ROPE) ** -0.5 * LOG2_E
        for hh in range(B_HEADS):
            sl = slice(LANE * hh, LANE * (hh + 1))
            qt = (rope_b(bq[:, sl]) * b_scale).T
            kh = bkn[:, sl] + kpe
            bq_o[sl, rs] = qt.astype(BF16)
            bk_o[rs, sl] = kh.astype(BF16)
            bqn_o[hh:hh + 1, rs] = jnp.sum(qt * qt, axis=0, keepdims=True)
            k2 = jnp.max(jnp.sum(kh * kh, axis=-1, keepdims=True), axis=0, keepdims=True)
            b_k2[hh] = k2 if b_k2[hh] is None else jnp.maximum(b_k2[hh], k2)
        bvt_o[r0 // kv_w, :, cs] = bv.T.astype(BF16)

        c_o[rs, C_W * C_Q:C_W * (C_Q + 1)] = _silu(p[:, OFF_CQ:OFF_CQ + C_W])
        for d, (off, ck) in enumerate(((OFF_CFF, C_KF), (OFF_CFB, C_KB))):
            lb = lb_ref[d:d + 1, :]
            sp, sn = _sigmoid_pair(p[:, off:off + C_W])
            f = lb + (1.0 - lb) * sp
            g2 = jnp.log2(jnp.maximum(f, F_TINY))
            hi = g2.astype(BF16)
            g_o[rs, C_W * 2 * d:C_W * (2 * d + 1)] = hi
            g_o[rs, C_W * (2 * d + 1):C_W * (2 * d + 2)] = (g2 - hi.astype(F32)).astype(BF16)
            c_o[rs, C_W * ck:C_W * (ck + 1)] = (1.0 - lb) * sn
        c_o[rs, C_W * C_V:C_W * (C_V + 1)] = p[:, OFF_CI:OFF_CI + C_W]
        c_o[rs, C_W * C_GATE:C_W * (C_GATE + 1)] = p[:, OFF_CG:OFF_CG + C_W]

    lane = lax.broadcasted_iota(jnp.int32, bkn_o.shape, 1)
    row = jnp.zeros(bkn_o.shape, F32)
    for hh in range(B_HEADS):
        row = jnp.where(lane == hh, jnp.broadcast_to(b_k2[hh], bkn_o.shape), row)
    bkn_o[...] = row


def _features(xs, mod, w, layer, mod_row, rope):
    bsz, t, _ = xs.shape
    tm = min(FEAT_TM, t)
    nt = t // tm
    kv_w = min(KV_CHUNK, t)
    n_kv = tm // kv_w
    nb_rows = w["n_mod_rows"]

    def mod_spec(j):
        return pl.BlockSpec((None, 1, D_MODEL),
                            lambda b, i: ((layer * nb_rows + mod_row(b)) * 6 + j, 0, 0))

    in_specs = [
        pl.BlockSpec((None, tm, D_MODEL), lambda b, i: (b, i, 0)),
        mod_spec(0), mod_spec(1),
        _const_spec((None, 1, D_MODEL), (layer, 0, 0)),
        _const_spec((None, D_MODEL, N_COL), (layer, 0, 0)),
        _const_spec((None, 256, 512), (layer, 0, 0)),
        _const_spec((None, LANE, 512), (layer, 0, 0)),
        _const_spec((None, LANE, 256), (layer, 0, 0)),
        _const_spec((None, 1, 256), (layer, 0, 0)),
        _const_spec((None, 1, 256), (layer, 0, 0)),
        _const_spec((None, 1, 256), (layer, 0, 0)),
        _const_spec((None, 1, LANE), (layer, 0, 0)),
        _const_spec((256, 256), (0, 0)),
        _const_spec((None, 2, C_W), (layer, 0, 0)),
    ]
    args = [xs, mod, mod, w["g_pre_mix"], w["w_in"], w["w_q_up"], w["w_kv_k"], w["w_kv_v"],
            w["a_q_norm"], w["a_k_norm"], w["b_q_norm"], w["b_kv_norm"], w["block_ones"], w["lower"]]
    if rope is not None:
        in_specs += [pl.BlockSpec((tm, LANE), lambda b, i: (i, 0))] * 6
        args += list(rope)
    out_shape = [
        jax.ShapeDtypeStruct((bsz, 512, t), BF16),
        jax.ShapeDtypeStruct((bsz, t, 256), BF16),
        jax.ShapeDtypeStruct((bsz, t // kv_w, LANE, kv_w), BF16),
        jax.ShapeDtypeStruct((bsz, 512, t), BF16),
        jax.ShapeDtypeStruct((bsz, t, 512), BF16),
        jax.ShapeDtypeStruct((bsz, t // kv_w, 256, kv_w), BF16),
        jax.ShapeDtypeStruct((bsz, t, N_SLAB * C_W), F32),
        jax.ShapeDtypeStruct((bsz, t, 4 * C_W), BF16),
        jax.ShapeDtypeStruct((bsz, 8, t), F32),
        jax.ShapeDtypeStruct((bsz, nt, 8, LANE), F32),
    ]
    out_specs = [
        pl.BlockSpec((None, 512, tm), lambda b, i: (b, 0, i)),
        pl.BlockSpec((None, tm, 256), lambda b, i: (b, i, 0)),
        pl.BlockSpec((None, n_kv, LANE, kv_w), lambda b, i: (b, i, 0, 0)),
        pl.BlockSpec((None, 512, tm), lambda b, i: (b, 0, i)),
        pl.BlockSpec((None, tm, 512), lambda b, i: (b, i, 0)),
        pl.BlockSpec((None, n_kv, 256, kv_w), lambda b, i: (b, i, 0, 0)),
        pl.BlockSpec((None, tm, N_SLAB * C_W), lambda b, i: (b, i, 0)),
        pl.BlockSpec((None, tm, 4 * C_W), lambda b, i: (b, i, 0)),
        pl.BlockSpec((None, 8, tm), lambda b, i: (b, 0, i)),
        pl.BlockSpec((None, None, 8, LANE), lambda b, i: (b, i, 0, 0)),
    ]
    outs = pl.pallas_call(
        functools.partial(_feat_kernel, use_rope=rope is not None),
        out_shape=out_shape, grid=(bsz, nt), in_specs=in_specs, out_specs=out_specs,
        compiler_params=_params(("parallel", "parallel")),
        name="feat_rope" if rope is not None else "feat_ctx",
    )(*args)
    return dict(zip(("aq", "ak", "avt", "bq", "bk", "bvt", "c", "g", "bqn", "bkn"), outs))


ACC_ROWS = B_V + 16
NEG_BIG = -1e30
LOG2_E = 1.4426950408889634
ATTN_TQ = 1024
SHIFT_LIMIT = 50.0
NORM_SLACK = 1.02


def _attn_kernel(*refs, n_src, n_heads, shared_kv, fixed_shift):
    if fixed_shift:
        shift_ref = refs[0]
        shift = jnp.concatenate([shift_ref[h:h + 1, :] for h in range(n_heads)], axis=1)
        refs = refs[1:]
    q_ref = refs[0]
    src_refs = refs[1:1 + 2 * n_src]
    o_ref = refs[1 + 2 * n_src]
    if fixed_shift:
        rhs_scr, acc_scr, st_scr = refs[2 + 2 * n_src:]
    else:
        rhs_scr, acc_scr, st_scr, m_scr, mx_scr = refs[2 + 2 * n_src:]
    tq = q_ref.shape[1]
    n_grp, k_dim, grp_w = rhs_scr.shape
    heads_per_grp = n_heads // n_grp

    if shared_kv:
        for h in range(n_heads):
            rhs_scr[0, :, tq * h:tq * (h + 1)] = q_ref[HEAD_DIM * h:HEAD_DIM * (h + 1), :]
    else:
        rhs_scr[...] = jnp.zeros(rhs_scr.shape, BF16)
        for h in range(n_heads):
            g, j = divmod(h, heads_per_grp)
            rhs_scr[g, LANE * j:LANE * (j + 1), tq * j:tq * (j + 1)] = q_ref[LANE * h:LANE * (h + 1), :]
    if not fixed_shift:
        m_scr[...] = jnp.full(m_scr.shape, NEG_BIG, F32)
    acc_scr[...] = jnp.zeros(acc_scr.shape, F32)

    def stage(s, c, slot):
        k_ref, vt_ref = src_refs[2 * s], src_refs[2 * s + 1]
        tk = vt_ref.shape[2]
        rows = pl.ds(pl.multiple_of(c * tk, tk), tk)
        for g in range(n_grp):
            cols = slice(grp_w * g, grp_w * (g + 1))
            kc = k_ref[rows, 0:k_dim] if shared_kv else k_ref[rows, k_dim * g:k_dim * (g + 1)]
            st = _dot(kc, rhs_scr[g])
            if fixed_shift:
                st_scr[slot, 0:tk, cols] = jnp.exp2(st - shift[:, cols]).astype(BF16)
            else:
                st_scr[slot, 0:tk, cols] = st
                mx_scr[slot, :, cols] = jnp.max(st, axis=0, keepdims=True)

    def consume(s, c, slot):
        vt_ref = src_refs[2 * s + 1]
        tk = vt_ref.shape[2]
        ones = jnp.ones((ACC_ROWS - B_V, tk), BF16)
        for g in range(n_grp):
            cols = slice(grp_w * g, grp_w * (g + 1))
            if fixed_shift:
                pt = st_scr[slot, 0:tk, cols]
                alpha = None
            else:
                m_old = m_scr[:, cols]
                m_new = jnp.maximum(m_old, mx_scr[slot, :, cols])
                pt = jnp.exp2(st_scr[slot, 0:tk, cols] - m_new).astype(BF16)
                alpha = jnp.exp2(m_old - m_new)
                m_scr[:, cols] = m_new
            if shared_kv:
                vt = jnp.concatenate([vt_ref[c], ones], axis=0)
                prev = acc_scr[:, cols] if fixed_shift else alpha * acc_scr[:, cols]
                acc_scr[:, cols] = prev + _dot(vt, pt)
            else:
                for j in range(heads_per_grp):
                    h = g * heads_per_grp + j
                    hc = slice(tq * h, tq * (h + 1))
                    lc = slice(tq * j, tq * (j + 1))
                    vt = jnp.concatenate([vt_ref[c, B_V * h:B_V * (h + 1), :], ones], axis=0)
                    prev = acc_scr[:, hc] if fixed_shift else alpha[:, lc] * acc_scr[:, hc]
                    acc_scr[:, hc] = prev + _dot(vt, pt[:, lc])

    n0 = src_refs[1].shape[0]
    n_loop = (n0 - 2) // 2 if n0 >= 4 else 0
    stage(0, 0, 0)
    if n_loop:
        def pair(i, carry):
            stage(0, 2 * i + 1, 1)
            consume(0, 2 * i, 0)
            stage(0, 2 * i + 2, 0)
            consume(0, 2 * i + 1, 1)
            return carry
        lax.fori_loop(0, n_loop, pair, 0)
    tail = [(0, c) for c in range(2 * n_loop, n0)]
    tail += [(s, c) for s in range(1, n_src) for c in range(src_refs[2 * s + 1].shape[0])]
    for i, (s, c) in enumerate(tail):
        if i + 1 < len(tail):
            stage(*tail[i + 1], (i + 1) % 2)
        consume(s, c, i % 2)

    for h in range(n_heads):
        acc = acc_scr[:, tq * h:tq * (h + 1)]
        o_ref[B_V * h:B_V * (h + 1), :] = (acc[:B_V] * (1.0 / acc[B_V:B_V + 1])).astype(BF16)


def _attention(q, srcs, *, n_kv, n_heads, shared_kv, name, shift=None):
    fixed_shift = shift is not None
    bsz, _, t = q.shape
    tq = min(ATTN_TQ * (2 if shared_kv else 1) * (2 if fixed_shift else 1), t)
    q_r = (HEAD_DIM if shared_kv else LANE) * n_heads
    k_w = LANE * (1 if shared_kv else n_heads)
    v_r = B_V * (1 if shared_kv else n_heads)
    rhs_shape = (1, HEAD_DIM, n_heads * tq) if shared_kv else (n_heads // 2, 2 * LANE, 2 * tq)
    in_specs = [pl.BlockSpec((None, q_r, tq), lambda b, g, i: (b, g, i))]
    args = [q]
    for k, vt in srcs:
        length = k.shape[1]
        n_chunk, tk = vt.shape[1], vt.shape[3]
        in_specs.append(pl.BlockSpec((None, length, k_w), lambda b, g, i: (b, 0, g)))
        in_specs.append(pl.BlockSpec((None, n_chunk, v_r, tk), lambda b, g, i: (b, 0, g, 0)))
        args += [k, vt]
    rows = B_V * n_heads
    width = n_heads * tq
    tk_max = max(vt.shape[3] for _, vt in srcs)
    scratch = [pltpu.VMEM(rhs_shape, BF16), pltpu.VMEM((ACC_ROWS, width), F32)]
    if fixed_shift:
        if shift.ndim == 0:
            shift = jnp.broadcast_to(shift, (1, 1, n_heads, tq))
            in_specs = [pl.BlockSpec((None, None, n_heads, tq), lambda b, g, i: (0, 0, 0, 0))] + in_specs
        else:
            in_specs = [pl.BlockSpec((None, None, n_heads, tq), lambda b, g, i: (b, g, 0, i))] + in_specs
        args = [shift] + args
        scratch += [pltpu.VMEM((2, tk_max, width), BF16)]
    else:
        scratch += [pltpu.VMEM((2, tk_max, width), F32), pltpu.VMEM((1, width), F32),
                    pltpu.VMEM((2, 1, width), F32)]
    return pl.pallas_call(
        functools.partial(_attn_kernel, n_src=len(srcs), n_heads=n_heads, shared_kv=shared_kv,
                          fixed_shift=fixed_shift),
        out_shape=jax.ShapeDtypeStruct((bsz, rows * n_kv, t), BF16),
        grid=(bsz, n_kv, t // tq),
        in_specs=in_specs,
        out_specs=pl.BlockSpec((None, rows, tq), lambda b, g, i: (b, g, i)),
        scratch_shapes=scratch,
        compiler_params=_params(("parallel", "parallel", "parallel")),
        name=name,
    )(*args)


def _scan_constants(reverse):
    c = SCAN_C
    t = np.arange(c)[:, None]
    s = np.arange(c)[None, :]
    cum = (s >= t) if reverse else (s <= t)
    msk = np.zeros((N_LEVELS + 1, c, c), np.float32)
    for lvl in range(N_LEVELS):
        w = (c // 2) >> lvl
        same = (t // (2 * w)) == (s // (2 * w))
        t_hi = (t % (2 * w)) >= w
        s_hi = (s % (2 * w)) >= w
        msk[lvl] = (same & ~t_hi & s_hi) if reverse else (same & t_hi & ~s_hi)
    msk[N_LEVELS] = (t == s)
    return jnp.asarray(cum, BF16), jnp.asarray(np.tile(msk, (1, 1, C_HEADS)), F32)


def _boundary_rows(b, lvl, reverse):
    w = (SCAN_C // 2) >> lvl
    width = b.shape[1]
    off = w if reverse else w - 1
    if w == 1:
        odd = lax.broadcasted_iota(jnp.int32, b.shape, 0) % 2 == 1
        if reverse:
            return jnp.where(odd, b, pltpu.roll(b, SCAN_C - 1, 0))
        return jnp.where(odd, pltpu.roll(b, 1, 0), b)
    if w == 2:
        low = lax.broadcasted_iota(jnp.int32, (8, width), 0) < 4
        pieces = []
        for r0 in range(0, SCAN_C, 8):
            first = jnp.broadcast_to(b[r0 + off:r0 + off + 1, :], (8, width))
            second = jnp.broadcast_to(b[r0 + 4 + off:r0 + 5 + off, :], (8, width))
            pieces.append(jnp.where(low, first, second))
        return jnp.concatenate(pieces, axis=0)
    pieces = [jnp.broadcast_to(b[r0 + off:r0 + off + 1, :], (2 * w, width))
              for r0 in range(0, SCAN_C, 2 * w)]
    return pieces[0] if len(pieces) == 1 else jnp.concatenate(pieces, axis=0)


def _hgrn_kernel(q_ref, k_ref, ghi_ref, glo_ref, v_ref, s0_ref, cum_ref, msk_ref, bm_ref,
                 o_ref, sfin_ref, st_ref, kt_ref, *, reverse, n_chunk):
    i = pl.program_id(1)

    @pl.when(i == 0)
    def _():
        st_ref[...] = s0_ref[...]

    bm = bm_ref[...]
    bm16 = bm.astype(BF16)
    cum = cum_ref[...]

    def stack_heads(a):
        a16 = a.astype(BF16)
        return jnp.concatenate([a16] * C_HEADS, axis=0) * bm16

    grp = min(n_chunk, SCAN_GROUP)
    n_grp = n_chunk // grp

    def group(jg, carry):
        jgg = (n_grp - 1 - jg) if reverse else jg
        base = jgg * (grp * SCAN_C)
        order = list(range(grp - 1, -1, -1)) if reverse else list(range(grp))
        rows = [pl.ds(pl.multiple_of(base + c * SCAN_C, SCAN_C), SCAN_C) for c in range(grp)]
        q = [q_ref[r, :] for r in rows]
        k = [k_ref[r, :] for r in rows]
        v = [v_ref[r, :] for r in rows]
        b = [_dot(cum, ghi_ref[r, :]) + _dot(cum, glo_ref[r, :]) for r in rows]
        b_all = [bc[0:1, :] if reverse else bc[SCAN_C - 1:SCAN_C, :] for bc in b]

        q16 = [qc.astype(BF16) for qc in q]
        kst = [stack_heads(kc) for kc in k]
        sc = [msk_ref[N_LEVELS] * _dot_nt(q16[c], kst[c]) for c in range(grp)]
        for lvl in range(N_LEVELS):
            for c in range(grp):
                d = b[c] - _boundary_rows(b[c], lvl, reverse)
                e16 = jnp.exp2(-jnp.abs(d)).astype(BF16)
                kl = kst[c] * jnp.concatenate([e16] * C_HEADS, axis=0)
                kt_ref[c] = kl.T
                sc[c] = sc[c] + msk_ref[lvl] * _dot(q16[c] * e16, kt_ref[c])

        ds = [_dot_tn(v[c].astype(BF16), (k[c] * jnp.exp2(b_all[c] - b[c])).astype(BF16)) * bm
              for c in range(grp)]
        st = st_ref[...]
        st_in = [None] * grp
        for c in order:
            st_in[c] = st
            st = st * jnp.exp2(b_all[c]) + ds[c]
        st_ref[...] = st

        for c in order:
            qdec = q[c] * jnp.exp2(b[c])
            o_ref[rows[c], :] = (_dot(sc[c].astype(BF16), stack_heads(v[c]))
                                 + _dot_nt(qdec.astype(BF16), st_in[c].astype(BF16)))
        return carry

    if n_grp == 1:
        group(0, 0)
    else:
        lax.fori_loop(0, n_grp, group, 0)

    @pl.when(i == pl.num_programs(1) - 1)
    def _():
        sfin_ref[...] = st_ref[...]


def _hgrn(c_slab, g16, s0, consts, block_mask, reverse):
    bsz, t, _ = c_slab.shape
    tb = min(SCAN_BLOCK, t)
    nblk = t // tb
    cum, msk = consts

    def blk(i):
        return (nblk - 1 - i) if reverse else i

    def slab_spec(j):
        return pl.BlockSpec((None, tb, C_W), lambda b, i: (b, blk(i), j))

    g0 = 2 if reverse else 0

    return pl.pallas_call(
        functools.partial(_hgrn_kernel, reverse=reverse, n_chunk=tb // SCAN_C),
        out_shape=[jax.ShapeDtypeStruct((bsz, t, C_W), F32),
                   jax.ShapeDtypeStruct((bsz, C_W, C_W), F32)],
        grid=(bsz, nblk),
        in_specs=[
            slab_spec(C_Q), slab_spec(C_KB if reverse else C_KF), slab_spec(g0), slab_spec(g0 + 1),
            slab_spec(C_V),
            pl.BlockSpec((None, C_W, C_W), lambda b, i: (b, 0, 0)),
            _const_spec(cum.shape, (0, 0)),
            _const_spec(msk.shape, (0, 0, 0)),
            _const_spec(block_mask.shape, (0, 0)),
        ],
        out_specs=[pl.BlockSpec((None, tb, C_W), lambda b, i: (b, blk(i), 0)),
                   pl.BlockSpec((None, C_W, C_W), lambda b, i: (b, 0, 0))],
        scratch_shapes=[pltpu.VMEM((C_W, C_W), F32),
                        pltpu.VMEM((min(tb // SCAN_C, SCAN_GROUP), C_W, C_HEADS * SCAN_C), BF16)],
        compiler_params=_params(("parallel", "arbitrary")),
        name="hgrn_bwd" if reverse else "hgrn_fwd",
    )(c_slab, c_slab, g16, g16, c_slab, s0, cum, msk, block_mask)


POST_TM = 1024
POST_SUB = 256


def _rms(v, gain):
    ms = jnp.mean(v * v, axis=-1, keepdims=True)
    return v * lax.rsqrt(ms + EPS) * gain


def _post_kernel(x_ref, ya_ref, yb_ref, of_ref, ob_ref, gate_ref,
                 gtm_ref, shf_ref, scf_ref, gtf_ref,
                 gpm_ref, gpf_ref, gqf_ref, con_ref, bo_ref,
                 woa_ref, wob_ref, woc_ref, w1_ref, w2_ref, o_ref):
    tm = x_ref.shape[0]
    sub = min(tm, POST_SUB)
    for r0 in range(0, tm, sub):
        rs = slice(r0, r0 + sub)
        o = of_ref[rs, :] + ob_ref[rs, :]
        yc = (_head_rms(o, bo_ref[...], con_ref[...]) * _silu(gate_ref[rs, :])).astype(BF16)
        mix = (_dot_tn(ya_ref[:, rs], woa_ref[...]) + _dot_tn(yb_ref[:, rs], wob_ref[...])
               + _dot(yc, woc_ref[...]))
        x1 = x_ref[rs, :] + gtm_ref[...] * _rms(mix, gpm_ref[...])
        h = _rms(x1, gpf_ref[...]) * (1.0 + scf_ref[...]) + shf_ref[...]
        u = jnp.maximum(_dot(h.astype(BF16), w1_ref[...]), 0.0)
        ff = _dot((u * u).astype(BF16), w2_ref[...])
        o_ref[rs, :] = x1 + gtf_ref[...] * _rms(ff, gqf_ref[...])


def _post(xs, yta, ytb, o_f, o_b, c_slab, mod, w, layer, mod_row):
    bsz, t, _ = xs.shape
    tm = min(POST_TM, t)
    nb_rows = w["n_mod_rows"]

    def mod_spec(j):
        return pl.BlockSpec((None, 1, D_MODEL),
                            lambda b, i: ((layer * nb_rows + mod_row(b)) * 6 + j, 0, 0))

    def vec_spec():
        return _const_spec((None, 1, D_MODEL), (layer, 0, 0))

    in_specs = [
        pl.BlockSpec((None, tm, D_MODEL), lambda b, i: (b, i, 0)),
        pl.BlockSpec((None, A_OUT, tm), lambda b, i: (b, 0, i)),
        pl.BlockSpec((None, B_OUT, tm), lambda b, i: (b, 0, i)),
        pl.BlockSpec((None, tm, C_W), lambda b, i: (b, i, 0)),
        pl.BlockSpec((None, tm, C_W), lambda b, i: (b, i, 0)),
        pl.BlockSpec((None, tm, C_W), lambda b, i: (b, i, C_GATE)),
        mod_spec(2), mod_spec(3), mod_spec(4), mod_spec(5),
        vec_spec(), vec_spec(), vec_spec(),
        _const_spec((None, 1, C_W), (layer, 0, 0)),
        _const_spec((256, 256), (0, 0)),
        _const_spec((None, A_OUT, D_MODEL), (layer, 0, 0)),
        _const_spec((None, B_OUT, D_MODEL), (layer, 0, 0)),
        _const_spec((None, C_OUT, D_MODEL), (layer, 0, 0)),
        _const_spec((None, D_MODEL, D_FF), (layer, 0, 0)),
        _const_spec((None, D_FF, D_MODEL), (layer, 0, 0)),
    ]
    return pl.pallas_call(
        _post_kernel,
        out_shape=jax.ShapeDtypeStruct((bsz, t, D_MODEL), F32),
        grid=(bsz, t // tm),
        in_specs=in_specs,
        out_specs=pl.BlockSpec((None, tm, D_MODEL), lambda b, i: (b, i, 0)),
        compiler_params=_params(("parallel", "parallel")),
        name="post",
    )(xs, yta, ytb, o_f, o_b, c_slab, mod, mod, mod, mod,
      w["g_post_mix"], w["g_pre_ffn"], w["g_post_ffn"], w["c_out_norm"], w["block_ones"],
      w["w_out_a"], w["w_out_b"], w["w_out_c"], w["w_ff1"], w["w_ff2"])


def _w_in_columns():
    src = np.full((N_COL,), -1, np.int64)
    a_k0 = A_HEADS * HEAD_DIM
    a_v0 = a_k0 + A_KV_HEADS * HEAD_DIM
    b_qd0 = a_v0 + A_KV_HEADS * HEAD_DIM
    b_kv0 = b_qd0 + B_Q_RANK
    b_kr0 = b_kv0 + B_KV_RANK
    c0 = b_kr0 + B_ROPE
    src[OFF_AQ:OFF_AQ + 512] = np.arange(512)
    for g in range(A_KV_HEADS):
        for rep in range(2):
            lo = OFF_AK + (2 * g + rep) * HEAD_DIM
            src[lo:lo + HEAD_DIM] = a_k0 + g * HEAD_DIM + np.arange(HEAD_DIM)
    src[OFF_AV:OFF_AV + 128] = a_v0 + np.arange(128)
    src[OFF_BQD:OFF_BQD + B_Q_RANK] = b_qd0 + np.arange(B_Q_RANK)
    src[OFF_BKVD:OFF_BKVD + B_KV_RANK] = b_kv0 + np.arange(B_KV_RANK)
    src[OFF_KPE + B_NOPE:OFF_KPE + B_NOPE + B_ROPE] = b_kr0 + np.arange(B_ROPE)
    src[OFF_CQ:OFF_CQ + 5 * C_W] = c0 + np.arange(5 * C_W)
    return src


def _gather_cols(w, src):
    pieces, i, n = [], 0, len(src)
    while i < n:
        j = i + 1
        if src[i] < 0:
            while j < n and src[j] < 0:
                j += 1
            pieces.append(jnp.zeros(w.shape[:-1] + (j - i,), w.dtype))
        else:
            while j < n and src[j] == src[j - 1] + 1:
                j += 1
            pieces.append(w[..., int(src[i]):int(src[i]) + (j - i)])
        i = j
    return jnp.concatenate(pieces, axis=-1)


def _prepare_weights(p, n_mod_rows):
    w = {"n_mod_rows": n_mod_rows}
    w["w_in"] = _gather_cols(p["w_in"], _w_in_columns()).astype(BF16)
    src = np.full((B_HEADS * LANE,), -1, np.int64)
    for hh in range(B_HEADS):
        src[hh * LANE:hh * LANE + B_NOPE + B_ROPE] = hh * (B_NOPE + B_ROPE) + np.arange(B_NOPE + B_ROPE)
    wq = _gather_cols(p["w_q_up"], src)
    w["w_q_up"] = jnp.pad(wq, ((0, 0), (0, 256 - B_Q_RANK), (0, 0))).astype(BF16)
    src = np.full((B_HEADS * LANE,), -1, np.int64)
    srcv = np.zeros((B_HEADS * B_V,), np.int64)
    for hh in range(B_HEADS):
        src[hh * LANE:hh * LANE + B_NOPE] = hh * (B_NOPE + B_V) + np.arange(B_NOPE)
        srcv[hh * B_V:(hh + 1) * B_V] = hh * (B_NOPE + B_V) + B_NOPE + np.arange(B_V)
    w["w_kv_k"] = _gather_cols(p["w_kv_up"], src).astype(BF16)
    w["w_kv_v"] = _gather_cols(p["w_kv_up"], srcv).astype(BF16)
    w["a_q_norm"] = jnp.tile(p["a_q_norm"], (1, 4))[:, None, :]
    w["a_k_norm"] = jnp.tile(p["a_k_norm"], (1, 4))[:, None, :]
    w["b_q_norm"] = jnp.pad(p["b_q_norm"], ((0, 0), (0, 256 - B_Q_RANK)))[:, None, :]
    w["b_kv_norm"] = p["b_kv_norm"][:, None, :]
    w["c_out_norm"] = jnp.tile(p["c_out_norm"], (1, C_HEADS))[:, None, :]
    for name in ("g_pre_mix", "g_post_mix", "g_pre_ffn", "g_post_ffn"):
        w[name] = p[name][:, None, :]
    head = np.arange(256) // HEAD_DIM
    same_head = head[:, None] == head[None, :]
    w["block_ones"] = jnp.asarray(same_head / float(HEAD_DIM), BF16)
    w["block_mask"] = jnp.asarray(same_head, F32)
    p_lb = jax.nn.softmax(p["c_lower_bounds"].astype(F32), axis=0)
    w["lower"] = jnp.cumsum(p_lb, axis=0) - p_lb[:1]
    w["w_out_a"] = p["w_out"][:, :A_OUT].astype(BF16)
    w["w_out_b"] = p["w_out"][:, A_OUT:A_OUT + B_OUT].astype(BF16)
    w["w_out_c"] = p["w_out"][:, A_OUT + B_OUT:].astype(BF16)
    w["w_ff1"] = p["w_ff1"].astype(BF16)
    w["w_ff2"] = p["w_ff2"].astype(BF16)
    return w


def _rope_tables(n_tok):
    tok = np.arange(n_tok)
    row = (tok // GRID_W).astype(np.float32)[:, None]
    col = (tok % GRID_W).astype(np.float32)[:, None]

    def angles(rot_dim):
        n_freq = rot_dim // 4
        inv = jnp.asarray(ROPE_THETA, F32) ** (-jnp.arange(n_freq, dtype=F32) / n_freq)
        ang = jnp.concatenate([jnp.asarray(row) * inv, jnp.asarray(col) * inv], axis=-1)
        return jnp.cos(ang), jnp.sin(ang)

    zeros = lambda n: jnp.zeros((n_tok, n), F32)
    ones = lambda n: jnp.ones((n_tok, n), F32)
    cos, sin = angles(HEAD_DIM)
    ca = jnp.concatenate([cos, cos, cos, cos], axis=-1)
    s1a = jnp.concatenate([-sin, zeros(32), -sin, zeros(32)], axis=-1)
    s2a = jnp.concatenate([zeros(32), sin, zeros(32), sin], axis=-1)
    cos, sin = angles(B_ROPE)
    cb = jnp.concatenate([ones(64), cos, cos, ones(32)], axis=-1)
    s1b = jnp.concatenate([zeros(64), -sin, zeros(16), zeros(32)], axis=-1)
    s2b = jnp.concatenate([zeros(64), zeros(16), sin, zeros(32)], axis=-1)
    return ca, s1a, s2a, cb, s1b, s2b


def kernel(x, c, ctx, c_ctx, w_ada, b_ada, g_pre_mix, g_post_mix, g_pre_ffn, g_post_ffn, w_in, a_q_norm, a_k_norm, b_q_norm, w_q_up, b_kv_norm, w_kv_up, c_lower_bounds, c_out_norm, w_out, w_ff1, w_ff2):
    bsz, n_lat, _ = x.shape
    depth = w_in.shape[0]
    n_mod_rows = -(-(bsz + 1) // 8) * 8
    params = dict(w_in=w_in, a_q_norm=a_q_norm, a_k_norm=a_k_norm, b_q_norm=b_q_norm, w_q_up=w_q_up,
                  b_kv_norm=b_kv_norm, w_kv_up=w_kv_up, c_lower_bounds=c_lower_bounds,
                  c_out_norm=c_out_norm, w_out=w_out, w_ff1=w_ff1, w_ff2=w_ff2,
                  g_pre_mix=g_pre_mix, g_post_mix=g_post_mix, g_pre_ffn=g_pre_ffn, g_post_ffn=g_post_ffn)
    w = _prepare_weights(params, n_mod_rows)
    rope = _rope_tables(n_lat)
    scan_f = _scan_constants(False)
    scan_b = _scan_constants(True)

    cvec = jnp.concatenate([c, c_ctx[None, :], jnp.zeros((n_mod_rows - bsz - 1, D_MODEL), F32)], axis=0)
    mod = _ada(cvec, w_ada, b_ada).reshape(depth * n_mod_rows * 6, 1, D_MODEL)

    lat_row = lambda b: b
    ctx_row = lambda b: bsz
    zero_state = jnp.zeros((bsz, C_W, C_W), F32)
    attn_a = functools.partial(_attention, n_kv=A_KV_HEADS, n_heads=A_HEADS // A_KV_HEADS, shared_kv=True)
    attn_b = functools.partial(_attention, n_kv=1, n_heads=B_HEADS, shared_kv=False)

    xc = ctx
    for layer in range(depth):
        need_ctx = layer < depth - 1
        fl = _features(x, mod, w, layer, lat_row, rope)
        fc = _features(xc, mod, w, layer, ctx_row, None)
        bound_a = ((HEAD_DIM ** 0.5 * LOG2_E) * jnp.max(jnp.abs(a_q_norm[layer]))
                   * jnp.max(jnp.abs(a_k_norm[layer])))
        a_srcs = [(fl["ak"], fl["avt"]), (fc["ak"], fc["avt"])]
        yta = lax.cond(
            bound_a <= SHIFT_LIMIT,
            lambda q, s, bnd: attn_a(q, s, name="attn_a_fixed", shift=bnd),
            lambda q, s, bnd: attn_a(q, s, name="attn_a"),
            fl["aq"], a_srcs, bound_a)
        k2_max = jnp.maximum(jnp.max(fl["bkn"][:, :, 0, :B_HEADS], axis=1),
                             jnp.max(fc["bkn"][:, :, 0, :B_HEADS], axis=1))
        bound_b = jnp.sqrt(fl["bqn"][:, :B_HEADS, :] * k2_max[:, :, None]) * NORM_SLACK
        b_srcs = [(fl["bk"], fl["bvt"]), (fc["bk"], fc["bvt"])]
        ytb = lax.cond(
            jnp.max(bound_b) <= SHIFT_LIMIT,
            lambda q, s, bnd: attn_b(q, s, name="attn_b_fixed", shift=bnd[:, None]),
            lambda q, s, bnd: attn_b(q, s, name="attn_b"),
            fl["bq"], b_srcs, bound_b)
        ocf, s_f = _hgrn(fc["c"], fc["g"], zero_state, scan_f, w["block_mask"], False)
        ocb, s_b = _hgrn(fc["c"], fc["g"], zero_state, scan_b, w["block_mask"], True)
        olf, _ = _hgrn(fl["c"], fl["g"], s_f, scan_f, w["block_mask"], False)
        olb, _ = _hgrn(fl["c"], fl["g"], s_b, scan_b, w["block_mask"], True)
        x_new = _post(x, yta, ytb, olf, olb, fl["c"], mod, w, layer, lat_row)
        if need_ctx:
            yta_c = attn_a(fc["aq"], [(fc["ak"], fc["avt"])], name="attn_a_ctx")
            ytb_c = attn_b(fc["bq"], [(fc["bk"], fc["bvt"])], name="attn_b_ctx")
            xc = _post(xc, yta_c, ytb_c, ocf, ocb, fc["c"], mod, w, layer, ctx_row)
        x = x_new
    return x
```

```python
import functools

import numpy as np
import jax
import jax.numpy as jnp
from jax import lax
from jax.experimental import pallas as pl
from jax.experimental.pallas import tpu as pltpu

F32 = jnp.float32
BF16 = jnp.bfloat16

D_MODEL = 1024
GRID_W = 64
HEAD_DIM = 64
A_HEADS = 8
A_KV_HEADS = 2
B_HEADS = 4
B_Q_RANK = 192
B_KV_RANK = 128
B_NOPE = 64
B_ROPE = 32
B_V = 64
C_HEADS = 4
C_DK = 64
C_DV = 64
D_FF = 4 * D_MODEL
A_OUT = A_HEADS * HEAD_DIM
B_OUT = B_HEADS * B_V
C_OUT = C_HEADS * C_DV
C_W = C_HEADS * C_DK
ROPE_THETA = 10000.0
EPS = 1e-6
F_TINY = 1e-30

LANE = 128
VMEM_LIMIT = 56 * 1024 * 1024

OFF_AQ = 0
OFF_AK = 512
OFF_AV = 768
OFF_BQD = 896
OFF_BKVD = 1152
OFF_KPE = 1280
OFF_CQ = 1408
OFF_CFF = 1664
OFF_CFB = 1920
OFF_CI = 2176
OFF_CG = 2432
N_COL = 2688

C_Q, C_KF, C_KB, C_V, C_GATE = range(5)
N_SLAB = 5

SCAN_C = 64
N_LEVELS = 6
SCAN_BLOCK = 2048
SCAN_GROUP = 8


def _dot(a, b):
    return jnp.dot(a, b, preferred_element_type=F32)


def _dot_nt(a, b):
    return lax.dot_general(a, b, (((1,), (1,)), ((), ())), preferred_element_type=F32)


def _dot_tn(a, b):
    return lax.dot_general(a, b, (((0,), (0,)), ((), ())), preferred_element_type=F32)


def _sigmoid_pair(z):
    e = jnp.exp(-jnp.abs(z))
    inv = 1.0 / (1.0 + e)
    small = e * inv
    pos = z >= 0
    return jnp.where(pos, inv, small), jnp.where(pos, small, inv)


def _silu(z):
    s, _ = _sigmoid_pair(z)
    return z * s


def _const_spec(shape, index):
    return pl.BlockSpec(shape, lambda *_: index, pipeline_mode=pl.Buffered(1))


def _params(sem):
    return pltpu.CompilerParams(dimension_semantics=sem, vmem_limit_bytes=VMEM_LIMIT)


def _ada_kernel(c_ref, w_ref, b_ref, o_ref):
    a = _silu(c_ref[...])
    w = w_ref[...]
    a_hi = a.astype(BF16)
    a_lo = (a - a_hi.astype(F32)).astype(BF16)
    w_hi = w.astype(BF16)
    w_lo = (w - w_hi.astype(F32)).astype(BF16)
    acc = _dot(a_hi, w_hi) + (_dot(a_hi, w_lo) + _dot(a_lo, w_hi))
    o_ref[...] = acc + b_ref[...]


def _ada(cvec, w_ada, b_ada):
    depth = w_ada.shape[0]
    rows = cvec.shape[0]
    n_blk = w_ada.shape[2] // D_MODEL
    return pl.pallas_call(
        _ada_kernel,
        out_shape=jax.ShapeDtypeStruct((depth, rows, n_blk * D_MODEL), F32),
        grid=(depth, n_blk),
        in_specs=[
            pl.BlockSpec((rows, D_MODEL), lambda l, j: (0, 0)),
            pl.BlockSpec((None, D_MODEL, D_MODEL), lambda l, j: (l, 0, j)),
            pl.BlockSpec((None, None, 1, D_MODEL), lambda l, j: (l, j, 0, 0)),
        ],
        out_specs=pl.BlockSpec((None, rows, D_MODEL), lambda l, j: (l, 0, j)),
        compiler_params=_params(("arbitrary", "arbitrary")),
        name="ada",
    )(cvec, w_ada, b_ada.reshape(depth, n_blk, 1, D_MODEL))


def _rope128(v, cos, s1, s2, half):
    up = pltpu.roll(v, LANE - half, 1)
    dn = pltpu.roll(v, half, 1)
    return v * cos + up * s1 + dn * s2


def _head_rms(v, bo, gain):
    sq = v * v
    hi = sq.astype(BF16)
    lo = (sq - hi.astype(F32)).astype(BF16)
    ms = _dot(hi, bo) + _dot(lo, bo)
    return v * lax.rsqrt(ms + EPS) * gain


FEAT_TM = 1024
FEAT_SUB = 256
KV_CHUNK = 512


def _feat_kernel(*refs, use_rope):
    (x_ref, sh_ref, sc_ref, gpre_ref, win_ref, wq_ref, wkk_ref, wkv_ref,
     aqn_ref, akn_ref, bqn_ref, bkvn_ref, bo_ref, lb_ref) = refs[:14]
    rest = refs[14:]
    if use_rope:
        ca_ref, s1a_ref, s2a_ref, cb_ref, s1b_ref, s2b_ref = rest[:6]
        rest = rest[6:]
    aq_o, ak_o, avt_o, bq_o, bk_o, bvt_o, c_o, g_o, bqn_o, bkn_o = rest
    bo = bo_ref[...]
    tm = x_ref.shape[0]
    sub = min(tm, FEAT_SUB)
    b_k2 = [None] * B_HEADS
    bqn_o[B_HEADS:, :] = jnp.zeros((bqn_o.shape[0] - B_HEADS, tm), F32)

    for r0 in range(0, tm, sub):
        rs = slice(r0, r0 + sub)

        def rope_a(v, rs=rs):
            if not use_rope:
                return v
            return _rope128(v, ca_ref[rs, :], s1a_ref[rs, :], s2a_ref[rs, :], HEAD_DIM // 2)

        def rope_b(v, rs=rs):
            if not use_rope:
                return v
            return _rope128(v, cb_ref[rs, :], s1b_ref[rs, :], s2b_ref[rs, :], B_ROPE // 2)

        x = x_ref[rs, :]
        ms = jnp.mean(x * x, axis=-1, keepdims=True)
        h = x * lax.rsqrt(ms + EPS) * gpre_ref[...]
        h = h * (1.0 + sc_ref[...]) + sh_ref[...]
        p = _dot(h.astype(BF16), win_ref[...])

        aqn = aqn_ref[...]
        for half in range(2):
            v = _head_rms(p[:, OFF_AQ + 256 * half:OFF_AQ + 256 * (half + 1)], bo, aqn)
            for s in range(2):
                blk = rope_a(v[:, LANE * s:LANE * (s + 1)]) * (HEAD_DIM ** -0.5 * LOG2_E)
                lo = 256 * half + LANE * s
                aq_o[lo:lo + LANE, rs] = blk.T.astype(BF16)
        v = _head_rms(p[:, OFF_AK:OFF_AK + 256], bo, akn_ref[...])
        for s in range(2):
            ak_o[rs, LANE * s:LANE * (s + 1)] = rope_a(v[:, LANE * s:LANE * (s + 1)]).astype(BF16)
        kv_w = avt_o.shape[2]
        cs = slice(r0 % kv_w, r0 % kv_w + sub)
        avt_o[r0 // kv_w, :, cs] = p[:, OFF_AV:OFF_AV + LANE].T.astype(BF16)

        bqd = p[:, OFF_BQD:OFF_BQD + 256]
        ms = jnp.sum(bqd * bqd, axis=-1, keepdims=True) * (1.0 / B_Q_RANK)
        qn = (bqd * lax.rsqrt(ms + EPS) * bqn_ref[...]).astype(BF16)
        bq = _dot(qn, wq_ref[...])
        bkvd = p[:, OFF_BKVD:OFF_BKVD + LANE]
        ms = jnp.mean(bkvd * bkvd, axis=-1, keepdims=True)
        kvn = (bkvd * lax.rsqrt(ms + EPS) * bkvn_ref[...]).astype(BF16)
        bkn = _dot(kvn, wkk_ref[...])
        bv = _dot(kvn, wkv_ref[...])
        kpe = rope_b(p[:, OFF_KPE:OFF_KPE + LANE])
        b_scale = (B_NOPE + B_ROPE) ** -0.5 * LOG2_E
        for hh in range(B_HEADS):
            sl = slice(LANE * hh, LANE * (hh + 1))
            qt = (rope_b(bq[:, sl]) * b_scale).T
            kh = bkn[:, sl] + kpe
            bq_o[sl, rs] = qt.astype(BF16)
            bk_o[rs, sl] = kh.astype(BF16)
            bqn_o[hh:hh + 1, rs] = jnp.sum(qt * qt, axis=0, keepdims=True)
            k2 = jnp.max(jnp.sum(kh * kh, axis=-1, keepdims=True), axis=0, keepdims=True)
            b_k2[hh] = k2 if b_k2[hh] is None else jnp.maximum(b_k2[hh], k2)
        bvt_o[r0 // kv_w, :, cs] = bv.T.astype(BF16)

        c_o[rs, C_W * C_Q:C_W * (C_Q + 1)] = _silu(p[:, OFF_CQ:OFF_CQ + C_W])
        for d, (off, ck) in enumerate(((OFF_CFF, C_KF), (OFF_CFB, C_KB))):
            lb = lb_ref[d:d + 1, :]
            sp, sn = _sigmoid_pair(p[:, off:off + C_W])
            f = lb + (1.0 - lb) * sp
            g2 = jnp.log2(jnp.maximum(f, F_TINY))
            hi = g2.astype(BF16)
            g_o[rs, C_W * 2 * d:C_W * (2 * d + 1)] = hi
            g_o[rs, C_W * (2 * d + 1):C_W * (2 * d + 2)] = (g2 - hi.astype(F32)).astype(BF16)
            c_o[rs, C_W * ck:C_W * (ck + 1)] = (1.0 - lb) * sn
        c_o[rs, C_W * C_V:C_W * (C_V + 1)] = p[:, OFF_CI:OFF_CI + C_W]
        c_o[rs, C_W * C_GATE:C_W * (C_GATE + 1)] = p[:, OFF_CG:OFF_CG + C_W]

    lane = lax.broadcasted_iota(jnp.int32, bkn_o.shape, 1)
    row = jnp.zeros(bkn_o.shape, F32)
    for hh in range(B_HEADS):
        row = jnp.where(lane == hh, jnp.broadcast_to(b_k2[hh], bkn_o.shape), row)
    bkn_o[...] = row


def _features(xs, mod, w, layer, mod_row, rope):
    bsz, t, _ = xs.shape
    tm = min(FEAT_TM, t)
    nt = t // tm
    kv_w = min(KV_CHUNK, t)
    n_kv = tm // kv_w
    nb_rows = w["n_mod_rows"]

    def mod_spec(j):
        return pl.BlockSpec((None, 1, D_MODEL),
                            lambda b, i: ((layer * nb_rows + mod_row(b)) * 6 + j, 0, 0))

    in_specs = [
        pl.BlockSpec((None, tm, D_MODEL), lambda b, i: (b, i, 0)),
        mod_spec(0), mod_spec(1),
        _const_spec((None, 1, D_MODEL), (layer, 0, 0)),
        _const_spec((None, D_MODEL, N_COL), (layer, 0, 0)),
        _const_spec((None, 256, 512), (layer, 0, 0)),
        _const_spec((None, LANE, 512), (layer, 0, 0)),
        _const_spec((None, LANE, 256), (layer, 0, 0)),
        _const_spec((None, 1, 256), (layer, 0, 0)),
        _const_spec((None, 1, 256), (layer, 0, 0)),
        _const_spec((None, 1, 256), (layer, 0, 0)),
        _const_spec((None, 1, LANE), (layer, 0, 0)),
        _const_spec((256, 256), (0, 0)),
        _const_spec((None, 2, C_W), (layer, 0, 0)),
    ]
    args = [xs, mod, mod, w["g_pre_mix"], w["w_in"], w["w_q_up"], w["w_kv_k"], w["w_kv_v"],
            w["a_q_norm"], w["a_k_norm"], w["b_q_norm"], w["b_kv_norm"], w["block_ones"], w["lower"]]
    if rope is not None:
        in_specs += [pl.BlockSpec((tm, LANE), lambda b, i: (i, 0))] * 6
        args += list(rope)
    out_shape = [
        jax.ShapeDtypeStruct((bsz, 512, t), BF16),
        jax.ShapeDtypeStruct((bsz, t, 256), BF16),
        jax.ShapeDtypeStruct((bsz, t // kv_w, LANE, kv_w), BF16),
        jax.ShapeDtypeStruct((bsz, 512, t), BF16),
        jax.ShapeDtypeStruct((bsz, t, 512), BF16),
        jax.ShapeDtypeStruct((bsz, t // kv_w, 256, kv_w), BF16),
        jax.ShapeDtypeStruct((bsz, t, N_SLAB * C_W), F32),
        jax.ShapeDtypeStruct((bsz, t, 4 * C_W), BF16),
        jax.ShapeDtypeStruct((bsz, 8, t), F32),
        jax.ShapeDtypeStruct((bsz, nt, 8, LANE), F32),
    ]
    out_specs = [
        pl.BlockSpec((None, 512, tm), lambda b, i: (b, 0, i)),
        pl.BlockSpec((None, tm, 256), lambda b, i: (b, i, 0)),
        pl.BlockSpec((None, n_kv, LANE, kv_w), lambda b, i: (b, i, 0, 0)),
        pl.BlockSpec((None, 512, tm), lambda b, i: (b, 0, i)),
        pl.BlockSpec((None, tm, 512), lambda b, i: (b, i, 0)),
        pl.BlockSpec((None, n_kv, 256, kv_w), lambda b, i: (b, i, 0, 0)),
        pl.BlockSpec((None, tm, N_SLAB * C_W), lambda b, i: (b, i, 0)),
        pl.BlockSpec((None, tm, 4 * C_W), lambda b, i: (b, i, 0)),
        pl.BlockSpec((None, 8, tm), lambda b, i: (b, 0, i)),
        pl.BlockSpec((None, None, 8, LANE), lambda b, i: (b, i, 0, 0)),
    ]
    outs = pl.pallas_call(
        functools.partial(_feat_kernel, use_rope=rope is not None),
        out_shape=out_shape, grid=(bsz, nt), in_specs=in_specs, out_specs=out_specs,
        compiler_params=_params(("parallel", "parallel")),
        name="feat_rope" if rope is not None else "feat_ctx",
    )(*args)
    return dict(zip(("aq", "ak", "avt", "bq", "bk", "bvt", "c", "g", "bqn", "bkn"), outs))


ACC_ROWS = B_V + 16
NEG_BIG = -1e30
LOG2_E = 1.4426950408889634
ATTN_TQ = 1024
SHIFT_LIMIT = 50.0
NORM_SLACK = 1.02


def _attn_kernel(*refs, n_src, n_heads, shared_kv, fixed_shift):
    if fixed_shift:
        shift_ref = refs[0]
        shift = jnp.concatenate([shift_ref[h:h + 1, :] for h in range(n_heads)], axis=1)
        refs = refs[1:]
    q_ref = refs[0]
    src_refs = refs[1:1 + 2 * n_src]
    o_ref = refs[1 + 2 * n_src]
    if fixed_shift:
        rhs_scr, acc_scr, st_scr = refs[2 + 2 * n_src:]
    else:
        rhs_scr, acc_scr, st_scr, m_scr, mx_scr = refs[2 + 2 * n_src:]
    tq = q_ref.shape[1]
    n_grp, k_dim, grp_w = rhs_scr.shape
    heads_per_grp = n_heads // n_grp

    if shared_kv:
        for h in range(n_heads):
            rhs_scr[0, :, tq * h:tq * (h + 1)] = q_ref[HEAD_DIM * h:HEAD_DIM * (h + 1), :]
    else:
        rhs_scr[...] = jnp.zeros(rhs_scr.shape, BF16)
        for h in range(n_heads):
            g, j = divmod(h, heads_per_grp)
            rhs_scr[g, LANE * j:LANE * (j + 1), tq * j:tq * (j + 1)] = q_ref[LANE * h:LANE * (h + 1), :]
    if not fixed_shift:
        m_scr[...] = jnp.full(m_scr.shape, NEG_BIG, F32)
    acc_scr[...] = jnp.zeros(acc_scr.shape, F32)

    def stage(s, c, slot):
        k_ref, vt_ref = src_refs[2 * s], src_refs[2 * s + 1]
        tk = vt_ref.shape[2]
        rows = pl.ds(pl.multiple_of(c * tk, tk), tk)
        for g in range(n_grp):
            cols = slice(grp_w * g, grp_w * (g + 1))
            kc = k_ref[rows, 0:k_dim] if shared_kv else k_ref[rows, k_dim * g:k_dim * (g + 1)]
            st = _dot(kc, rhs_scr[g])
            if fixed_shift:
                st_scr[slot, 0:tk, cols] = jnp.exp2(st - shift[:, cols]).astype(BF16)
            else:
                st_scr[slot, 0:tk, cols] = st
                mx_scr[slot, :, cols] = jnp.max(st, axis=0, keepdims=True)

    def consume(s, c, slot):
        vt_ref = src_refs[2 * s + 1]
        tk = vt_ref.shape[2]
        ones = jnp.ones((ACC_ROWS - B_V, tk), BF16)
        for g in range(n_grp):
            cols = slice(grp_w * g, grp_w * (g + 1))
            if fixed_shift:
                pt = st_scr[slot, 0:tk, cols]
                alpha = None
            else:
                m_old = m_scr[:, cols]
                m_new = jnp.maximum(m_old, mx_scr[slot, :, cols])
                pt = jnp.exp2(st_scr[slot, 0:tk, cols] - m_new).astype(BF16)
                alpha = jnp.exp2(m_old - m_new)
                m_scr[:, cols] = m_new
            if shared_kv:
                vt = jnp.concatenate([vt_ref[c], ones], axis=0)
                prev = acc_scr[:, cols] if fixed_shift else alpha * acc_scr[:, cols]
                acc_scr[:, cols] = prev + _dot(vt, pt)
            else:
                for j in range(heads_per_grp):
                    h = g * heads_per_grp + j
                    hc = slice(tq * h, tq * (h + 1))
                    lc = slice(tq * j, tq * (j + 1))
                    vt = jnp.concatenate([vt_ref[c, B_V * h:B_V * (h + 1), :], ones], axis=0)
                    prev = acc_scr[:, hc] if fixed_shift else alpha[:, lc] * acc_scr[:, hc]
                    acc_scr[:, hc] = prev + _dot(vt, pt[:, lc])

    n0 = src_refs[1].shape[0]
    n_loop = (n0 - 2) // 2 if n0 >= 4 else 0
    stage(0, 0, 0)
    if n_loop:
        def pair(i, carry):
            stage(0, 2 * i + 1, 1)
            consume(0, 2 * i, 0)
            stage(0, 2 * i + 2, 0)
            consume(0, 2 * i + 1, 1)
            return carry
        lax.fori_loop(0, n_loop, pair, 0)
    tail = [(0, c) for c in range(2 * n_loop, n0)]
    tail += [(s, c) for s in range(1, n_src) for c in range(src_refs[2 * s + 1].shape[0])]
    for i, (s, c) in enumerate(tail):
        if i + 1 < len(tail):
            stage(*tail[i + 1], (i + 1) % 2)
        consume(s, c, i % 2)

    for h in range(n_heads):
        acc = acc_scr[:, tq * h:tq * (h + 1)]
        o_ref[B_V * h:B_V * (h + 1), :] = (acc[:B_V] * (1.0 / acc[B_V:B_V + 1])).astype(BF16)


def _attention(q, srcs, *, n_kv, n_heads, shared_kv, name, shift=None):
    fixed_shift = shift is not None
    bsz, _, t = q.shape
    tq = min(ATTN_TQ * (2 if shared_kv else 1) * (2 if fixed_shift else 1), t)
    q_r = (HEAD_DIM if shared_kv else LANE) * n_heads
    k_w = LANE * (1 if shared_kv else n_heads)
    v_r = B_V * (1 if shared_kv else n_heads)
    rhs_shape = (1, HEAD_DIM, n_heads * tq) if shared_kv else (n_heads // 2, 2 * LANE, 2 * tq)
    in_specs = [pl.BlockSpec((None, q_r, tq), lambda b, g, i: (b, g, i))]
    args = [q]
    for k, vt in srcs:
        length = k.shape[1]
        n_chunk, tk = vt.shape[1], vt.shape[3]
        in_specs.append(pl.BlockSpec((None, length, k_w), lambda b, g, i: (b, 0, g)))
        in_specs.append(pl.BlockSpec((None, n_chunk, v_r, tk), lambda b, g, i: (b, 0, g, 0)))
        args += [k, vt]
    rows = B_V * n_heads
    width = n_heads * tq
    tk_max = max(vt.shape[3] for _, vt in srcs)
    scratch = [pltpu.VMEM(rhs_shape, BF16), pltpu.VMEM((ACC_ROWS, width), F32)]
    if fixed_shift:
        if shift.ndim == 0:
            shift = jnp.broadcast_to(shift, (1, 1, n_heads, tq))
            in_specs = [pl.BlockSpec((None, None, n_heads, tq), lambda b, g, i: (0, 0, 0, 0))] + in_specs
        else:
            in_specs = [pl.BlockSpec((None, None, n_heads, tq), lambda b, g, i: (b, g, 0, i))] + in_specs
        args = [shift] + args
        scratch += [pltpu.VMEM((2, tk_max, width), BF16)]
    else:
        scratch += [pltpu.VMEM((2, tk_max, width), F32), pltpu.VMEM((1, width), F32),
                    pltpu.VMEM((2, 1, width), F32)]
    return pl.pallas_call(
        functools.partial(_attn_kernel, n_src=len(srcs), n_heads=n_heads, shared_kv=shared_kv,
                          fixed_shift=fixed_shift),
        out_shape=jax.ShapeDtypeStruct((bsz, rows * n_kv, t), BF16),
        grid=(bsz, n_kv, t // tq),
        in_specs=in_specs,
        out_specs=pl.BlockSpec((None, rows, tq), lambda b, g, i: (b, g, i)),
        scratch_shapes=scratch,
        compiler_params=_params(("parallel", "parallel", "parallel")),
        name=name,
    )(*args)


def _scan_constants(reverse):
    c = SCAN_C
    t = np.arange(c)[:, None]
    s = np.arange(c)[None, :]
    cum = (s >= t) if reverse else (s <= t)
    msk = np.zeros((N_LEVELS + 1, c, c), np.float32)
    for lvl in range(N_LEVELS):
        w = (c // 2) >> lvl
        same = (t // (2 * w)) == (s // (2 * w))
        t_hi = (t % (2 * w)) >= w
        s_hi = (s % (2 * w)) >= w
        msk[lvl] = (same & ~t_hi & s_hi) if reverse else (same & t_hi & ~s_hi)
    msk[N_LEVELS] = (t == s)
    return jnp.asarray(cum, BF16), jnp.asarray(np.tile(msk, (1, 1, C_HEADS)), F32)


def _boundary_rows(b, lvl, reverse):
    w = (SCAN_C // 2) >> lvl
    width = b.shape[1]
    off = w if reverse else w - 1
    if w == 1:
        odd = lax.broadcasted_iota(jnp.int32, b.shape, 0) % 2 == 1
        if reverse:
            return jnp.where(odd, b, pltpu.roll(b, SCAN_C - 1, 0))
        return jnp.where(odd, pltpu.roll(b, 1, 0), b)
    if w == 2:
        low = lax.broadcasted_iota(jnp.int32, (8, width), 0) < 4
        pieces = []
        for r0 in range(0, SCAN_C, 8):
            first = jnp.broadcast_to(b[r0 + off:r0 + off + 1, :], (8, width))
            second = jnp.broadcast_to(b[r0 + 4 + off:r0 + 5 + off, :], (8, width))
            pieces.append(jnp.where(low, first, second))
        return jnp.concatenate(pieces, axis=0)
    pieces = [jnp.broadcast_to(b[r0 + off:r0 + off + 1, :], (2 * w, width))
              for r0 in range(0, SCAN_C, 2 * w)]
    return pieces[0] if len(pieces) == 1 else jnp.concatenate(pieces, axis=0)


def _hgrn_kernel(q_ref, k_ref, ghi_ref, glo_ref, v_ref, s0_ref, cum_ref, msk_ref, bm_ref,
                 o_ref, sfin_ref, st_ref, kt_ref, *, reverse, n_chunk):
    i = pl.program_id(1)

    @pl.when(i == 0)
    def _():
        st_ref[...] = s0_ref[...]

    bm = bm_ref[...]
    bm16 = bm.astype(BF16)
    cum = cum_ref[...]

    def stack_heads(a):
        a16 = a.astype(BF16)
        return jnp.concatenate([a16] * C_HEADS, axis=0) * bm16

    grp = min(n_chunk, SCAN_GROUP)
    n_grp = n_chunk // grp

    def group(jg, carry):
        jgg = (n_grp - 1 - jg) if reverse else jg
        base = jgg * (grp * SCAN_C)
        order = list(range(grp - 1, -1, -1)) if reverse else list(range(grp))
        rows = [pl.ds(pl.multiple_of(base + c * SCAN_C, SCAN_C), SCAN_C) for c in range(grp)]
        q = [q_ref[r, :] for r in rows]
        k = [k_ref[r, :] for r in rows]
        v = [v_ref[r, :] for r in rows]
        b = [_dot(cum, ghi_ref[r, :]) + _dot(cum, glo_ref[r, :]) for r in rows]
        b_all = [bc[0:1, :] if reverse else bc[SCAN_C - 1:SCAN_C, :] for bc in b]

        q16 = [qc.astype(BF16) for qc in q]
        kst = [stack_heads(kc) for kc in k]
        diag = _dot(jnp.concatenate([(q[c] * k[c]).astype(BF16) for c in range(grp)], axis=0), bm16)
        sc = [None] * grp
        for lvl in range(N_LEVELS):
            for c in range(grp):
                d = b[c] - _boundary_rows(b[c], lvl, reverse)
                e16 = jnp.exp2(-jnp.abs(d)).astype(BF16)
                kl = kst[c] * jnp.concatenate([e16] * C_HEADS, axis=0)
                kt_ref[c] = kl.T
                term = msk_ref[lvl] * _dot(q16[c] * e16, kt_ref[c])
                sc[c] = term if sc[c] is None else sc[c] + term

        ds = [_dot_tn(v[c].astype(BF16), (k[c] * jnp.exp2(b_all[c] - b[c])).astype(BF16)) * bm
              for c in range(grp)]
        st = st_ref[...]
        st_in = [None] * grp
        for c in order:
            st_in[c] = st
            st = st * jnp.exp2(b_all[c]) + ds[c]
        st_ref[...] = st

        for c in order:
            qdec = q[c] * jnp.exp2(b[c])
            o_ref[rows[c], :] = (_dot(sc[c].astype(BF16), stack_heads(v[c]))
                                 + diag[SCAN_C * c:SCAN_C * (c + 1), :] * v[c]
                                 + _dot_nt(qdec.astype(BF16), st_in[c].astype(BF16)))
        return carry

    if n_grp == 1:
        group(0, 0)
    else:
        lax.fori_loop(0, n_grp, group, 0)

    @pl.when(i == pl.num_programs(1) - 1)
    def _():
        sfin_ref[...] = st_ref[...]


def _hgrn(c_slab, g16, s0, consts, block_mask, reverse):
    bsz, t, _ = c_slab.shape
    tb = min(SCAN_BLOCK, t)
    nblk = t // tb
    cum, msk = consts

    def blk(i):
        return (nblk - 1 - i) if reverse else i

    def slab_spec(j):
        return pl.BlockSpec((None, tb, C_W), lambda b, i: (b, blk(i), j))

    g0 = 2 if reverse else 0

    return pl.pallas_call(
        functools.partial(_hgrn_kernel, reverse=reverse, n_chunk=tb // SCAN_C),
        out_shape=[jax.ShapeDtypeStruct((bsz, t, C_W), F32),
                   jax.ShapeDtypeStruct((bsz, C_W, C_W), F32)],
        grid=(bsz, nblk),
        in_specs=[
            slab_spec(C_Q), slab_spec(C_KB if reverse else C_KF), slab_spec(g0), slab_spec(g0 + 1),
            slab_spec(C_V),
            pl.BlockSpec((None, C_W, C_W), lambda b, i: (b, 0, 0)),
            _const_spec(cum.shape, (0, 0)),
            _const_spec(msk.shape, (0, 0, 0)),
            _const_spec(block_mask.shape, (0, 0)),
        ],
        out_specs=[pl.BlockSpec((None, tb, C_W), lambda b, i: (b, blk(i), 0)),
                   pl.BlockSpec((None, C_W, C_W), lambda b, i: (b, 0, 0))],
        scratch_shapes=[pltpu.VMEM((C_W, C_W), F32),
                        pltpu.VMEM((min(tb // SCAN_C, SCAN_GROUP), C_W, C_HEADS * SCAN_C), BF16)],
        compiler_params=_params(("parallel", "arbitrary")),
        name="hgrn_bwd" if reverse else "hgrn_fwd",
    )(c_slab, c_slab, g16, g16, c_slab, s0, cum, msk, block_mask)


POST_TM = 1024
POST_SUB = 256


def _rms(v, gain):
    ms = jnp.mean(v * v, axis=-1, keepdims=True)
    return v * lax.rsqrt(ms + EPS) * gain


def _post_kernel(x_ref, ya_ref, yb_ref, of_ref, ob_ref, gate_ref,
                 gtm_ref, shf_ref, scf_ref, gtf_ref,
                 gpm_ref, gpf_ref, gqf_ref, con_ref, bo_ref,
                 woa_ref, wob_ref, woc_ref, w1_ref, w2_ref, o_ref):
    tm = x_ref.shape[0]
    sub = min(tm, POST_SUB)
    for r0 in range(0, tm, sub):
        rs = slice(r0, r0 + sub)
        o = of_ref[rs, :] + ob_ref[rs, :]
        yc = (_head_rms(o, bo_ref[...], con_ref[...]) * _silu(gate_ref[rs, :])).astype(BF16)
        mix = (_dot_tn(ya_ref[:, rs], woa_ref[...]) + _dot_tn(yb_ref[:, rs], wob_ref[...])
               + _dot(yc, woc_ref[...]))
        x1 = x_ref[rs, :] + gtm_ref[...] * _rms(mix, gpm_ref[...])
        h = _rms(x1, gpf_ref[...]) * (1.0 + scf_ref[...]) + shf_ref[...]
        u = jnp.maximum(_dot(h.astype(BF16), w1_ref[...]), 0.0)
        ff = _dot((u * u).astype(BF16), w2_ref[...])
        o_ref[rs, :] = x1 + gtf_ref[...] * _rms(ff, gqf_ref[...])


def _post(xs, yta, ytb, o_f, o_b, c_slab, mod, w, layer, mod_row):
    bsz, t, _ = xs.shape
    tm = min(POST_TM, t)
    nb_rows = w["n_mod_rows"]

    def mod_spec(j):
        return pl.BlockSpec((None, 1, D_MODEL),
                            lambda b, i: ((layer * nb_rows + mod_row(b)) * 6 + j, 0, 0))

    def vec_spec():
        return _const_spec((None, 1, D_MODEL), (layer, 0, 0))

    in_specs = [
        pl.BlockSpec((None, tm, D_MODEL), lambda b, i: (b, i, 0)),
        pl.BlockSpec((None, A_OUT, tm), lambda b, i: (b, 0, i)),
        pl.BlockSpec((None, B_OUT, tm), lambda b, i: (b, 0, i)),
        pl.BlockSpec((None, tm, C_W), lambda b, i: (b, i, 0)),
        pl.BlockSpec((None, tm, C_W), lambda b, i: (b, i, 0)),
        pl.BlockSpec((None, tm, C_W), lambda b, i: (b, i, C_GATE)),
        mod_spec(2), mod_spec(3), mod_spec(4), mod_spec(5),
        vec_spec(), vec_spec(), vec_spec(),
        _const_spec((None, 1, C_W), (layer, 0, 0)),
        _const_spec((256, 256), (0, 0)),
        _const_spec((None, A_OUT, D_MODEL), (layer, 0, 0)),
        _const_spec((None, B_OUT, D_MODEL), (layer, 0, 0)),
        _const_spec((None, C_OUT, D_MODEL), (layer, 0, 0)),
        _const_spec((None, D_MODEL, D_FF), (layer, 0, 0)),
        _const_spec((None, D_FF, D_MODEL), (layer, 0, 0)),
    ]
    return pl.pallas_call(
        _post_kernel,
        out_shape=jax.ShapeDtypeStruct((bsz, t, D_MODEL), F32),
        grid=(bsz, t // tm),
        in_specs=in_specs,
        out_specs=pl.BlockSpec((None, tm, D_MODEL), lambda b, i: (b, i, 0)),
        compiler_params=_params(("parallel", "parallel")),
        name="post",
    )(xs, yta, ytb, o_f, o_b, c_slab, mod, mod, mod, mod,
      w["g_post_mix"], w["g_pre_ffn"], w["g_post_ffn"], w["c_out_norm"], w["block_ones"],
      w["w_out_a"], w["w_out_b"], w["w_out_c"], w["w_ff1"], w["w_ff2"])


def _w_in_columns():
    src = np.full((N_COL,), -1, np.int64)
    a_k0 = A_HEADS * HEAD_DIM
    a_v0 = a_k0 + A_KV_HEADS * HEAD_DIM
    b_qd0 = a_v0 + A_KV_HEADS * HEAD_DIM
    b_kv0 = b_qd0 + B_Q_RANK
    b_kr0 = b_kv0 + B_KV_RANK
    c0 = b_kr0 + B_ROPE
    src[OFF_AQ:OFF_AQ + 512] = np.arange(512)
    for g in range(A_KV_HEADS):
        for rep in range(2):
            lo = OFF_AK + (2 * g + rep) * HEAD_DIM
            src[lo:lo + HEAD_DIM] = a_k0 + g * HEAD_DIM + np.arange(HEAD_DIM)
    src[OFF_AV:OFF_AV + 128] = a_v0 + np.arange(128)
    src[OFF_BQD:OFF_BQD + B_Q_RANK] = b_qd0 + np.arange(B_Q_RANK)
    src[OFF_BKVD:OFF_BKVD + B_KV_RANK] = b_kv0 + np.arange(B_KV_RANK)
    src[OFF_KPE + B_NOPE:OFF_KPE + B_NOPE + B_ROPE] = b_kr0 + np.arange(B_ROPE)
    src[OFF_CQ:OFF_CQ + 5 * C_W] = c0 + np.arange(5 * C_W)
    return src


def _gather_cols(w, src):
    pieces, i, n = [], 0, len(src)
    while i < n:
        j = i + 1
        if src[i] < 0:
            while j < n and src[j] < 0:
                j += 1
            pieces.append(jnp.zeros(w.shape[:-1] + (j - i,), w.dtype))
        else:
            while j < n and src[j] == src[j - 1] + 1:
                j += 1
            pieces.append(w[..., int(src[i]):int(src[i]) + (j - i)])
        i = j
    return jnp.concatenate(pieces, axis=-1)


def _prepare_weights(p, n_mod_rows):
    w = {"n_mod_rows": n_mod_rows}
    w["w_in"] = _gather_cols(p["w_in"], _w_in_columns()).astype(BF16)
    src = np.full((B_HEADS * LANE,), -1, np.int64)
    for hh in range(B_HEADS):
        src[hh * LANE:hh * LANE + B_NOPE + B_ROPE] = hh * (B_NOPE + B_ROPE) + np.arange(B_NOPE + B_ROPE)
    wq = _gather_cols(p["w_q_up"], src)
    w["w_q_up"] = jnp.pad(wq, ((0, 0), (0, 256 - B_Q_RANK), (0, 0))).astype(BF16)
    src = np.full((B_HEADS * LANE,), -1, np.int64)
    srcv = np.zeros((B_HEADS * B_V,), np.int64)
    for hh in range(B_HEADS):
        src[hh * LANE:hh * LANE + B_NOPE] = hh * (B_NOPE + B_V) + np.arange(B_NOPE)
        srcv[hh * B_V:(hh + 1) * B_V] = hh * (B_NOPE + B_V) + B_NOPE + np.arange(B_V)
    w["w_kv_k"] = _gather_cols(p["w_kv_up"], src).astype(BF16)
    w["w_kv_v"] = _gather_cols(p["w_kv_up"], srcv).astype(BF16)
    w["a_q_norm"] = jnp.tile(p["a_q_norm"], (1, 4))[:, None, :]
    w["a_k_norm"] = jnp.tile(p["a_k_norm"], (1, 4))[:, None, :]
    w["b_q_norm"] = jnp.pad(p["b_q_norm"], ((0, 0), (0, 256 - B_Q_RANK)))[:, None, :]
    w["b_kv_norm"] = p["b_kv_norm"][:, None, :]
    w["c_out_norm"] = jnp.tile(p["c_out_norm"], (1, C_HEADS))[:, None, :]
    for name in ("g_pre_mix", "g_post_mix", "g_pre_ffn", "g_post_ffn"):
        w[name] = p[name][:, None, :]
    head = np.arange(256) // HEAD_DIM
    same_head = head[:, None] == head[None, :]
    w["block_ones"] = jnp.asarray(same_head / float(HEAD_DIM), BF16)
    w["block_mask"] = jnp.asarray(same_head, F32)
    p_lb = jax.nn.softmax(p["c_lower_bounds"].astype(F32), axis=0)
    w["lower"] = jnp.cumsum(p_lb, axis=0) - p_lb[:1]
    w["w_out_a"] = p["w_out"][:, :A_OUT].astype(BF16)
    w["w_out_b"] = p["w_out"][:, A_OUT:A_OUT + B_OUT].astype(BF16)
    w["w_out_c"] = p["w_out"][:, A_OUT + B_OUT:].astype(BF16)
    w["w_ff1"] = p["w_ff1"].astype(BF16)
    w["w_ff2"] = p["w_ff2"].astype(BF16)
    return w


def _rope_tables(n_tok):
    tok = np.arange(n_tok)
    row = (tok // GRID_W).astype(np.float32)[:, None]
    col = (tok % GRID_W).astype(np.float32)[:, None]

    def angles(rot_dim):
        n_freq = rot_dim // 4
        inv = jnp.asarray(ROPE_THETA, F32) ** (-jnp.arange(n_freq, dtype=F32) / n_freq)
        ang = jnp.concatenate([jnp.asarray(row) * inv, jnp.asarray(col) * inv], axis=-1)
        return jnp.cos(ang), jnp.sin(ang)

    zeros = lambda n: jnp.zeros((n_tok, n), F32)
    ones = lambda n: jnp.ones((n_tok, n), F32)
    cos, sin = angles(HEAD_DIM)
    ca = jnp.concatenate([cos, cos, cos, cos], axis=-1)
    s1a = jnp.concatenate([-sin, zeros(32), -sin, zeros(32)], axis=-1)
    s2a = jnp.concatenate([zeros(32), sin, zeros(32), sin], axis=-1)
    cos, sin = angles(B_ROPE)
    cb = jnp.concatenate([ones(64), cos, cos, ones(32)], axis=-1)
    s1b = jnp.concatenate([zeros(64), -sin, zeros(16), zeros(32)], axis=-1)
    s2b = jnp.concatenate([zeros(64), zeros(16), sin, zeros(32)], axis=-1)
    return ca, s1a, s2a, cb, s1b, s2b


def kernel(x, c, ctx, c_ctx, w_ada, b_ada, g_pre_mix, g_post_mix, g_pre_ffn, g_post_ffn, w_in, a_q_norm, a_k_norm, b_q_norm, w_q_up, b_kv_norm, w_kv_up, c_lower_bounds, c_out_norm, w_out, w_ff1, w_ff2):
    bsz, n_lat, _ = x.shape
    depth = w_in.shape[0]
    n_mod_rows = -(-(bsz + 1) // 8) * 8
    params = dict(w_in=w_in, a_q_norm=a_q_norm, a_k_norm=a_k_norm, b_q_norm=b_q_norm, w_q_up=w_q_up,
                  b_kv_norm=b_kv_norm, w_kv_up=w_kv_up, c_lower_bounds=c_lower_bounds,
                  c_out_norm=c_out_norm, w_out=w_out, w_ff1=w_ff1, w_ff2=w_ff2,
                  g_pre_mix=g_pre_mix, g_post_mix=g_post_mix, g_pre_ffn=g_pre_ffn, g_post_ffn=g_post_ffn)
    w = _prepare_weights(params, n_mod_rows)
    rope = _rope_tables(n_lat)
    scan_f = _scan_constants(False)
    scan_b = _scan_constants(True)

    cvec = jnp.concatenate([c, c_ctx[None, :], jnp.zeros((n_mod_rows - bsz - 1, D_MODEL), F32)], axis=0)
    mod = _ada(cvec, w_ada, b_ada).reshape(depth * n_mod_rows * 6, 1, D_MODEL)

    lat_row = lambda b: b
    ctx_row = lambda b: bsz
    zero_state = jnp.zeros((bsz, C_W, C_W), F32)
    attn_a = functools.partial(_attention, n_kv=A_KV_HEADS, n_heads=A_HEADS // A_KV_HEADS, shared_kv=True)
    attn_b = functools.partial(_attention, n_kv=1, n_heads=B_HEADS, shared_kv=False)

    xc = ctx
    for layer in range(depth):
        need_ctx = layer < depth - 1
        fl = _features(x, mod, w, layer, lat_row, rope)
        fc = _features(xc, mod, w, layer, ctx_row, None)
        bound_a = ((HEAD_DIM ** 0.5 * LOG2_E) * jnp.max(jnp.abs(a_q_norm[layer]))
                   * jnp.max(jnp.abs(a_k_norm[layer])))
        a_srcs = [(fl["ak"], fl["avt"]), (fc["ak"], fc["avt"])]
        yta = lax.cond(
            bound_a <= SHIFT_LIMIT,
            lambda q, s, bnd: attn_a(q, s, name="attn_a_fixed", shift=bnd),
            lambda q, s, bnd: attn_a(q, s, name="attn_a"),
            fl["aq"], a_srcs, bound_a)
        k2_max = jnp.maximum(jnp.max(fl["bkn"][:, :, 0, :B_HEADS], axis=1),
                             jnp.max(fc["bkn"][:, :, 0, :B_HEADS], axis=1))
        bound_b = jnp.sqrt(fl["bqn"][:, :B_HEADS, :] * k2_max[:, :, None]) * NORM_SLACK
        b_srcs = [(fl["bk"], fl["bvt"]), (fc["bk"], fc["bvt"])]
        ytb = lax.cond(
            jnp.max(bound_b) <= SHIFT_LIMIT,
            lambda q, s, bnd: attn_b(q, s, name="attn_b_fixed", shift=bnd[:, None]),
            lambda q, s, bnd: attn_b(q, s, name="attn_b"),
            fl["bq"], b_srcs, bound_b)
        ocf, s_f = _hgrn(fc["c"], fc["g"], zero_state, scan_f, w["block_mask"], False)
        ocb, s_b = _hgrn(fc["c"], fc["g"], zero_state, scan_b, w["block_mask"], True)
        olf, _ = _hgrn(fl["c"], fl["g"], s_f, scan_f, w["block_mask"], False)
        olb, _ = _hgrn(fl["c"], fl["g"], s_b, scan_b, w["block_mask"], True)
        x_new = _post(x, yta, ytb, olf, olb, fl["c"], mod, w, layer, lat_row)
        if need_ctx:
            yta_c = attn_a(fc["aq"], [(fc["ak"], fc["avt"])], name="attn_a_ctx")
            ytb_c = attn_b(fc["bq"], [(fc["bk"], fc["bvt"])], name="attn_b_ctx")
            xc = _post(xc, yta_c, ytb_c, ocf, ocb, fc["c"], mod, w, layer, ctx_row)
        x = x_new
    return x
```
